```python
import jax, jax.numpy as jnp
from jax import lax
import numpy as np

D_MODEL = 2048
BATCH = 2
SEQ = 4096
DEPTH = 1
DEC_BATCH = 32
DEC_SEQ = 4
PAST_LEN = 16384
PAGE_SIZE = 128

GLA_WIDTH = D_MODEL // 2
N_GLA_HEADS = 4
GLA_DV = GLA_WIDTH // N_GLA_HEADS
GLA_DK = GLA_DV // 2
GLA_KEY_WIDTH = N_GLA_HEADS * GLA_DK
GLA_GATE_RANK = 16
GLA_GATE_NORM = 16.0
GLA_CHUNK = 16
ATT_WIDTH = D_MODEL - GLA_WIDTH
N_ATT_HEADS = 8
ATT_HEAD_DIM = ATT_WIDTH // N_ATT_HEADS
DILATED_CONFIGS = ((128, 1), (512, 4), (2048, 16))
MAX_WINDOW = 2048
SUB_WINDOW = 128
ATT_BLOCK = 128
D_FF = 4 * D_MODEL
RMS_EPS = 1e-6
IN_COLS = 2 * GLA_KEY_WIDTH + 2 * GLA_WIDTH + GLA_GATE_RANK + 3 * ATT_WIDTH

kernel_name = 'hymba_gla_dilated_alibi_decode_step'


def rmsnorm(x, w):
    xf = x.astype(jnp.float32)
    r = lax.rsqrt(jnp.mean(xf * xf, axis=-1, keepdims=True) + RMS_EPS)
    return (xf * r).astype(x.dtype) * w


def alibi_slopes():
    h = jnp.arange(1, N_ATT_HEADS + 1, dtype=jnp.float32)
    return jnp.exp2(-8.0 * h / N_ATT_HEADS)


def mixer_inputs(x, attn_norm_w, w_in, w_gk_up, b_gk):
    B, T, _ = x.shape
    h = rmsnorm(x, attn_norm_w)
    proj = h @ w_in
    sizes = [GLA_KEY_WIDTH, GLA_KEY_WIDTH, GLA_WIDTH, GLA_WIDTH, GLA_GATE_RANK, ATT_WIDTH, ATT_WIDTH]
    cuts = [int(c) for c in np.cumsum(sizes)]
    gq, gk, gv, gg, glr, aq, ak, av = jnp.split(proj, cuts, axis=-1)
    log_a = jax.nn.log_sigmoid((glr @ w_gk_up + b_gk).astype(jnp.float32)) / GLA_GATE_NORM
    gla = (gq.reshape(B, T, N_GLA_HEADS, GLA_DK) * (GLA_DK ** -0.5),
           gk.reshape(B, T, N_GLA_HEADS, GLA_DK),
           gv.reshape(B, T, N_GLA_HEADS, GLA_DV),
           log_a.reshape(B, T, N_GLA_HEADS, GLA_DK))
    att = tuple(a.reshape(B, T, N_ATT_HEADS, ATT_HEAD_DIM) for a in (aq, ak, av))
    return gla, gg, att


def gla_chunked(q, k, v, log_a, s0):
    f32 = jnp.float32
    B, T = q.shape[0], q.shape[1]
    nc = -(-T // GLA_CHUNK)
    pad = nc * GLA_CHUNK - T
    def blocks(a):
        a = jnp.pad(a.astype(f32), ((0, 0), (0, pad), (0, 0), (0, 0)))
        return a.reshape(B, nc, GLA_CHUNK, a.shape[2], a.shape[3]).transpose(1, 0, 3, 2, 4)
    qc, kc, vc, gc = blocks(q), blocks(k), blocks(v), blocks(log_a)
    bc = jnp.cumsum(gc, axis=3)
    causal = jnp.tril(jnp.ones((GLA_CHUNK, GLA_CHUNK), dtype=bool))[:, :, None]

    def step(S, inp):
        qb, kb, vb, bb = inp
        b_last = bb[:, :, -1, :]
        o_inter = jnp.einsum('bhck,bhkv->bhcv', qb * jnp.exp(bb), S)
        diff = bb[:, :, :, None, :] - bb[:, :, None, :, :]
        decay = jnp.exp(jnp.where(causal, diff, -jnp.inf))
        A = jnp.einsum('bhtk,bhsk,bhtsk->bhts', qb, kb, decay)
        o_intra = jnp.einsum('bhts,bhsv->bhtv', A, vb)
        S_new = jnp.exp(b_last)[..., None] * S + jnp.einsum(
            'bhsk,bhsv->bhkv', kb * jnp.exp(b_last[:, :, None, :] - bb), vb)
        return S_new, o_inter + o_intra

    S_fin, o = lax.scan(step, s0.astype(f32), (qc, kc, vc, bc))
    o = o.transpose(1, 0, 3, 2, 4).reshape(B, nc * GLA_CHUNK, N_GLA_HEADS, GLA_DV)[:, :T]
    return o, S_fin


def strided_window_attention_prompt(q, k, v, dil, slopes):
    B, S, H, E = q.shape
    L = S // dil
    nb = -(-L // ATT_BLOCK)
    Lp = nb * ATT_BLOCK
    Bd = B * dil
    def sub(a):
        a = a.reshape(B, L, dil, H, E).transpose(0, 2, 1, 3, 4).reshape(Bd, L, H, E)
        return jnp.pad(a, ((0, 0), (0, Lp - L), (0, 0), (0, 0)))
    qs, ks, vs = sub(q), sub(k), sub(v)
    qb = qs.reshape(Bd, nb, ATT_BLOCK, H, E)
    def band(a):
        prev = jnp.pad(a, ((0, 0), (ATT_BLOCK, 0), (0, 0), (0, 0)))[:, :Lp]
        return jnp.concatenate([prev.reshape(Bd, nb, ATT_BLOCK, H, E),
                                a.reshape(Bd, nb, ATT_BLOCK, H, E)], axis=2)
    kb, vb = band(ks), band(vs)
    s = jnp.einsum('bnqhe,bnkhe->bnhqk', qb, kb,
                   preferred_element_type=jnp.float32) * (E ** -0.5)
    qi = jnp.arange(ATT_BLOCK)[:, None]
    ki = jnp.arange(2 * ATT_BLOCK)[None, :]
    dist = qi - ki + ATT_BLOCK
    key_idx = jnp.arange(nb)[:, None, None] * ATT_BLOCK - ATT_BLOCK + ki[None]
    valid = (dist >= 0)[None] & (dist <= SUB_WINDOW)[None] & (key_idx >= 0)
    bias = -slopes[:, None, None] * (dil * dist).astype(jnp.float32)[None]
    s = jnp.where(valid[None, :, None], s + bias[None, None], -jnp.inf)
    lse = jax.nn.logsumexp(s, axis=-1)
    p = jnp.exp(s - lse[..., None])
    o = jnp.einsum('bnhqk,bnkhe->bnqhe', p.astype(v.dtype), vb)
    o = o.reshape(Bd, Lp, H, E)[:, :L].reshape(B, dil, L, H, E).transpose(0, 2, 1, 3, 4).reshape(B, S, H, E)
    lse = lse.transpose(0, 1, 3, 2).reshape(Bd, Lp, H)[:, :L].reshape(B, dil, L, H).transpose(0, 2, 1, 3).reshape(B, S, H)
    return o, lse


def strided_window_attention_sample(q, k_all, v_all, w_cache, dil, slopes):
    B, T, H, E = q.shape
    J = SUB_WINDOW + 1
    j = jnp.arange(J)
    idx = w_cache + jnp.arange(T)[:, None] - dil * j[None, :]
    valid = idx >= 0
    idx_c = jnp.clip(idx, 0, None).reshape(-1)
    kg = jnp.take(k_all, idx_c, axis=1).reshape(B, T, J, H, E)
    vg = jnp.take(v_all, idx_c, axis=1).reshape(B, T, J, H, E)
    s = jnp.einsum('bthe,btjhe->bhtj', q, kg,
                   preferred_element_type=jnp.float32) * (E ** -0.5)
    bias = -slopes[:, None, None] * (dil * j).astype(jnp.float32)[None, None, :]
    s = jnp.where(valid[None, None], s + bias[None], -jnp.inf)
    lse = jax.nn.logsumexp(s, axis=-1)
    p = jnp.exp(s - lse[..., None])
    o = jnp.einsum('bhtj,btjhe->bthe', p.astype(v_all.dtype), vg)
    return o, lse.transpose(0, 2, 1)


def combine_dilations(outs, lses):
    w = jax.nn.softmax(jnp.stack(lses, 0), axis=0)
    return jnp.einsum('gbth,gbthe->bthe', w.astype(outs[0].dtype), jnp.stack(outs, 0))


def layer_output(x, gla_o, gg, att_o, gla_norm_w, att_out_norm_w, w_out,
                 ffn_norm_w, w_up, w_down):
    B, T, _ = x.shape
    gla_part = rmsnorm(gla_o, gla_norm_w) * jax.nn.silu(gg.astype(jnp.float32)).reshape(B, T, N_GLA_HEADS, GLA_DV)
    gla_part = gla_part.reshape(B, T, GLA_WIDTH).astype(x.dtype)
    att_part = rmsnorm(att_o.reshape(B, T, ATT_WIDTH), att_out_norm_w).astype(x.dtype)
    x = x + jnp.concatenate([gla_part, att_part], axis=-1) @ w_out
    h = rmsnorm(x, ffn_norm_w)
    return x + jnp.square(jax.nn.relu(h @ w_up)) @ w_down


def setup_inputs(seed: int = 0) -> dict:
    key = jax.random.key(seed)
    ks = jax.random.split(key, 16)
    f32 = jnp.float32
    w_s = min(MAX_WINDOW, PAST_LEN)
    def nrm(k, shape, scale):
        return scale * jax.random.normal(k, shape, f32)
    def gain(k, shape):
        return 1.0 + 0.01 * jax.random.normal(k, shape, f32)
    return {
        'x_prompt': nrm(ks[0], (BATCH, SEQ, D_MODEL), 1.0),
        'x_sample': nrm(ks[1], (DEC_BATCH, DEC_SEQ, D_MODEL), 1.0),
        'cache_k_win': nrm(ks[2], (DEPTH, DEC_BATCH, w_s, N_ATT_HEADS, ATT_HEAD_DIM), 1.0),
        'cache_v_win': nrm(ks[3], (DEPTH, DEC_BATCH, w_s, N_ATT_HEADS, ATT_HEAD_DIM), 1.0),
        'state_gla': nrm(ks[4], (DEPTH, DEC_BATCH, N_GLA_HEADS, GLA_DK, GLA_DV), 0.5),
        'attn_norm_w': gain(ks[5], (DEPTH, D_MODEL)),
        'w_in': nrm(ks[6], (DEPTH, D_MODEL, IN_COLS), D_MODEL ** -0.5),
        'w_gk_up': nrm(ks[7], (DEPTH, GLA_GATE_RANK, GLA_KEY_WIDTH), GLA_GATE_RANK ** -0.5),
        'b_gk': nrm(ks[8], (DEPTH, GLA_KEY_WIDTH), 0.1),
        'gla_norm_w': gain(ks[9], (DEPTH, GLA_DV)),
        'att_out_norm_w': gain(ks[10], (DEPTH, ATT_WIDTH)),
        'w_out': nrm(ks[11], (DEPTH, D_MODEL, D_MODEL), D_MODEL ** -0.5),
        'ffn_norm_w': gain(ks[12], (DEPTH, D_MODEL)),
        'w_up': nrm(ks[13], (DEPTH, D_MODEL, D_FF), D_MODEL ** -0.5),
        'w_down': nrm(ks[14], (DEPTH, D_FF, D_MODEL), D_FF ** -0.5),
        'final_norm_w': gain(ks[15], (D_MODEL,)),
    }


def reference(x_prompt, x_sample, cache_k_win, cache_v_win, state_gla,
              attn_norm_w, w_in, w_gk_up, b_gk, gla_norm_w, att_out_norm_w, w_out,
              ffn_norm_w, w_up, w_down, final_norm_w):
    slopes = alibi_slopes()
    xp, xs = x_prompt, x_sample
    S = xp.shape[1]
    w_p = min(MAX_WINDOW, S)
    w_s = cache_k_win.shape[2]
    kp_l, vp_l, sp_l, ks_l, vs_l, ss_l = [], [], [], [], [], []
    for l in range(DEPTH):
        (gq, gk, gv, ga), gg, (aq, ak, av) = mixer_inputs(xp, attn_norm_w[l], w_in[l], w_gk_up[l], b_gk[l])
        s0 = jnp.zeros((xp.shape[0], N_GLA_HEADS, GLA_DK, GLA_DV), jnp.float32)
        gla_o, s_fin_p = gla_chunked(gq, gk, gv, ga, s0)
        outs, lses = [], []
        for _, dil in DILATED_CONFIGS:
            o, lse = strided_window_attention_prompt(aq, ak, av, dil, slopes)
            outs.append(o)
            lses.append(lse)
        att_o = combine_dilations(outs, lses)
        xp = layer_output(xp, gla_o, gg, att_o, gla_norm_w[l], att_out_norm_w[l], w_out[l],
                          ffn_norm_w[l], w_up[l], w_down[l])
        kp_l.append(ak[:, S - w_p:])
        vp_l.append(av[:, S - w_p:])
        sp_l.append(s_fin_p)
        (gq, gk, gv, ga), gg, (aq, ak, av) = mixer_inputs(xs, attn_norm_w[l], w_in[l], w_gk_up[l], b_gk[l])
        gla_o, s_fin_s = gla_chunked(gq, gk, gv, ga, state_gla[l])
        k_all = jnp.concatenate([cache_k_win[l].astype(ak.dtype), ak], axis=1)
        v_all = jnp.concatenate([cache_v_win[l].astype(av.dtype), av], axis=1)
        outs, lses = [], []
        for _, dil in DILATED_CONFIGS:
            o, lse = strided_window_attention_sample(aq, k_all, v_all, w_s, dil, slopes)
            outs.append(o)
            lses.append(lse)
        att_o = combine_dilations(outs, lses)
        xs = layer_output(xs, gla_o, gg, att_o, gla_norm_w[l], att_out_norm_w[l], w_out[l],
                          ffn_norm_w[l], w_up[l], w_down[l])
        ks_l.append(ak)
        vs_l.append(av)
        ss_l.append(s_fin_s)
    y_prompt = rmsnorm(xp, final_norm_w)
    y_sample = rmsnorm(xs, final_norm_w)
    k_win_prompt = jnp.stack(kp_l, 0)
    v_win_prompt = jnp.stack(vp_l, 0)
    gla_prompt = jnp.stack(sp_l, 0)
    k_new_sample = jnp.stack(ks_l, 0)
    v_new_sample = jnp.stack(vs_l, 0)
    gla_sample = jnp.stack(ss_l, 0)
    return (y_prompt, y_sample, k_win_prompt, v_win_prompt, gla_prompt,
            k_new_sample, v_new_sample, gla_sample)
```

```python
import functools

import jax
import jax.numpy as jnp
from jax import lax
from jax.experimental import pallas as pl
from jax.experimental.pallas import tpu as pltpu

f32 = jnp.float32
bf16 = jnp.bfloat16

D_MODEL = 2048
N_GLA_HEADS = 4
GLA_DK = 128
GLA_DV = 256
GLA_KEY_WIDTH = N_GLA_HEADS * GLA_DK
GLA_WIDTH = N_GLA_HEADS * GLA_DV
GLA_GATE_RANK = 16
GLA_GATE_NORM = 16.0
GLA_CHUNK = 16
N_ATT_HEADS = 8
ATT_HEAD_DIM = 128
ATT_WIDTH = N_ATT_HEADS * ATT_HEAD_DIM
DILATIONS = (1, 4, 16)
SUB_WINDOW = 128
ATT_BLOCK = 128
MAX_WINDOW = 2048
D_FF = 4 * D_MODEL
RMS_EPS = 1e-6
GLA_COLS = 2 * GLA_KEY_WIDTH + 2 * GLA_WIDTH
ATT_COLS = 3 * ATT_WIDTH
LANES = 128
VMEM_LIMIT = 56 * 1024 * 1024
NEG_INF = float("-inf")
ALIBI_SLOPES = tuple(2.0 ** (-8.0 * (h + 1) / N_ATT_HEADS) for h in range(N_ATT_HEADS))
ATT_SCALE = ATT_HEAD_DIM ** -0.5

NT_DIMS = (((1,), (1,)), ((), ()))
TN_DIMS = (((0,), (0,)), ((), ()))


def _cparams(*sem):
    return pltpu.CompilerParams(dimension_semantics=sem, vmem_limit_bytes=VMEM_LIMIT)


def _rms(x, w):
    r = lax.rsqrt(jnp.mean(x * x, axis=-1, keepdims=True) + RMS_EPS)
    return (x * r) * w


def _norm_body(x_ref, w_ref, o_ref):
    o_ref[...] = _rms(x_ref[...], w_ref[...]).astype(o_ref.dtype)


def _norm(x, w, tm):
    n, d = x.shape
    return pl.pallas_call(
        _norm_body,
        grid=(n // tm,),
        in_specs=[pl.BlockSpec((tm, d), lambda i: (i, 0)), pl.BlockSpec((1, d), lambda i: (0, 0))],
        out_specs=pl.BlockSpec((tm, d), lambda i: (i, 0)),
        out_shape=jax.ShapeDtypeStruct((n, d), bf16),
        compiler_params=_cparams("parallel"),
        name="rmsnorm_cast",
    )(x, w.reshape(1, d))


def _mm_body(h_ref, w_ref, o_ref):
    o_ref[...] = jnp.dot(h_ref[...], w_ref[...], preferred_element_type=f32).astype(o_ref.dtype)


def _mm(h, w, *, out_dtype, tm, tn, col0, ncols, name, row_groups=1, row0=0, rows_per_group=None):
    n, k = h.shape
    group_stride = n // row_groups
    if rows_per_group is None:
        rows_per_group = group_stride
    assert row0 % tm == 0 and rows_per_group % tm == 0 and group_stride % tm == 0
    assert col0 % tn == 0 and ncols % tn == 0
    nb = rows_per_group // tm
    gb, r0b, c0b = group_stride // tm, row0 // tm, col0 // tn
    return pl.pallas_call(
        _mm_body,
        grid=(ncols // tn, row_groups, nb),
        in_specs=[
            pl.BlockSpec((tm, k), lambda j, g, i: (g * gb + r0b + i, 0)),
            pl.BlockSpec((k, tn), lambda j, g, i: (0, c0b + j)),
        ],
        out_specs=pl.BlockSpec((tm, tn), lambda j, g, i: (g * nb + i, j)),
        out_shape=jax.ShapeDtypeStruct((row_groups * rows_per_group, ncols), out_dtype),
        compiler_params=_cparams("parallel", "parallel", "parallel"),
        name=name,
    )(h, w)


def _gate_body(h_ref, wlr_ref, wup_ref, b_ref, o_ref):
    glr = jnp.dot(h_ref[...], wlr_ref[...], preferred_element_type=f32)
    pre = jnp.dot(glr, wup_ref[...], precision=lax.Precision.HIGHEST, preferred_element_type=f32) + b_ref[...]
    log_sig = jnp.minimum(pre, 0.0) - jnp.log1p(jnp.exp(-jnp.abs(pre)))
    o_ref[...] = log_sig * (1.0 / GLA_GATE_NORM)


def _gate(h, wlr_pad, wup_pad, b_gk, tm):
    n, d = h.shape
    return pl.pallas_call(
        _gate_body,
        grid=(n // tm,),
        in_specs=[
            pl.BlockSpec((tm, d), lambda i: (i, 0)),
            pl.BlockSpec((d, LANES), lambda i: (0, 0)),
            pl.BlockSpec((LANES, GLA_KEY_WIDTH), lambda i: (0, 0)),
            pl.BlockSpec((1, GLA_KEY_WIDTH), lambda i: (0, 0)),
        ],
        out_specs=pl.BlockSpec((tm, GLA_KEY_WIDTH), lambda i: (i, 0)),
        out_shape=jax.ShapeDtypeStruct((n, GLA_KEY_WIDTH), f32),
        compiler_params=_cparams("parallel"),
        name="gla_gate",
    )(h, wlr_pad, wup_pad, b_gk.reshape(1, GLA_KEY_WIDTH))


GLA_SUB = 128


def _gla_body(q_ref, k_ref, v_ref, g_ref, s0_ref, o_ref, sfin_ref, st_scr, qe_scr, ke_scr, dec_scr, *, tb, t_valid):
    C = GLA_CHUNK
    t_blk = pl.program_id(1)

    @pl.when(t_blk == 0)
    def _():
        for h in range(N_GLA_HEADS):
            st_scr[h] = s0_ref[h].T

    row = lax.broadcasted_iota(jnp.int32, (tb, GLA_KEY_WIDTH), 0)
    rin = row & (C - 1)
    g = g_ref[...]
    if t_valid < C:
        g = jnp.where(rin < t_valid, g, 0.0)
    b = g
    for s in (1, 2, 4, 8):
        b = b + jnp.where(rin >= s, pltpu.roll(b, s, 0), 0.0)
    bl = b
    for s in (1, 2, 4, 8):
        bl = jnp.where(rin + s <= C - 1, pltpu.roll(bl, tb - s, 0), bl)
    q = q_ref[...] * (GLA_DK ** -0.5)
    k = k_ref[...]
    qe_scr[...] = q * jnp.exp(b)
    ke_scr[...] = k * jnp.exp(bl - b)
    dec_scr[...] = jnp.exp(bl)

    sub = min(GLA_SUB, tb)
    srow = lax.broadcasted_iota(jnp.int32, (sub, GLA_DK), 0)
    slane = lax.broadcasted_iota(jnp.int32, (sub, GLA_DK), 1)
    srin = srow & (C - 1)
    for sb in range(tb // sub):
        r0 = sb * sub
        for h in range(N_GLA_HEADS):
            ks = slice(h * GLA_DK, (h + 1) * GLA_DK)
            vs = slice(h * GLA_DV, (h + 1) * GLA_DV)
            qh = q[r0:r0 + sub, ks]
            kh = k[r0:r0 + sub, ks]
            bh = b[r0:r0 + sub, ks]
            a = jnp.zeros((sub, LANES), f32)
            for d in range(C):
                k_d = kh if d == 0 else pltpu.roll(kh, d, 0)
                b_d = bh if d == 0 else pltpu.roll(bh, d, 0)
                e = jnp.exp(jnp.where(srin >= d, bh - b_d, NEG_INF))
                a_d = jnp.sum(qh * k_d * e, axis=-1, keepdims=True)
                a = jnp.where(slane == srow - d, a_d, a)
            vh = v_ref[r0:r0 + sub, vs]
            o_ref[r0:r0 + sub, vs] = jnp.dot(a[:, :sub].astype(bf16), vh.astype(bf16), preferred_element_type=f32)

    def chunk_step(c, carry):
        c0 = pl.multiple_of(c * C, C)
        for h in range(N_GLA_HEADS):
            ks = slice(h * GLA_DK, (h + 1) * GLA_DK)
            vs = slice(h * GLA_DV, (h + 1) * GLA_DV)
            st = st_scr[h]
            qe_c = qe_scr[pl.ds(c0, C), ks]
            ke_c = ke_scr[pl.ds(c0, C), ks]
            v_c = v_ref[pl.ds(c0, C), vs]
            dec = dec_scr[pl.ds(c0, 1), ks]
            o_inter = lax.dot_general(qe_c.astype(bf16), st.astype(bf16), NT_DIMS, preferred_element_type=f32)
            o_ref[pl.ds(c0, C), vs] = o_ref[pl.ds(c0, C), vs] + o_inter
            kv_t = lax.dot_general(v_c.astype(bf16), ke_c.astype(bf16), TN_DIMS, preferred_element_type=f32)
            st_scr[h] = dec * st + kv_t
        return carry

    lax.fori_loop(0, tb // C, chunk_step, 0)

    @pl.when(t_blk == pl.num_programs(1) - 1)
    def _():
        for h in range(N_GLA_HEADS):
            sfin_ref[h] = st_scr[h].T


def _gla(pg, loga, s0, *, batch, t_len, tb, t_valid):
    nt = t_len // tb
    kern = functools.partial(_gla_body, tb=tb, t_valid=t_valid)
    state_spec = pl.BlockSpec((None, N_GLA_HEADS, GLA_DK, GLA_DV), lambda b, t: (b, 0, 0, 0))
    return pl.pallas_call(
        kern,
        grid=(batch, nt),
        in_specs=[
            pl.BlockSpec((tb, GLA_KEY_WIDTH), lambda b, t: (b * nt + t, 0)),
            pl.BlockSpec((tb, GLA_KEY_WIDTH), lambda b, t: (b * nt + t, 1)),
            pl.BlockSpec((tb, GLA_WIDTH), lambda b, t: (b * nt + t, 1)),
            pl.BlockSpec((tb, GLA_KEY_WIDTH), lambda b, t: (b * nt + t, 0)),
            state_spec,
        ],
        out_specs=[pl.BlockSpec((tb, GLA_WIDTH), lambda b, t: (b * nt + t, 0)), state_spec],
        out_shape=[
            jax.ShapeDtypeStruct((batch * t_len, GLA_WIDTH), f32),
            jax.ShapeDtypeStruct((batch, N_GLA_HEADS, GLA_DK, GLA_DV), f32),
        ],
        scratch_shapes=[
            pltpu.VMEM((N_GLA_HEADS, GLA_DV, GLA_DK), f32),
            pltpu.VMEM((tb, GLA_KEY_WIDTH), f32),
            pltpu.VMEM((tb, GLA_KEY_WIDTH), f32),
            pltpu.VMEM((tb, GLA_KEY_WIDTH), f32),
        ],
        compiler_params=_cparams("parallel", "arbitrary"),
        name="gla",
    )(pg, pg, pg, loga, s0)


def _att_prompt_body(q_ref, k_ref, v_ref, o_ref, lse_ref, kprev_scr, vprev_scr, *, dil):
    n = pl.program_id(2)

    @pl.when(n == 0)
    def _():
        kprev_scr[...] = jnp.zeros_like(kprev_scr)
        vprev_scr[...] = jnp.zeros_like(vprev_scr)

    blk = ATT_BLOCK
    qi = lax.broadcasted_iota(jnp.int32, (blk, blk), 0)
    ki = lax.broadcasted_iota(jnp.int32, (blk, blk), 1)
    dist_cur = qi - ki
    dist_prev = dist_cur + blk
    valid_cur = dist_cur >= 0
    no_prev = jnp.where(n > 0, 0, 2 * SUB_WINDOW)
    valid_prev = dist_prev + no_prev <= SUB_WINDOW
    fd_cur = (dil * dist_cur).astype(f32)
    fd_prev = (dil * dist_prev).astype(f32)
    lane = lax.broadcasted_iota(jnp.int32, (blk, LANES), 1)
    lse_all = jnp.zeros((blk, LANES), f32)
    for h in range(N_ATT_HEADS):
        hs = slice(h * ATT_HEAD_DIM, (h + 1) * ATT_HEAD_DIM)
        qh = q_ref[:, hs]
        s_cur = lax.dot_general(qh, k_ref[:, hs], NT_DIMS, preferred_element_type=f32) * ATT_SCALE
        s_prev = lax.dot_general(qh, kprev_scr[:, hs], NT_DIMS, preferred_element_type=f32) * ATT_SCALE
        s_cur = jnp.where(valid_cur, s_cur - ALIBI_SLOPES[h] * fd_cur, NEG_INF)
        s_prev = jnp.where(valid_prev, s_prev - ALIBI_SLOPES[h] * fd_prev, NEG_INF)
        m = jnp.maximum(jnp.max(s_cur, axis=-1, keepdims=True), jnp.max(s_prev, axis=-1, keepdims=True))
        p_cur = jnp.exp(s_cur - m)
        p_prev = jnp.exp(s_prev - m)
        l = jnp.sum(p_cur, axis=-1, keepdims=True) + jnp.sum(p_prev, axis=-1, keepdims=True)
        acc = jnp.dot(p_cur.astype(bf16), v_ref[:, hs], preferred_element_type=f32)
        acc = acc + jnp.dot(p_prev.astype(bf16), vprev_scr[:, hs], preferred_element_type=f32)
        o_ref[:, hs] = acc / l
        lse_all = jnp.where(lane == h, m + jnp.log(l), lse_all)
    lse_ref[...] = lse_all
    kprev_scr[...] = k_ref[...]
    vprev_scr[...] = v_ref[...]


def _att_prompt(pa, *, batch, seq, dil):
    L = seq // dil
    nb = L // ATT_BLOCK
    pa_v = pa.reshape(batch, L, dil * ATT_COLS)
    kern = functools.partial(_att_prompt_body, dil=dil)
    blk = (None, ATT_BLOCK, ATT_WIDTH)
    o, lse = pl.pallas_call(
        kern,
        grid=(batch, dil, nb),
        in_specs=[
            pl.BlockSpec(blk, lambda b, r, n: (b, n, 3 * r)),
            pl.BlockSpec(blk, lambda b, r, n: (b, n, 3 * r + 1)),
            pl.BlockSpec(blk, lambda b, r, n: (b, n, 3 * r + 2)),
        ],
        out_specs=[
            pl.BlockSpec(blk, lambda b, r, n: (b, n, r)),
            pl.BlockSpec((None, ATT_BLOCK, LANES), lambda b, r, n: (b, n, r)),
        ],
        out_shape=[
            jax.ShapeDtypeStruct((batch, L, dil * ATT_WIDTH), f32),
            jax.ShapeDtypeStruct((batch, L, dil * LANES), f32),
        ],
        scratch_shapes=[pltpu.VMEM((ATT_BLOCK, ATT_WIDTH), bf16), pltpu.VMEM((ATT_BLOCK, ATT_WIDTH), bf16)],
        compiler_params=_cparams("parallel", "parallel", "arbitrary"),
        name=f"att_prompt_d{dil}",
    )(pa_v, pa_v, pa_v)
    return o.reshape(batch * seq, ATT_WIDTH), lse.reshape(batch * seq, LANES)


SAMPLE_PAD = 16


def _att_sample_body(qkv_ref, k1_ref, v1_ref, k4_ref, v4_ref, k16_ref, v16_ref, o_ref, *, t_new):
    P = SAMPLE_PAD
    J = SUB_WINDOW
    rowq = lax.broadcasted_iota(jnp.int32, (P, J), 0)
    lanek = lax.broadcasted_iota(jnp.int32, (P, J), 1)
    rown = lax.broadcasted_iota(jnp.int32, (P, P), 0)
    lanen = lax.broadcasted_iota(jnp.int32, (P, P), 1)
    orow = lax.broadcasted_iota(jnp.int32, (P, ATT_HEAD_DIM), 0)
    j1 = (J + rowq - lanek)
    valid1 = lanek >= rowq
    jd = (J - lanek)
    new_valid1 = (lanen <= rown) & (lanen < t_new)
    new_diag = lanen == rown
    jn1 = rown - lanen
    for h in range(N_ATT_HEADS):
        hs = slice(h * ATT_HEAD_DIM, (h + 1) * ATT_HEAD_DIM)
        slope = ALIBI_SLOPES[h]
        qh = qkv_ref[:, h * ATT_HEAD_DIM:(h + 1) * ATT_HEAD_DIM]
        knew = qkv_ref[:, ATT_WIDTH + h * ATT_HEAD_DIM:ATT_WIDTH + (h + 1) * ATT_HEAD_DIM]
        vnew = qkv_ref[:, 2 * ATT_WIDTH + h * ATT_HEAD_DIM:2 * ATT_WIDTH + (h + 1) * ATT_HEAD_DIM]
        s_new = lax.dot_general(qh, knew, NT_DIMS, preferred_element_type=f32) * ATT_SCALE
        outs, lses = [], []
        for dil in DILATIONS:
            kres_ref, vres_ref = (k4_ref, v4_ref) if dil == 4 else (k16_ref, v16_ref)
            if dil == 1:
                kc = k1_ref[:, hs].astype(bf16)
                s_c = lax.dot_general(qh, kc, NT_DIMS, preferred_element_type=f32) * ATT_SCALE
                s_c = jnp.where(valid1, s_c - slope * j1.astype(f32), NEG_INF)
                s_n = jnp.where(new_valid1, s_new - slope * jn1.astype(f32), NEG_INF)
            else:
                s_c = jnp.zeros((P, J), f32)
                for r in range(t_new):
                    kc = kres_ref[:, r * ATT_WIDTH + h * ATT_HEAD_DIM:r * ATT_WIDTH + (h + 1) * ATT_HEAD_DIM]
                    s_r = lax.dot_general(qh, kc.astype(bf16), NT_DIMS, preferred_element_type=f32)
                    s_c = jnp.where(rowq == r, s_r, s_c)
                s_c = s_c * ATT_SCALE - (slope * dil) * jd.astype(f32)
                s_n = jnp.where(new_diag, s_new, NEG_INF)
            m = jnp.maximum(jnp.max(s_c, axis=-1, keepdims=True), jnp.max(s_n, axis=-1, keepdims=True))
            p_c = jnp.exp(s_c - m)
            p_n = jnp.exp(s_n - m)
            l = jnp.sum(p_c, axis=-1, keepdims=True) + jnp.sum(p_n, axis=-1, keepdims=True)
            acc = jnp.dot(p_n.astype(bf16), vnew, preferred_element_type=f32)
            if dil == 1:
                vc = v1_ref[:, hs].astype(bf16)
                acc = acc + jnp.dot(p_c.astype(bf16), vc, preferred_element_type=f32)
            else:
                for r in range(t_new):
                    vc = vres_ref[:, r * ATT_WIDTH + h * ATT_HEAD_DIM:r * ATT_WIDTH + (h + 1) * ATT_HEAD_DIM]
                    o_r = jnp.dot(p_c.astype(bf16), vc.astype(bf16), preferred_element_type=f32)
                    acc = acc + jnp.where(orow == r, o_r, 0.0)
            outs.append(acc / l)
            lses.append(m + jnp.log(l))
        mm = jnp.maximum(jnp.maximum(lses[0], lses[1]), lses[2])
        ws = [jnp.exp(x - mm) for x in lses]
        tot = ws[0] + ws[1] + ws[2]
        comb = (ws[0] * outs[0] + ws[1] * outs[1] + ws[2] * outs[2]) / tot
        o_ref[:, hs] = comb[0:t_new, :]


def _att_sample(pa_s, cache_k, cache_v, *, batch, t_new):
    w_cache = cache_k.shape[1]
    J = SUB_WINDOW
    assert w_cache == max(DILATIONS) * J and t_new <= 4
    specs, views = [], []
    for dil in DILATIONS:
        rows = w_cache // dil
        for c in (cache_k, cache_v):
            views.append(c.reshape(batch, rows, dil * ATT_WIDTH))
            specs.append(pl.BlockSpec((None, J, min(dil, 4) * ATT_WIDTH), lambda b, nb=rows // J - 1: (b, nb, 0)))
    kern = functools.partial(_att_sample_body, t_new=t_new)
    return pl.pallas_call(
        kern,
        grid=(batch,),
        in_specs=[pl.BlockSpec((SAMPLE_PAD, ATT_COLS), lambda b: (b, 0))] + specs,
        out_specs=pl.BlockSpec((None, t_new, ATT_WIDTH), lambda b: (b, 0, 0)),
        out_shape=jax.ShapeDtypeStruct((batch, t_new, ATT_WIDTH), f32),
        compiler_params=_cparams("parallel"),
        name="att_sample",
    )(pa_s, *views)


def _outproj_body(*refs, n_att):
    x_ref, go_ref, gg_ref = refs[0:3]
    o_refs = refs[3:3 + n_att]
    lse_refs = refs[3 + n_att:3 + 2 * n_att] if n_att > 1 else ()
    gnw_ref, anw_ref, wg_ref, wa_ref, out_ref = refs[3 + (2 * n_att if n_att > 1 else 1):]
    tm = x_ref.shape[0]
    parts = []
    for h in range(N_GLA_HEADS):
        vs = slice(h * GLA_DV, (h + 1) * GLA_DV)
        y = _rms(go_ref[:, vs], gnw_ref[...])
        gate = gg_ref[:, vs]
        parts.append((y * (gate * jax.nn.sigmoid(gate))).astype(bf16))
    gla_part = jnp.concatenate(parts, axis=-1)
    if n_att == 1:
        att = o_refs[0][...]
    else:
        heads = []
        for h in range(N_ATT_HEADS):
            hs = slice(h * ATT_HEAD_DIM, (h + 1) * ATT_HEAD_DIM)
            ls = [jnp.broadcast_to(r[:, h:h + 1], (tm, ATT_HEAD_DIM)) for r in lse_refs]
            mm = ls[0]
            for x in ls[1:]:
                mm = jnp.maximum(mm, x)
            ws = [jnp.exp(x - mm) for x in ls]
            tot = ws[0]
            num = ws[0] * o_refs[0][:, hs]
            for w, o in zip(ws[1:], o_refs[1:]):
                tot = tot + w
                num = num + w * o[:, hs]
            heads.append(num / tot)
        att = jnp.concatenate(heads, axis=-1)
    att_part = _rms(att, anw_ref[...]).astype(bf16)
    mix = jnp.dot(gla_part, wg_ref[...], preferred_element_type=f32)
    mix = mix + jnp.dot(att_part, wa_ref[...], preferred_element_type=f32)
    out_ref[...] = x_ref[...] + mix


def _outproj(x, gla_o, gg_src, gg_col, att_os, att_lses, gla_norm_w, att_norm_w, w_out_bf, tm):
    n = x.shape[0]
    n_att = len(att_os)
    row = lambda i: (i, 0)
    const = lambda i: (0, 0)
    in_specs = [
        pl.BlockSpec((tm, D_MODEL), row),
        pl.BlockSpec((tm, GLA_WIDTH), row),
        pl.BlockSpec((tm, GLA_WIDTH), lambda i: (i, gg_col)),
    ]
    in_specs += [pl.BlockSpec((tm, ATT_WIDTH), row) for _ in att_os]
    in_specs += [pl.BlockSpec((tm, LANES), row) for _ in att_lses]
    in_specs += [
        pl.BlockSpec((1, GLA_DV), const),
        pl.BlockSpec((1, ATT_WIDTH), const),
        pl.BlockSpec((GLA_WIDTH, D_MODEL), lambda i: (0, 0)),
        pl.BlockSpec((ATT_WIDTH, D_MODEL), lambda i: (1, 0)),
    ]
    return pl.pallas_call(
        functools.partial(_outproj_body, n_att=n_att),
        grid=(n // tm,),
        in_specs=in_specs,
        out_specs=pl.BlockSpec((tm, D_MODEL), row),
        out_shape=jax.ShapeDtypeStruct((n, D_MODEL), f32),
        compiler_params=_cparams("parallel"),
        name="mixer_out_proj",
    )(x, gla_o, gg_src, *att_os, *att_lses, gla_norm_w.reshape(1, GLA_DV), att_norm_w.reshape(1, ATT_WIDTH),
      w_out_bf, w_out_bf)


def _mlp_body(x_ref, fw_ref, wu_ref, wd_ref, nw_ref, o_ref, h_scr):
    f = pl.program_id(1)

    @pl.when(f == 0)
    def _():
        h_scr[...] = _rms(x_ref[...], fw_ref[...]).astype(bf16)
        o_ref[...] = jnp.zeros_like(o_ref)

    u = jnp.dot(h_scr[...], wu_ref[...], preferred_element_type=f32)
    a = jnp.square(jnp.maximum(u, 0.0)).astype(bf16)
    o_ref[...] += jnp.dot(a, wd_ref[...], preferred_element_type=f32)

    @pl.when(f == pl.num_programs(1) - 1)
    def _():
        o_ref[...] = _rms(x_ref[...] + o_ref[...], nw_ref[...])


def _mlp(x, ffn_norm_w, w_up_bf, w_down_bf, final_norm_w, tm, tf):
    n = x.shape[0]
    return pl.pallas_call(
        _mlp_body,
        grid=(n // tm, D_FF // tf),
        in_specs=[
            pl.BlockSpec((tm, D_MODEL), lambda i, f: (i, 0)),
            pl.BlockSpec((1, D_MODEL), lambda i, f: (0, 0)),
            pl.BlockSpec((D_MODEL, tf), lambda i, f: (0, f)),
            pl.BlockSpec((tf, D_MODEL), lambda i, f: (f, 0)),
            pl.BlockSpec((1, D_MODEL), lambda i, f: (0, 0)),
        ],
        out_specs=pl.BlockSpec((tm, D_MODEL), lambda i, f: (i, 0)),
        out_shape=jax.ShapeDtypeStruct((n, D_MODEL), f32),
        scratch_shapes=[pltpu.VMEM((tm, D_MODEL), bf16)],
        compiler_params=_cparams("parallel", "arbitrary"),
        name="mlp_final_norm",
    )(x, ffn_norm_w.reshape(1, D_MODEL), w_up_bf, w_down_bf, final_norm_w.reshape(1, D_MODEL))


def _project(x2d, norm_w, w_gla_bf, w_att_bf, wlr_pad, wup_pad, b_gk, tm):
    h = _norm(x2d, norm_w, min(tm, 512))
    pg = _mm(h, w_gla_bf, out_dtype=f32, tm=tm, tn=1024, col0=0, ncols=GLA_COLS, name="proj_gla")
    loga = _gate(h, wlr_pad, wup_pad, b_gk, min(tm, 512))
    pa = _mm(h, w_att_bf, out_dtype=bf16, tm=tm, tn=1024, col0=0, ncols=ATT_COLS, name="proj_att")
    return h, pg, loga, pa


def kernel(x_prompt, x_sample, cache_k_win, cache_v_win, state_gla, attn_norm_w, w_in, w_gk_up, b_gk, gla_norm_w,
           att_out_norm_w, w_out, ffn_norm_w, w_up, w_down, final_norm_w):
    depth = w_in.shape[0]
    assert depth == 1, "single trunk layer"
    B, S, _ = x_prompt.shape
    Bs, Ts, _ = x_sample.shape
    w_p = min(MAX_WINDOW, S)
    assert S % (ATT_BLOCK * max(DILATIONS)) == 0 and Ts <= GLA_CHUNK

    w_in0 = w_in[0]
    c_lr = GLA_COLS
    w_gla_bf = w_in0[:, :c_lr].astype(bf16)
    wlr_pad = jnp.pad(w_in0[:, c_lr:c_lr + GLA_GATE_RANK], ((0, 0), (0, LANES - GLA_GATE_RANK))).astype(bf16)
    w_att_bf = w_in0[:, c_lr + GLA_GATE_RANK:].astype(bf16)
    wup_pad = jnp.pad(w_gk_up[0], ((0, LANES - GLA_GATE_RANK), (0, 0)))
    w_out_bf = w_out[0].astype(bf16)
    w_up_bf = w_up[0].astype(bf16)
    w_down_bf = w_down[0].astype(bf16)

    xp = x_prompt.reshape(B * S, D_MODEL)
    h, pg, loga, pa = _project(xp, attn_norm_w[0], w_gla_bf, w_att_bf, wlr_pad, wup_pad, b_gk[0], 1024)
    kv_rows = dict(row_groups=B, row0=S - w_p, rows_per_group=w_p)
    k_win = _mm(h, w_att_bf, out_dtype=f32, tm=1024, tn=1024, col0=ATT_WIDTH, ncols=ATT_WIDTH, name="k_window", **kv_rows)
    v_win = _mm(h, w_att_bf, out_dtype=f32, tm=1024, tn=1024, col0=2 * ATT_WIDTH, ncols=ATT_WIDTH, name="v_window", **kv_rows)
    s0 = jnp.zeros((B, N_GLA_HEADS, GLA_DK, GLA_DV), f32)
    gla_o, gla_state_p = _gla(pg, loga, s0, batch=B, t_len=S, tb=512, t_valid=GLA_CHUNK)
    att = [_att_prompt(pa, batch=B, seq=S, dil=d) for d in DILATIONS]
    x1 = _outproj(xp, gla_o, pg, 2, [a[0] for a in att], [a[1] for a in att], gla_norm_w[0], att_out_norm_w[0],
                  w_out_bf, 256)
    y_prompt = _mlp(x1, ffn_norm_w[0], w_up_bf, w_down_bf, final_norm_w, 1024, 512).reshape(B, S, D_MODEL)

    P = SAMPLE_PAD
    xs_pad = jnp.pad(x_sample, ((0, 0), (0, P - Ts), (0, 0))).reshape(Bs * P, D_MODEL)
    hs_, pgs, logas, pas = _project(xs_pad, attn_norm_w[0], w_gla_bf, w_att_bf, wlr_pad, wup_pad, b_gk[0], Bs * P)
    kv_s = _mm(hs_, w_att_bf, out_dtype=f32, tm=Bs * P, tn=1024, col0=ATT_WIDTH, ncols=2 * ATT_WIDTH, name="kv_new")
    kv_s = kv_s.reshape(Bs, P, 2, N_ATT_HEADS, ATT_HEAD_DIM)[:, :Ts]
    gla_o_s, gla_state_s = _gla(pgs, logas, state_gla[0], batch=Bs, t_len=P, tb=P, t_valid=Ts)
    att_s = _att_sample(pas, cache_k_win[0], cache_v_win[0], batch=Bs, t_new=Ts).reshape(Bs * Ts, ATT_WIDTH)
    gla_o_s = gla_o_s.reshape(Bs, P, GLA_WIDTH)[:, :Ts].reshape(Bs * Ts, GLA_WIDTH)
    gg_s = pgs.reshape(Bs, P, GLA_COLS)[:, :Ts, 2 * GLA_KEY_WIDTH + GLA_WIDTH:].reshape(Bs * Ts, GLA_WIDTH)
    xs = x_sample.reshape(Bs * Ts, D_MODEL)
    x1s = _outproj(xs, gla_o_s, gg_s, 0, [att_s], [], gla_norm_w[0], att_out_norm_w[0], w_out_bf, Bs * Ts)
    y_sample = _mlp(x1s, ffn_norm_w[0], w_up_bf, w_down_bf, final_norm_w, Bs * Ts, 512).reshape(Bs, Ts, D_MODEL)

    k_win_prompt = k_win.reshape(1, B, w_p, N_ATT_HEADS, ATT_HEAD_DIM)
    v_win_prompt = v_win.reshape(1, B, w_p, N_ATT_HEADS, ATT_HEAD_DIM)
    k_new_sample = kv_s[:, :, 0][None]
    v_new_sample = kv_s[:, :, 1][None]
    return (y_prompt, y_sample, k_win_prompt, v_win_prompt, gla_state_p[None], k_new_sample, v_new_sample,
            gla_state_s[None])
```

```python
import functools

import jax
import jax.numpy as jnp
from jax import lax
from jax.experimental import pallas as pl
from jax.experimental.pallas import tpu as pltpu

f32 = jnp.float32
bf16 = jnp.bfloat16

D_MODEL = 2048
N_GLA_HEADS = 4
GLA_DK = 128
GLA_DV = 256
GLA_KEY_WIDTH = N_GLA_HEADS * GLA_DK
GLA_WIDTH = N_GLA_HEADS * GLA_DV
GLA_GATE_RANK = 16
GLA_GATE_NORM = 16.0
GLA_CHUNK = 16
N_ATT_HEADS = 8
ATT_HEAD_DIM = 128
ATT_WIDTH = N_ATT_HEADS * ATT_HEAD_DIM
DILATIONS = (1, 4, 16)
SUB_WINDOW = 128
ATT_BLOCK = 128
MAX_WINDOW = 2048
D_FF = 4 * D_MODEL
RMS_EPS = 1e-6
GLA_COLS = 2 * GLA_KEY_WIDTH + 2 * GLA_WIDTH
ATT_COLS = 3 * ATT_WIDTH
LANES = 128
VMEM_LIMIT = 56 * 1024 * 1024
NEG_INF = float("-inf")
ALIBI_SLOPES = tuple(2.0 ** (-8.0 * (h + 1) / N_ATT_HEADS) for h in range(N_ATT_HEADS))
ATT_SCALE = ATT_HEAD_DIM ** -0.5

NT_DIMS = (((1,), (1,)), ((), ()))
TN_DIMS = (((0,), (0,)), ((), ()))


def _cparams(*sem):
    return pltpu.CompilerParams(dimension_semantics=sem, vmem_limit_bytes=VMEM_LIMIT)


def _rms(x, w):
    r = lax.rsqrt(jnp.mean(x * x, axis=-1, keepdims=True) + RMS_EPS)
    return (x * r) * w


def _norm_body(x_ref, w_ref, o_ref):
    o_ref[...] = _rms(x_ref[...], w_ref[...]).astype(o_ref.dtype)


def _norm(x, w, tm):
    n, d = x.shape
    return pl.pallas_call(
        _norm_body,
        grid=(n // tm,),
        in_specs=[pl.BlockSpec((tm, d), lambda i: (i, 0)), pl.BlockSpec((1, d), lambda i: (0, 0))],
        out_specs=pl.BlockSpec((tm, d), lambda i: (i, 0)),
        out_shape=jax.ShapeDtypeStruct((n, d), bf16),
        compiler_params=_cparams("parallel"),
        name="rmsnorm_cast",
    )(x, w.reshape(1, d))


def _mm_body(h_ref, w_ref, o_ref):
    o_ref[...] = jnp.dot(h_ref[...], w_ref[...], preferred_element_type=f32).astype(o_ref.dtype)


def _mm(h, w, *, out_dtype, tm, tn, col0, ncols, name, row_groups=1, row0=0, rows_per_group=None):
    n, k = h.shape
    group_stride = n // row_groups
    if rows_per_group is None:
        rows_per_group = group_stride
    assert row0 % tm == 0 and rows_per_group % tm == 0 and group_stride % tm == 0
    assert col0 % tn == 0 and ncols % tn == 0
    nb = rows_per_group // tm
    gb, r0b, c0b = group_stride // tm, row0 // tm, col0 // tn
    return pl.pallas_call(
        _mm_body,
        grid=(ncols // tn, row_groups, nb),
        in_specs=[
            pl.BlockSpec((tm, k), lambda j, g, i: (g * gb + r0b + i, 0)),
            pl.BlockSpec((k, tn), lambda j, g, i: (0, c0b + j)),
        ],
        out_specs=pl.BlockSpec((tm, tn), lambda j, g, i: (g * nb + i, j)),
        out_shape=jax.ShapeDtypeStruct((row_groups * rows_per_group, ncols), out_dtype),
        compiler_params=_cparams("parallel", "parallel", "parallel"),
        name=name,
    )(h, w)


def _gate_body(h_ref, wlr_ref, wup_ref, b_ref, o_ref):
    glr = jnp.dot(h_ref[...], wlr_ref[...], preferred_element_type=f32)
    pre = jnp.dot(glr, wup_ref[...], precision=lax.Precision.HIGHEST, preferred_element_type=f32) + b_ref[...]
    log_sig = jnp.minimum(pre, 0.0) - jnp.log1p(jnp.exp(-jnp.abs(pre)))
    o_ref[...] = log_sig * (1.0 / GLA_GATE_NORM)


def _gate(h, wlr_pad, wup_pad, b_gk, tm):
    n, d = h.shape
    return pl.pallas_call(
        _gate_body,
        grid=(n // tm,),
        in_specs=[
            pl.BlockSpec((tm, d), lambda i: (i, 0)),
            pl.BlockSpec((d, LANES), lambda i: (0, 0)),
            pl.BlockSpec((LANES, GLA_KEY_WIDTH), lambda i: (0, 0)),
            pl.BlockSpec((1, GLA_KEY_WIDTH), lambda i: (0, 0)),
        ],
        out_specs=pl.BlockSpec((tm, GLA_KEY_WIDTH), lambda i: (i, 0)),
        out_shape=jax.ShapeDtypeStruct((n, GLA_KEY_WIDTH), f32),
        compiler_params=_cparams("parallel"),
        name="gla_gate",
    )(h, wlr_pad, wup_pad, b_gk.reshape(1, GLA_KEY_WIDTH))


GLA_SUB = 128


def _gla_body(q_ref, k_ref, v_ref, g_ref, s0_ref, o_ref, sfin_ref, st_scr, qe_scr, ke_scr, dec_scr, *, tb, t_valid):
    C = GLA_CHUNK
    t_blk = pl.program_id(1)

    @pl.when(t_blk == 0)
    def _():
        for h in range(N_GLA_HEADS):
            st_scr[h] = s0_ref[h].T

    row = lax.broadcasted_iota(jnp.int32, (tb, GLA_KEY_WIDTH), 0)
    rin = row & (C - 1)
    g = g_ref[...]
    if t_valid < C:
        g = jnp.where(rin < t_valid, g, 0.0)
    b = g
    for s in (1, 2, 4, 8):
        b = b + jnp.where(rin >= s, pltpu.roll(b, s, 0), 0.0)
    bl = b
    for s in (1, 2, 4, 8):
        bl = jnp.where(rin + s <= C - 1, pltpu.roll(bl, tb - s, 0), bl)
    q = q_ref[...] * (GLA_DK ** -0.5)
    k = k_ref[...]
    qe_scr[...] = q * jnp.exp(b)
    ke_scr[...] = k * jnp.exp(bl - b)
    dec_scr[...] = jnp.exp(bl)

    sub = min(GLA_SUB, tb)
    srow = lax.broadcasted_iota(jnp.int32, (sub, GLA_DK), 0)
    slane = lax.broadcasted_iota(jnp.int32, (sub, GLA_DK), 1)
    srin = srow & (C - 1)
    for sb in range(tb // sub):
        r0 = sb * sub
        for h in range(N_GLA_HEADS):
            ks = slice(h * GLA_DK, (h + 1) * GLA_DK)
            vs = slice(h * GLA_DV, (h + 1) * GLA_DV)
            qh = q[r0:r0 + sub, ks]
            kh = k[r0:r0 + sub, ks]
            bh = b[r0:r0 + sub, ks]
            a = jnp.zeros((sub, LANES), f32)
            for d in range(C):
                k_d = kh if d == 0 else pltpu.roll(kh, d, 0)
                b_d = bh if d == 0 else pltpu.roll(bh, d, 0)
                e = jnp.exp(jnp.where(srin >= d, bh - b_d, NEG_INF))
                a_d = jnp.sum(qh * k_d * e, axis=-1, keepdims=True)
                a = jnp.where(slane == srow - d, a_d, a)
            vh = v_ref[r0:r0 + sub, vs]
            o_ref[r0:r0 + sub, vs] = jnp.dot(a[:, :sub].astype(bf16), vh.astype(bf16), preferred_element_type=f32)

    def chunk_step(c, carry):
        c0 = pl.multiple_of(c * C, C)
        for h in range(N_GLA_HEADS):
            ks = slice(h * GLA_DK, (h + 1) * GLA_DK)
            vs = slice(h * GLA_DV, (h + 1) * GLA_DV)
            st = st_scr[h]
            qe_c = qe_scr[pl.ds(c0, C), ks]
            ke_c = ke_scr[pl.ds(c0, C), ks]
            v_c = v_ref[pl.ds(c0, C), vs]
            dec = dec_scr[pl.ds(c0, 1), ks]
            o_inter = lax.dot_general(qe_c.astype(bf16), st.astype(bf16), NT_DIMS, preferred_element_type=f32)
            o_ref[pl.ds(c0, C), vs] = o_ref[pl.ds(c0, C), vs] + o_inter
            kv_t = lax.dot_general(v_c.astype(bf16), ke_c.astype(bf16), TN_DIMS, preferred_element_type=f32)
            st_scr[h] = dec * st + kv_t
        return carry

    lax.fori_loop(0, tb // C, chunk_step, 0)

    @pl.when(t_blk == pl.num_programs(1) - 1)
    def _():
        for h in range(N_GLA_HEADS):
            sfin_ref[h] = st_scr[h].T


def _gla(pg, loga, s0, *, batch, t_len, tb, t_valid):
    nt = t_len // tb
    kern = functools.partial(_gla_body, tb=tb, t_valid=t_valid)
    state_spec = pl.BlockSpec((None, N_GLA_HEADS, GLA_DK, GLA_DV), lambda b, t: (b, 0, 0, 0))
    return pl.pallas_call(
        kern,
        grid=(batch, nt),
        in_specs=[
            pl.BlockSpec((tb, GLA_KEY_WIDTH), lambda b, t: (b * nt + t, 0)),
            pl.BlockSpec((tb, GLA_KEY_WIDTH), lambda b, t: (b * nt + t, 1)),
            pl.BlockSpec((tb, GLA_WIDTH), lambda b, t: (b * nt + t, 1)),
            pl.BlockSpec((tb, GLA_KEY_WIDTH), lambda b, t: (b * nt + t, 0)),
            state_spec,
        ],
        out_specs=[pl.BlockSpec((tb, GLA_WIDTH), lambda b, t: (b * nt + t, 0)), state_spec],
        out_shape=[
            jax.ShapeDtypeStruct((batch * t_len, GLA_WIDTH), f32),
            jax.ShapeDtypeStruct((batch, N_GLA_HEADS, GLA_DK, GLA_DV), f32),
        ],
        scratch_shapes=[
            pltpu.VMEM((N_GLA_HEADS, GLA_DV, GLA_DK), f32),
            pltpu.VMEM((tb, GLA_KEY_WIDTH), f32),
            pltpu.VMEM((tb, GLA_KEY_WIDTH), f32),
            pltpu.VMEM((tb, GLA_KEY_WIDTH), f32),
        ],
        compiler_params=_cparams("parallel", "arbitrary"),
        name="gla",
    )(pg, pg, pg, loga, s0)


def _att_prompt_body(q_ref, k_ref, v_ref, o_ref, lse_ref, kprev_scr, vprev_scr, *, dil):
    n = pl.program_id(2)

    @pl.when(n == 0)
    def _():
        kprev_scr[...] = jnp.zeros_like(kprev_scr)
        vprev_scr[...] = jnp.zeros_like(vprev_scr)

    blk = ATT_BLOCK
    qi = lax.broadcasted_iota(jnp.int32, (blk, blk), 0)
    ki = lax.broadcasted_iota(jnp.int32, (blk, blk), 1)
    dist_cur = qi - ki
    dist_prev = dist_cur + blk
    valid_cur = dist_cur >= 0
    no_prev = jnp.where(n > 0, 0, 2 * SUB_WINDOW)
    valid_prev = dist_prev + no_prev <= SUB_WINDOW
    fd_cur = (dil * dist_cur).astype(f32)
    fd_prev = (dil * dist_prev).astype(f32)
    lane = lax.broadcasted_iota(jnp.int32, (blk, LANES), 1)
    lse_all = jnp.zeros((blk, LANES), f32)
    for h in range(N_ATT_HEADS):
        hs = slice(h * ATT_HEAD_DIM, (h + 1) * ATT_HEAD_DIM)
        qh = q_ref[:, hs]
        s_cur = lax.dot_general(qh, k_ref[:, hs], NT_DIMS, preferred_element_type=f32) * ATT_SCALE
        s_prev = lax.dot_general(qh, kprev_scr[:, hs], NT_DIMS, preferred_element_type=f32) * ATT_SCALE
        s_cur = jnp.where(valid_cur, s_cur - ALIBI_SLOPES[h] * fd_cur, NEG_INF)
        s_prev = jnp.where(valid_prev, s_prev - ALIBI_SLOPES[h] * fd_prev, NEG_INF)
        m = jnp.maximum(jnp.max(s_cur, axis=-1, keepdims=True), jnp.max(s_prev, axis=-1, keepdims=True))
        p_cur = jnp.exp(s_cur - m)
        p_prev = jnp.exp(s_prev - m)
        l = jnp.sum(p_cur, axis=-1, keepdims=True) + jnp.sum(p_prev, axis=-1, keepdims=True)
        acc = jnp.dot(p_cur.astype(bf16), v_ref[:, hs], preferred_element_type=f32)
        acc = acc + jnp.dot(p_prev.astype(bf16), vprev_scr[:, hs], preferred_element_type=f32)
        o_ref[:, hs] = acc / l
        lse_all = jnp.where(lane == h, m + jnp.log(l), lse_all)
    lse_ref[...] = lse_all
    kprev_scr[...] = k_ref[...]
    vprev_scr[...] = v_ref[...]


def _att_prompt(pa, *, batch, seq, dil):
    L = seq // dil
    nb = L // ATT_BLOCK
    pa_v = pa.reshape(batch, L, dil * ATT_COLS)
    kern = functools.partial(_att_prompt_body, dil=dil)
    blk = (None, ATT_BLOCK, ATT_WIDTH)
    o, lse = pl.pallas_call(
        kern,
        grid=(batch, dil, nb),
        in_specs=[
            pl.BlockSpec(blk, lambda b, r, n: (b, n, 3 * r)),
            pl.BlockSpec(blk, lambda b, r, n: (b, n, 3 * r + 1)),
            pl.BlockSpec(blk, lambda b, r, n: (b, n, 3 * r + 2)),
        ],
        out_specs=[
            pl.BlockSpec(blk, lambda b, r, n: (b, n, r)),
            pl.BlockSpec((None, ATT_BLOCK, LANES), lambda b, r, n: (b, n, r)),
        ],
        out_shape=[
            jax.ShapeDtypeStruct((batch, L, dil * ATT_WIDTH), f32),
            jax.ShapeDtypeStruct((batch, L, dil * LANES), f32),
        ],
        scratch_shapes=[pltpu.VMEM((ATT_BLOCK, ATT_WIDTH), bf16), pltpu.VMEM((ATT_BLOCK, ATT_WIDTH), bf16)],
        compiler_params=_cparams("parallel", "parallel", "arbitrary"),
        name=f"att_prompt_d{dil}",
    )(pa_v, pa_v, pa_v)
    return o.reshape(batch * seq, ATT_WIDTH), lse.reshape(batch * seq, LANES)


SAMPLE_PAD = 16


def _att_sample_body(qkv_ref, k1_ref, v1_ref, k4_ref, v4_ref, k16_ref, v16_ref, o_ref, *, t_new):
    J = SUB_WINDOW
    hidx = lax.broadcasted_iota(jnp.int32, (N_ATT_HEADS, 1), 0)
    slope = jnp.exp2((hidx + 1).astype(f32) * (-8.0 / N_ATT_HEADS))
    slot = lax.broadcasted_iota(jnp.int32, (J, N_ATT_HEADS, 1), 0)
    q = [qkv_ref[i, 0] for i in range(t_new)]
    kn = [qkv_ref[i, 1] for i in range(t_new)]
    vn = [qkv_ref[i, 2] for i in range(t_new)]
    for i in range(t_new):
        s_new = [jnp.sum(q[i] * kn[n], axis=-1, keepdims=True) * ATT_SCALE for n in range(i + 1)]
        outs, lses = [], []
        for dil in DILATIONS:
            if dil == 1:
                kc, vc = k1_ref[...], v1_ref[...]
                dist = (J + i - slot).astype(f32)
                news = [(s_new[n] - slope * float(i - n), vn[n]) for n in range(i + 1)]
            else:
                kres_ref, vres_ref = (k4_ref, v4_ref) if dil == 4 else (k16_ref, v16_ref)
                kc, vc = kres_ref[:, i], vres_ref[:, i]
                dist = (J - slot).astype(f32)
                news = [(s_new[i], vn[i])]
            s = jnp.sum(kc * q[i][None], axis=-1, keepdims=True) * ATT_SCALE - (slope * float(dil))[None] * dist
            if dil == 1:
                s = jnp.where(slot >= i, s, NEG_INF)
            m = jnp.max(s, axis=0)
            for sn, _ in news:
                m = jnp.maximum(m, sn)
            p = jnp.exp(s - m[None])
            l = jnp.sum(p, axis=0)
            acc = jnp.sum(p * vc, axis=0)
            for sn, v in news:
                pn = jnp.exp(sn - m)
                l = l + pn
                acc = acc + pn * v
            outs.append(acc / l)
            lses.append(m + jnp.log(l))
        mm = jnp.maximum(jnp.maximum(lses[0], lses[1]), lses[2])
        ws = [jnp.exp(x - mm) for x in lses]
        tot = ws[0] + ws[1] + ws[2]
        o_ref[i] = (ws[0] * outs[0] + ws[1] * outs[1] + ws[2] * outs[2]) / tot


def _att_sample(qkv_s, cache_k, cache_v, *, batch, t_new):
    w_cache = cache_k.shape[1]
    J = SUB_WINDOW
    he = (N_ATT_HEADS, ATT_HEAD_DIM)
    assert w_cache == max(DILATIONS) * J and t_new <= 4
    specs, views = [], []
    for dil in DILATIONS:
        rows = w_cache // dil
        last = rows // J - 1
        for c in (cache_k, cache_v):
            if dil == 1:
                views.append(c)
                specs.append(pl.BlockSpec((None, J) + he, lambda b, last=last: (b, last, 0, 0)))
            else:
                views.append(c.reshape((batch, rows, dil) + he))
                specs.append(pl.BlockSpec((None, J, 4) + he, lambda b, last=last: (b, last, 0, 0, 0)))
    kern = functools.partial(_att_sample_body, t_new=t_new)
    return pl.pallas_call(
        kern,
        grid=(batch,),
        in_specs=[pl.BlockSpec((None, t_new, 3) + he, lambda b: (b, 0, 0, 0, 0))] + specs,
        out_specs=pl.BlockSpec((None, t_new) + he, lambda b: (b, 0, 0, 0)),
        out_shape=jax.ShapeDtypeStruct((batch, t_new) + he, f32),
        compiler_params=_cparams("parallel"),
        name="att_sample",
    )(qkv_s, *views)


def _outproj_body(*refs, n_att):
    x_ref, go_ref, gg_ref = refs[0:3]
    o_refs = refs[3:3 + n_att]
    lse_refs = refs[3 + n_att:3 + 2 * n_att] if n_att > 1 else ()
    gnw_ref, anw_ref, wg_ref, wa_ref, out_ref = refs[3 + (2 * n_att if n_att > 1 else 1):]
    tm = x_ref.shape[0]
    parts = []
    for h in range(N_GLA_HEADS):
        vs = slice(h * GLA_DV, (h + 1) * GLA_DV)
        y = _rms(go_ref[:, vs], gnw_ref[...])
        gate = gg_ref[:, vs]
        parts.append((y * (gate * jax.nn.sigmoid(gate))).astype(bf16))
    gla_part = jnp.concatenate(parts, axis=-1)
    if n_att == 1:
        att = o_refs[0][...]
    else:
        heads = []
        for h in range(N_ATT_HEADS):
            hs = slice(h * ATT_HEAD_DIM, (h + 1) * ATT_HEAD_DIM)
            ls = [jnp.broadcast_to(r[:, h:h + 1], (tm, ATT_HEAD_DIM)) for r in lse_refs]
            mm = ls[0]
            for x in ls[1:]:
                mm = jnp.maximum(mm, x)
            ws = [jnp.exp(x - mm) for x in ls]
            tot = ws[0]
            num = ws[0] * o_refs[0][:, hs]
            for w, o in zip(ws[1:], o_refs[1:]):
                tot = tot + w
                num = num + w * o[:, hs]
            heads.append(num / tot)
        att = jnp.concatenate(heads, axis=-1)
    att_part = _rms(att, anw_ref[...]).astype(bf16)
    mix = jnp.dot(gla_part, wg_ref[...], preferred_element_type=f32)
    mix = mix + jnp.dot(att_part, wa_ref[...], preferred_element_type=f32)
    out_ref[...] = x_ref[...] + mix


def _outproj(x, gla_o, gg_src, gg_col, att_os, att_lses, gla_norm_w, att_norm_w, w_out_bf, tm):
    n = x.shape[0]
    n_att = len(att_os)
    row = lambda i: (i, 0)
    const = lambda i: (0, 0)
    in_specs = [
        pl.BlockSpec((tm, D_MODEL), row),
        pl.BlockSpec((tm, GLA_WIDTH), row),
        pl.BlockSpec((tm, GLA_WIDTH), lambda i: (i, gg_col)),
    ]
    in_specs += [pl.BlockSpec((tm, ATT_WIDTH), row) for _ in att_os]
    in_specs += [pl.BlockSpec((tm, LANES), row) for _ in att_lses]
    in_specs += [
        pl.BlockSpec((1, GLA_DV), const),
        pl.BlockSpec((1, ATT_WIDTH), const),
        pl.BlockSpec((GLA_WIDTH, D_MODEL), lambda i: (0, 0)),
        pl.BlockSpec((ATT_WIDTH, D_MODEL), lambda i: (1, 0)),
    ]
    return pl.pallas_call(
        functools.partial(_outproj_body, n_att=n_att),
        grid=(n // tm,),
        in_specs=in_specs,
        out_specs=pl.BlockSpec((tm, D_MODEL), row),
        out_shape=jax.ShapeDtypeStruct((n, D_MODEL), f32),
        compiler_params=_cparams("parallel"),
        name="mixer_out_proj",
    )(x, gla_o, gg_src, *att_os, *att_lses, gla_norm_w.reshape(1, GLA_DV), att_norm_w.reshape(1, ATT_WIDTH),
      w_out_bf, w_out_bf)


def _mlp_body(x_ref, fw_ref, wu_ref, wd_ref, nw_ref, o_ref, h_scr):
    f = pl.program_id(1)

    @pl.when(f == 0)
    def _():
        h_scr[...] = _rms(x_ref[...], fw_ref[...]).astype(bf16)
        o_ref[...] = jnp.zeros_like(o_ref)

    u = jnp.dot(h_scr[...], wu_ref[...], preferred_element_type=f32)
    a = jnp.square(jnp.maximum(u, 0.0)).astype(bf16)
    o_ref[...] += jnp.dot(a, wd_ref[...], preferred_element_type=f32)

    @pl.when(f == pl.num_programs(1) - 1)
    def _():
        o_ref[...] = _rms(x_ref[...] + o_ref[...], nw_ref[...])


def _mlp(x, ffn_norm_w, w_up_bf, w_down_bf, final_norm_w, tm, tf):
    n = x.shape[0]
    return pl.pallas_call(
        _mlp_body,
        grid=(n // tm, D_FF // tf),
        in_specs=[
            pl.BlockSpec((tm, D_MODEL), lambda i, f: (i, 0)),
            pl.BlockSpec((1, D_MODEL), lambda i, f: (0, 0)),
            pl.BlockSpec((D_MODEL, tf), lambda i, f: (0, f)),
            pl.BlockSpec((tf, D_MODEL), lambda i, f: (f, 0)),
            pl.BlockSpec((1, D_MODEL), lambda i, f: (0, 0)),
        ],
        out_specs=pl.BlockSpec((tm, D_MODEL), lambda i, f: (i, 0)),
        out_shape=jax.ShapeDtypeStruct((n, D_MODEL), f32),
        scratch_shapes=[pltpu.VMEM((tm, D_MODEL), bf16)],
        compiler_params=_cparams("parallel", "arbitrary"),
        name="mlp_final_norm",
    )(x, ffn_norm_w.reshape(1, D_MODEL), w_up_bf, w_down_bf, final_norm_w.reshape(1, D_MODEL))


def _project(x2d, norm_w, w_gla_bf, w_att_bf, wlr_pad, wup_pad, b_gk, tm, att_dtype):
    h = _norm(x2d, norm_w, min(tm, 512))
    pg = _mm(h, w_gla_bf, out_dtype=f32, tm=tm, tn=1024, col0=0, ncols=GLA_COLS, name="proj_gla")
    loga = _gate(h, wlr_pad, wup_pad, b_gk, min(tm, 512))
    pa = _mm(h, w_att_bf, out_dtype=att_dtype, tm=tm, tn=1024, col0=0, ncols=ATT_COLS, name="proj_att")
    return h, pg, loga, pa


def kernel(x_prompt, x_sample, cache_k_win, cache_v_win, state_gla, attn_norm_w, w_in, w_gk_up, b_gk, gla_norm_w,
           att_out_norm_w, w_out, ffn_norm_w, w_up, w_down, final_norm_w):
    depth = w_in.shape[0]
    assert depth == 1, "single trunk layer"
    B, S, _ = x_prompt.shape
    Bs, Ts, _ = x_sample.shape
    w_p = min(MAX_WINDOW, S)
    assert S % (ATT_BLOCK * max(DILATIONS)) == 0 and Ts <= GLA_CHUNK

    w_in0 = w_in[0]
    c_lr = GLA_COLS
    w_gla_bf = w_in0[:, :c_lr].astype(bf16)
    wlr_pad = jnp.pad(w_in0[:, c_lr:c_lr + GLA_GATE_RANK], ((0, 0), (0, LANES - GLA_GATE_RANK))).astype(bf16)
    w_att_bf = w_in0[:, c_lr + GLA_GATE_RANK:].astype(bf16)
    wup_pad = jnp.pad(w_gk_up[0], ((0, LANES - GLA_GATE_RANK), (0, 0)))
    w_out_bf = w_out[0].astype(bf16)
    w_up_bf = w_up[0].astype(bf16)
    w_down_bf = w_down[0].astype(bf16)

    xp = x_prompt.reshape(B * S, D_MODEL)
    h, pg, loga, pa = _project(xp, attn_norm_w[0], w_gla_bf, w_att_bf, wlr_pad, wup_pad, b_gk[0], 1024, att_dtype=bf16)
    kv_rows = dict(row_groups=B, row0=S - w_p, rows_per_group=w_p)
    k_win = _mm(h, w_att_bf, out_dtype=f32, tm=1024, tn=1024, col0=ATT_WIDTH, ncols=ATT_WIDTH, name="k_window", **kv_rows)
    v_win = _mm(h, w_att_bf, out_dtype=f32, tm=1024, tn=1024, col0=2 * ATT_WIDTH, ncols=ATT_WIDTH, name="v_window", **kv_rows)
    s0 = jnp.zeros((B, N_GLA_HEADS, GLA_DK, GLA_DV), f32)
    gla_o, gla_state_p = _gla(pg, loga, s0, batch=B, t_len=S, tb=512, t_valid=GLA_CHUNK)
    att = [_att_prompt(pa, batch=B, seq=S, dil=d) for d in DILATIONS]
    x1 = _outproj(xp, gla_o, pg, 2, [a[0] for a in att], [a[1] for a in att], gla_norm_w[0], att_out_norm_w[0],
                  w_out_bf, 256)
    y_prompt = _mlp(x1, ffn_norm_w[0], w_up_bf, w_down_bf, final_norm_w, 1024, 512).reshape(B, S, D_MODEL)

    P = SAMPLE_PAD
    xs_pad = jnp.pad(x_sample, ((0, 0), (0, P - Ts), (0, 0))).reshape(Bs * P, D_MODEL)
    hs_, pgs, logas, pas = _project(xs_pad, attn_norm_w[0], w_gla_bf, w_att_bf, wlr_pad, wup_pad, b_gk[0], Bs * P,
                                    att_dtype=f32)
    qkv_s = pas.reshape(Bs, P, 3, N_ATT_HEADS, ATT_HEAD_DIM)[:, :Ts]
    gla_o_s, gla_state_s = _gla(pgs, logas, state_gla[0], batch=Bs, t_len=P, tb=P, t_valid=Ts)
    att_s = _att_sample(qkv_s, cache_k_win[0], cache_v_win[0], batch=Bs, t_new=Ts).reshape(Bs * Ts, ATT_WIDTH)
    gla_o_s = gla_o_s.reshape(Bs, P, GLA_WIDTH)[:, :Ts].reshape(Bs * Ts, GLA_WIDTH)
    gg_s = pgs.reshape(Bs, P, GLA_COLS)[:, :Ts, 2 * GLA_KEY_WIDTH + GLA_WIDTH:].reshape(Bs * Ts, GLA_WIDTH)
    xs = x_sample.reshape(Bs * Ts, D_MODEL)
    x1s = _outproj(xs, gla_o_s, gg_s, 0, [att_s], [], gla_norm_w[0], att_out_norm_w[0], w_out_bf, Bs * Ts)
    y_sample = _mlp(x1s, ffn_norm_w[0], w_up_bf, w_down_bf, final_norm_w, Bs * Ts, 512).reshape(Bs, Ts, D_MODEL)

    k_win_prompt = k_win.reshape(1, B, w_p, N_ATT_HEADS, ATT_HEAD_DIM)
    v_win_prompt = v_win.reshape(1, B, w_p, N_ATT_HEADS, ATT_HEAD_DIM)
    k_new_sample = qkv_s[:, :, 1][None]
    v_new_sample = qkv_s[:, :, 2][None]
    return (y_prompt, y_sample, k_win_prompt, v_win_prompt, gla_state_p[None], k_new_sample, v_new_sample,
            gla_state_s[None])
```

```python
import functools

import jax
import jax.numpy as jnp
from jax import lax
from jax.experimental import pallas as pl
from jax.experimental.pallas import tpu as pltpu

f32 = jnp.float32
bf16 = jnp.bfloat16

D_MODEL = 2048
N_GLA_HEADS = 4
GLA_DK = 128
GLA_DV = 256
GLA_KEY_WIDTH = N_GLA_HEADS * GLA_DK
GLA_WIDTH = N_GLA_HEADS * GLA_DV
GLA_GATE_RANK = 16
GLA_GATE_NORM = 16.0
GLA_CHUNK = 16
N_ATT_HEADS = 8
ATT_HEAD_DIM = 128
ATT_WIDTH = N_ATT_HEADS * ATT_HEAD_DIM
DILATIONS = (1, 4, 16)
SUB_WINDOW = 128
ATT_BLOCK = 128
MAX_WINDOW = 2048
D_FF = 4 * D_MODEL
RMS_EPS = 1e-6
GLA_COLS = 2 * GLA_KEY_WIDTH + 2 * GLA_WIDTH
ATT_COLS = 3 * ATT_WIDTH
LANES = 128
VMEM_LIMIT = 56 * 1024 * 1024
NEG_INF = float("-inf")
ALIBI_SLOPES = tuple(2.0 ** (-8.0 * (h + 1) / N_ATT_HEADS) for h in range(N_ATT_HEADS))
ATT_SCALE = ATT_HEAD_DIM ** -0.5

NT_DIMS = (((1,), (1,)), ((), ()))
TN_DIMS = (((0,), (0,)), ((), ()))


def _cparams(*sem):
    return pltpu.CompilerParams(dimension_semantics=sem, vmem_limit_bytes=VMEM_LIMIT)


def _rms(x, w):
    r = lax.rsqrt(jnp.mean(x * x, axis=-1, keepdims=True) + RMS_EPS)
    return (x * r) * w


def _norm_body(x_ref, w_ref, o_ref):
    o_ref[...] = _rms(x_ref[...], w_ref[...]).astype(o_ref.dtype)


def _norm(x, w, tm):
    n, d = x.shape
    return pl.pallas_call(
        _norm_body,
        grid=(n // tm,),
        in_specs=[pl.BlockSpec((tm, d), lambda i: (i, 0)), pl.BlockSpec((1, d), lambda i: (0, 0))],
        out_specs=pl.BlockSpec((tm, d), lambda i: (i, 0)),
        out_shape=jax.ShapeDtypeStruct((n, d), bf16),
        compiler_params=_cparams("parallel"),
        name="rmsnorm_cast",
    )(x, w.reshape(1, d))


def _mm_body(h_ref, w_ref, o_ref):
    o_ref[...] = jnp.dot(h_ref[...], w_ref[...], preferred_element_type=f32).astype(o_ref.dtype)


def _mm(h, w, *, out_dtype, tm, tn, col0, ncols, name, row_groups=1, row0=0, rows_per_group=None):
    n, k = h.shape
    group_stride = n // row_groups
    if rows_per_group is None:
        rows_per_group = group_stride
    assert row0 % tm == 0 and rows_per_group % tm == 0 and group_stride % tm == 0
    assert col0 % tn == 0 and ncols % tn == 0
    nb = rows_per_group // tm
    gb, r0b, c0b = group_stride // tm, row0 // tm, col0 // tn
    return pl.pallas_call(
        _mm_body,
        grid=(ncols // tn, row_groups, nb),
        in_specs=[
            pl.BlockSpec((tm, k), lambda j, g, i: (g * gb + r0b + i, 0)),
            pl.BlockSpec((k, tn), lambda j, g, i: (0, c0b + j)),
        ],
        out_specs=pl.BlockSpec((tm, tn), lambda j, g, i: (g * nb + i, j)),
        out_shape=jax.ShapeDtypeStruct((row_groups * rows_per_group, ncols), out_dtype),
        compiler_params=_cparams("parallel", "parallel", "parallel"),
        name=name,
    )(h, w)


def _mm_hm_body(h_ref, w_ref, o_ref):
    res = jnp.dot(h_ref[...], w_ref[...], preferred_element_type=f32)
    for g in range(o_ref.shape[0]):
        o_ref[g] = res[:, g * LANES:(g + 1) * LANES]


def _mm_head_major(h, w, *, tm, tn, name):
    n, k = h.shape
    m = w.shape[1]
    return pl.pallas_call(
        _mm_hm_body,
        grid=(m // tn, n // tm),
        in_specs=[pl.BlockSpec((tm, k), lambda j, i: (i, 0)), pl.BlockSpec((k, tn), lambda j, i: (0, j))],
        out_specs=pl.BlockSpec((tn // LANES, tm, LANES), lambda j, i: (j, i, 0)),
        out_shape=jax.ShapeDtypeStruct((m // LANES, n, LANES), f32),
        compiler_params=_cparams("parallel", "parallel"),
        name=name,
    )(h, w)


def _gate_body(h_ref, wlr_ref, wup_ref, b_ref, o_ref):
    glr = jnp.dot(h_ref[...], wlr_ref[...], preferred_element_type=f32)
    pre = jnp.dot(glr, wup_ref[...], precision=lax.Precision.HIGHEST, preferred_element_type=f32) + b_ref[...]
    log_sig = jnp.minimum(pre, 0.0) - jnp.log1p(jnp.exp(-jnp.abs(pre)))
    o_ref[...] = log_sig * (1.0 / GLA_GATE_NORM)


def _gate(h, wlr_pad, wup_pad, b_gk, tm):
    n, d = h.shape
    return pl.pallas_call(
        _gate_body,
        grid=(n // tm,),
        in_specs=[
            pl.BlockSpec((tm, d), lambda i: (i, 0)),
            pl.BlockSpec((d, LANES), lambda i: (0, 0)),
            pl.BlockSpec((LANES, GLA_KEY_WIDTH), lambda i: (0, 0)),
            pl.BlockSpec((1, GLA_KEY_WIDTH), lambda i: (0, 0)),
        ],
        out_specs=pl.BlockSpec((tm, GLA_KEY_WIDTH), lambda i: (i, 0)),
        out_shape=jax.ShapeDtypeStruct((n, GLA_KEY_WIDTH), f32),
        compiler_params=_cparams("parallel"),
        name="gla_gate",
    )(h, wlr_pad, wup_pad, b_gk.reshape(1, GLA_KEY_WIDTH))


GLA_SUB = 128


def _gla_body(q_ref, k_ref, v_ref, g_ref, s0_ref, o_ref, sfin_ref, st_scr, qe_scr, ke_scr, dec_scr, *, tb, t_valid):
    C = GLA_CHUNK
    t_blk = pl.program_id(1)

    @pl.when(t_blk == 0)
    def _():
        for h in range(N_GLA_HEADS):
            st_scr[h] = s0_ref[h].T

    row = lax.broadcasted_iota(jnp.int32, (tb, GLA_KEY_WIDTH), 0)
    rin = row & (C - 1)
    g = g_ref[...]
    if t_valid < C:
        g = jnp.where(rin < t_valid, g, 0.0)
    b = g
    for s in (1, 2, 4, 8):
        b = b + jnp.where(rin >= s, pltpu.roll(b, s, 0), 0.0)
    bl = b
    for s in (1, 2, 4, 8):
        bl = jnp.where(rin + s <= C - 1, pltpu.roll(bl, tb - s, 0), bl)
    q = q_ref[...] * (GLA_DK ** -0.5)
    k = k_ref[...]
    qe_scr[...] = q * jnp.exp(b)
    ke_scr[...] = k * jnp.exp(bl - b)
    dec_scr[...] = jnp.exp(bl)

    sub = min(GLA_SUB, tb)
    srow = lax.broadcasted_iota(jnp.int32, (sub, GLA_DK), 0)
    slane = lax.broadcasted_iota(jnp.int32, (sub, GLA_DK), 1)
    srin = srow & (C - 1)
    for sb in range(tb // sub):
        r0 = sb * sub
        for h in range(N_GLA_HEADS):
            ks = slice(h * GLA_DK, (h + 1) * GLA_DK)
            vs = slice(h * GLA_DV, (h + 1) * GLA_DV)
            qh = q[r0:r0 + sub, ks]
            kh = k[r0:r0 + sub, ks]
            bh = b[r0:r0 + sub, ks]
            a = jnp.zeros((sub, LANES), f32)
            for d in range(C):
                k_d = kh if d == 0 else pltpu.roll(kh, d, 0)
                b_d = bh if d == 0 else pltpu.roll(bh, d, 0)
                e = jnp.exp(jnp.where(srin >= d, bh - b_d, NEG_INF))
                a_d = jnp.sum(qh * k_d * e, axis=-1, keepdims=True)
                a = jnp.where(slane == srow - d, a_d, a)
            vh = v_ref[r0:r0 + sub, vs]
            o_ref[r0:r0 + sub, vs] = jnp.dot(a[:, :sub].astype(bf16), vh.astype(bf16), preferred_element_type=f32)

    def chunk_step(c, carry):
        c0 = pl.multiple_of(c * C, C)
        for h in range(N_GLA_HEADS):
            ks = slice(h * GLA_DK, (h + 1) * GLA_DK)
            vs = slice(h * GLA_DV, (h + 1) * GLA_DV)
            st = st_scr[h]
            qe_c = qe_scr[pl.ds(c0, C), ks]
            ke_c = ke_scr[pl.ds(c0, C), ks]
            v_c = v_ref[pl.ds(c0, C), vs]
            dec = dec_scr[pl.ds(c0, 1), ks]
            o_inter = lax.dot_general(qe_c.astype(bf16), st.astype(bf16), NT_DIMS, preferred_element_type=f32)
            o_ref[pl.ds(c0, C), vs] = o_ref[pl.ds(c0, C), vs] + o_inter
            kv_t = lax.dot_general(v_c.astype(bf16), ke_c.astype(bf16), TN_DIMS, preferred_element_type=f32)
            st_scr[h] = dec * st + kv_t
        return carry

    lax.fori_loop(0, tb // C, chunk_step, 0)

    @pl.when(t_blk == pl.num_programs(1) - 1)
    def _():
        for h in range(N_GLA_HEADS):
            sfin_ref[h] = st_scr[h].T


def _gla(pg, loga, s0, *, batch, t_len, tb, t_valid):
    nt = t_len // tb
    kern = functools.partial(_gla_body, tb=tb, t_valid=t_valid)
    state_spec = pl.BlockSpec((None, N_GLA_HEADS, GLA_DK, GLA_DV), lambda b, t: (b, 0, 0, 0))
    return pl.pallas_call(
        kern,
        grid=(batch, nt),
        in_specs=[
            pl.BlockSpec((tb, GLA_KEY_WIDTH), lambda b, t: (b * nt + t, 0)),
            pl.BlockSpec((tb, GLA_KEY_WIDTH), lambda b, t: (b * nt + t, 1)),
            pl.BlockSpec((tb, GLA_WIDTH), lambda b, t: (b * nt + t, 1)),
            pl.BlockSpec((tb, GLA_KEY_WIDTH), lambda b, t: (b * nt + t, 0)),
            state_spec,
        ],
        out_specs=[pl.BlockSpec((tb, GLA_WIDTH), lambda b, t: (b * nt + t, 0)), state_spec],
        out_shape=[
            jax.ShapeDtypeStruct((batch * t_len, GLA_WIDTH), f32),
            jax.ShapeDtypeStruct((batch, N_GLA_HEADS, GLA_DK, GLA_DV), f32),
        ],
        scratch_shapes=[
            pltpu.VMEM((N_GLA_HEADS, GLA_DV, GLA_DK), f32),
            pltpu.VMEM((tb, GLA_KEY_WIDTH), f32),
            pltpu.VMEM((tb, GLA_KEY_WIDTH), f32),
            pltpu.VMEM((tb, GLA_KEY_WIDTH), f32),
        ],
        compiler_params=_cparams("parallel", "arbitrary"),
        name="gla",
    )(pg, pg, pg, loga, s0)


ATT_ROWS = ATT_BLOCK * max(DILATIONS)


def _att_prompt_body(q_ref, k_ref, v_ref, o_ref, qd_scr, kd1, vd1, kd4, vd4, kd16, vd16, od_scr, lse_scr):
    U = ATT_BLOCK
    h = pl.program_id(1)
    n = pl.program_id(2)
    kds, vds = (kd1, kd4, kd16), (vd1, vd4, vd16)

    @pl.when(n == 0)
    def _():
        for d, kd, vd in zip(DILATIONS, kds, vds):
            L = ATT_ROWS // d
            for r in range(d):
                kd[r * (U + L):r * (U + L) + U, :] = jnp.zeros((U, ATT_HEAD_DIM), bf16)
                vd[r * (U + L):r * (U + L) + U, :] = jnp.zeros((U, ATT_HEAD_DIM), bf16)

    qi = lax.broadcasted_iota(jnp.int32, (U, 2 * U), 0)
    ki = lax.broadcasted_iota(jnp.int32, (U, 2 * U), 1)
    dist = qi - ki + U
    in_window = (dist >= 0) & (dist <= SUB_WINDOW)
    slope = jnp.exp2(jnp.zeros((U, 2 * U), f32) - (h + 1).astype(f32) * (8.0 / N_ATT_HEADS))
    first_pen = jnp.where(n == 0, NEG_INF, 0.0)

    for di, d in enumerate(DILATIONS):
        L = ATT_ROWS // d
        nu = L // U
        kd, vd = kds[di], vds[di]
        bias = jnp.where(in_window, -(slope * float(d)) * dist.astype(f32), NEG_INF)
        for r in range(d):
            src = pl.ds(r, L, stride=d) if d > 1 else slice(None)
            base = r * (U + L)
            qd_scr[r * L:(r + 1) * L, :] = (q_ref[src, :] * ATT_SCALE).astype(bf16)
            kd[base + U:base + U + L, :] = k_ref[src, :].astype(bf16)
            vd[base + U:base + U + L, :] = v_ref[src, :].astype(bf16)

        def unit(idx, carry, d=d, L=L, nu=nu, kd=kd, vd=vd, bias=bias, di=di):
            r = idx // nu
            u = idx - r * nu
            qoff = pl.multiple_of(r * L + u * U, U)
            koff = pl.multiple_of(r * (U + L) + u * U, U)
            qq = qd_scr[pl.ds(qoff, U), :]
            kk = kd[pl.ds(koff, 2 * U), :]
            vv = vd[pl.ds(koff, 2 * U), :]
            pen = jnp.where(u == 0, first_pen, 0.0)
            s = lax.dot_general(qq, kk, NT_DIMS, preferred_element_type=f32) + bias
            s = s + jnp.where(ki < U, pen, 0.0)
            m = jnp.max(s, axis=-1, keepdims=True)
            p = jnp.exp(s - m)
            l = jnp.sum(p, axis=-1, keepdims=True)
            o = jnp.dot(p.astype(bf16), vv, preferred_element_type=f32) / l
            rows = pl.ds(u * (U * d) + r, U, stride=d) if d > 1 else pl.ds(pl.multiple_of(u * U, U), U)
            od_scr[di, rows, :] = o
            lse_scr[di, rows, :] = jnp.broadcast_to(m + jnp.log(l), (U, ATT_HEAD_DIM))
            return carry

        lax.fori_loop(0, d * nu, unit, 0, unroll=8)
        for r in range(d):
            base = r * (U + L)
            kd[base:base + U, :] = kd[base + L:base + L + U, :]
            vd[base:base + U, :] = vd[base + L:base + L + U, :]

    CH = 512
    for c in range(ATT_ROWS // CH):
        rs = slice(c * CH, (c + 1) * CH)
        ls = [lse_scr[di, rs, :] for di in range(len(DILATIONS))]
        mm = jnp.maximum(jnp.maximum(ls[0], ls[1]), ls[2])
        ws = [jnp.exp(x - mm) for x in ls]
        num = ws[0] * od_scr[0, rs, :] + ws[1] * od_scr[1, rs, :] + ws[2] * od_scr[2, rs, :]
        o_ref[rs, :] = num / (ws[0] + ws[1] + ws[2])


def _att_prompt(pa_hm, *, batch, seq):
    nb = seq // ATT_ROWS
    H, E, U = N_ATT_HEADS, ATT_HEAD_DIM, ATT_BLOCK
    blk = (None, ATT_ROWS, E)
    kv_scratch = []
    for d in DILATIONS:
        kv_scratch += [pltpu.VMEM((d * U + ATT_ROWS, E), bf16)] * 2
    return pl.pallas_call(
        _att_prompt_body,
        grid=(batch, H, nb),
        in_specs=[
            pl.BlockSpec(blk, lambda b, h, n: (h, b * nb + n, 0)),
            pl.BlockSpec(blk, lambda b, h, n: (H + h, b * nb + n, 0)),
            pl.BlockSpec(blk, lambda b, h, n: (2 * H + h, b * nb + n, 0)),
        ],
        out_specs=pl.BlockSpec((ATT_ROWS, E), lambda b, h, n: (b * nb + n, h)),
        out_shape=jax.ShapeDtypeStruct((batch * seq, H * E), f32),
        scratch_shapes=[pltpu.VMEM((ATT_ROWS, E), bf16)] + kv_scratch + [
            pltpu.VMEM((len(DILATIONS), ATT_ROWS, E), f32),
            pltpu.VMEM((len(DILATIONS), ATT_ROWS, E), f32),
        ],
        compiler_params=_cparams("parallel", "parallel", "arbitrary"),
        name="att_prompt",
    )(pa_hm, pa_hm, pa_hm)


SAMPLE_PAD = 16


def _att_sample_body(qkv_ref, k1_ref, v1_ref, k4_ref, v4_ref, k16_ref, v16_ref, o_ref, *, t_new):
    J = SUB_WINDOW
    hidx = lax.broadcasted_iota(jnp.int32, (N_ATT_HEADS, 1), 0)
    slope = jnp.exp2((hidx + 1).astype(f32) * (-8.0 / N_ATT_HEADS))
    slot = lax.broadcasted_iota(jnp.int32, (J, N_ATT_HEADS, 1), 0)
    q = [qkv_ref[i, 0] for i in range(t_new)]
    kn = [qkv_ref[i, 1] for i in range(t_new)]
    vn = [qkv_ref[i, 2] for i in range(t_new)]
    for i in range(t_new):
        s_new = [jnp.sum(q[i] * kn[n], axis=-1, keepdims=True) * ATT_SCALE for n in range(i + 1)]
        outs, lses = [], []
        for dil in DILATIONS:
            if dil == 1:
                kc, vc = k1_ref[...], v1_ref[...]
                dist = (J + i - slot).astype(f32)
                news = [(s_new[n] - slope * float(i - n), vn[n]) for n in range(i + 1)]
            else:
                kres_ref, vres_ref = (k4_ref, v4_ref) if dil == 4 else (k16_ref, v16_ref)
                kc, vc = kres_ref[:, i], vres_ref[:, i]
                dist = (J - slot).astype(f32)
                news = [(s_new[i], vn[i])]
            s = jnp.sum(kc * q[i][None], axis=-1, keepdims=True) * ATT_SCALE - (slope * float(dil))[None] * dist
            if dil == 1:
                s = jnp.where(slot >= i, s, NEG_INF)
            m = jnp.max(s, axis=0)
            for sn, _ in news:
                m = jnp.maximum(m, sn)
            p = jnp.exp(s - m[None])
            l = jnp.sum(p, axis=0)
            acc = jnp.sum(p * vc, axis=0)
            for sn, v in news:
                pn = jnp.exp(sn - m)
                l = l + pn
                acc = acc + pn * v
            outs.append(acc / l)
            lses.append(m + jnp.log(l))
        mm = jnp.maximum(jnp.maximum(lses[0], lses[1]), lses[2])
        ws = [jnp.exp(x - mm) for x in lses]
        tot = ws[0] + ws[1] + ws[2]
        o_ref[i] = (ws[0] * outs[0] + ws[1] * outs[1] + ws[2] * outs[2]) / tot


def _att_sample(qkv_s, cache_k, cache_v, *, batch, t_new):
    w_cache = cache_k.shape[1]
    J = SUB_WINDOW
    he = (N_ATT_HEADS, ATT_HEAD_DIM)
    assert w_cache == max(DILATIONS) * J and t_new <= 4
    specs, views = [], []
    for dil in DILATIONS:
        rows = w_cache // dil
        last = rows // J - 1
        for c in (cache_k, cache_v):
            if dil == 1:
                views.append(c)
                specs.append(pl.BlockSpec((None, J) + he, lambda b, last=last: (b, last, 0, 0)))
            else:
                views.append(c.reshape((batch, rows, dil) + he))
                specs.append(pl.BlockSpec((None, J, 4) + he, lambda b, last=last: (b, last, 0, 0, 0)))
    kern = functools.partial(_att_sample_body, t_new=t_new)
    return pl.pallas_call(
        kern,
        grid=(batch,),
        in_specs=[pl.BlockSpec((None, t_new, 3) + he, lambda b: (b, 0, 0, 0, 0))] + specs,
        out_specs=pl.BlockSpec((None, t_new) + he, lambda b: (b, 0, 0, 0)),
        out_shape=jax.ShapeDtypeStruct((batch, t_new) + he, f32),
        compiler_params=_cparams("parallel"),
        name="att_sample",
    )(qkv_s, *views)


def _outproj_body(x_ref, go_ref, gg_ref, att_ref, gnw_ref, anw_ref, wg_ref, wa_ref, out_ref):
    parts = []
    for h in range(N_GLA_HEADS):
        vs = slice(h * GLA_DV, (h + 1) * GLA_DV)
        y = _rms(go_ref[:, vs], gnw_ref[...])
        gate = gg_ref[:, vs]
        parts.append((y * (gate * jax.nn.sigmoid(gate))).astype(bf16))
    gla_part = jnp.concatenate(parts, axis=-1)
    att_part = _rms(att_ref[...], anw_ref[...]).astype(bf16)
    mix = jnp.dot(gla_part, wg_ref[...], preferred_element_type=f32)
    mix = mix + jnp.dot(att_part, wa_ref[...], preferred_element_type=f32)
    out_ref[...] = x_ref[...] + mix


def _outproj(x, gla_o, gg_src, gg_col, att_o, gla_norm_w, att_norm_w, w_out_bf, tm):
    n = x.shape[0]
    row = lambda i: (i, 0)
    const = lambda i: (0, 0)
    in_specs = [
        pl.BlockSpec((tm, D_MODEL), row),
        pl.BlockSpec((tm, GLA_WIDTH), row),
        pl.BlockSpec((tm, GLA_WIDTH), lambda i: (i, gg_col)),
        pl.BlockSpec((tm, ATT_WIDTH), row),
        pl.BlockSpec((1, GLA_DV), const),
        pl.BlockSpec((1, ATT_WIDTH), const),
        pl.BlockSpec((GLA_WIDTH, D_MODEL), lambda i: (0, 0)),
        pl.BlockSpec((ATT_WIDTH, D_MODEL), lambda i: (1, 0)),
    ]
    return pl.pallas_call(
        _outproj_body,
        grid=(n // tm,),
        in_specs=in_specs,
        out_specs=pl.BlockSpec((tm, D_MODEL), row),
        out_shape=jax.ShapeDtypeStruct((n, D_MODEL), f32),
        compiler_params=_cparams("parallel"),
        name="mixer_out_proj",
    )(x, gla_o, gg_src, att_o, gla_norm_w.reshape(1, GLA_DV), att_norm_w.reshape(1, ATT_WIDTH), w_out_bf, w_out_bf)


def _mlp_body(x_ref, fw_ref, wu_ref, wd_ref, nw_ref, o_ref, h_scr):
    f = pl.program_id(1)

    @pl.when(f == 0)
    def _():
        h_scr[...] = _rms(x_ref[...], fw_ref[...]).astype(bf16)
        o_ref[...] = jnp.zeros_like(o_ref)

    u = jnp.dot(h_scr[...], wu_ref[...], preferred_element_type=f32)
    a = jnp.square(jnp.maximum(u, 0.0)).astype(bf16)
    o_ref[...] += jnp.dot(a, wd_ref[...], preferred_element_type=f32)

    @pl.when(f == pl.num_programs(1) - 1)
    def _():
        o_ref[...] = _rms(x_ref[...] + o_ref[...], nw_ref[...])


def _mlp(x, ffn_norm_w, w_up_bf, w_down_bf, final_norm_w, tm, tf):
    n = x.shape[0]
    return pl.pallas_call(
        _mlp_body,
        grid=(n // tm, D_FF // tf),
        in_specs=[
            pl.BlockSpec((tm, D_MODEL), lambda i, f: (i, 0)),
            pl.BlockSpec((1, D_MODEL), lambda i, f: (0, 0)),
            pl.BlockSpec((D_MODEL, tf), lambda i, f: (0, f)),
            pl.BlockSpec((tf, D_MODEL), lambda i, f: (f, 0)),
            pl.BlockSpec((1, D_MODEL), lambda i, f: (0, 0)),
        ],
        out_specs=pl.BlockSpec((tm, D_MODEL), lambda i, f: (i, 0)),
        out_shape=jax.ShapeDtypeStruct((n, D_MODEL), f32),
        scratch_shapes=[pltpu.VMEM((tm, D_MODEL), bf16)],
        compiler_params=_cparams("parallel", "arbitrary"),
        name="mlp_final_norm",
    )(x, ffn_norm_w.reshape(1, D_MODEL), w_up_bf, w_down_bf, final_norm_w.reshape(1, D_MODEL))


def _project(x2d, norm_w, w_gla_bf, w_att_bf, wlr_pad, wup_pad, b_gk, tm, head_major):
    h = _norm(x2d, norm_w, min(tm, 512))
    pg = _mm(h, w_gla_bf, out_dtype=f32, tm=tm, tn=1024, col0=0, ncols=GLA_COLS, name="proj_gla")
    loga = _gate(h, wlr_pad, wup_pad, b_gk, min(tm, 512))
    if head_major:
        pa = _mm_head_major(h, w_att_bf, tm=tm, tn=1024, name="proj_att_heads")
    else:
        pa = _mm(h, w_att_bf, out_dtype=f32, tm=tm, tn=1024, col0=0, ncols=ATT_COLS, name="proj_att")
    return h, pg, loga, pa


def kernel(x_prompt, x_sample, cache_k_win, cache_v_win, state_gla, attn_norm_w, w_in, w_gk_up, b_gk, gla_norm_w,
           att_out_norm_w, w_out, ffn_norm_w, w_up, w_down, final_norm_w):
    depth = w_in.shape[0]
    assert depth == 1, "single trunk layer"
    B, S, _ = x_prompt.shape
    Bs, Ts, _ = x_sample.shape
    w_p = min(MAX_WINDOW, S)
    assert S % (ATT_BLOCK * max(DILATIONS)) == 0 and Ts <= GLA_CHUNK

    w_in0 = w_in[0]
    c_lr = GLA_COLS
    w_gla_bf = w_in0[:, :c_lr].astype(bf16)
    wlr_pad = jnp.pad(w_in0[:, c_lr:c_lr + GLA_GATE_RANK], ((0, 0), (0, LANES - GLA_GATE_RANK))).astype(bf16)
    w_att_bf = w_in0[:, c_lr + GLA_GATE_RANK:].astype(bf16)
    wup_pad = jnp.pad(w_gk_up[0], ((0, LANES - GLA_GATE_RANK), (0, 0)))
    w_out_bf = w_out[0].astype(bf16)
    w_up_bf = w_up[0].astype(bf16)
    w_down_bf = w_down[0].astype(bf16)

    xp = x_prompt.reshape(B * S, D_MODEL)
    h, pg, loga, pa = _project(xp, attn_norm_w[0], w_gla_bf, w_att_bf, wlr_pad, wup_pad, b_gk[0], 1024, head_major=True)
    kv_rows = dict(row_groups=B, row0=S - w_p, rows_per_group=w_p)
    k_win = _mm(h, w_att_bf, out_dtype=f32, tm=1024, tn=1024, col0=ATT_WIDTH, ncols=ATT_WIDTH, name="k_window", **kv_rows)
    v_win = _mm(h, w_att_bf, out_dtype=f32, tm=1024, tn=1024, col0=2 * ATT_WIDTH, ncols=ATT_WIDTH, name="v_window", **kv_rows)
    s0 = jnp.zeros((B, N_GLA_HEADS, GLA_DK, GLA_DV), f32)
    gla_o, gla_state_p = _gla(pg, loga, s0, batch=B, t_len=S, tb=512, t_valid=GLA_CHUNK)
    att = _att_prompt(pa, batch=B, seq=S)
    x1 = _outproj(xp, gla_o, pg, 2, att, gla_norm_w[0], att_out_norm_w[0], w_out_bf, 256)
    y_prompt = _mlp(x1, ffn_norm_w[0], w_up_bf, w_down_bf, final_norm_w, 1024, 512).reshape(B, S, D_MODEL)

    P = SAMPLE_PAD
    xs_pad = jnp.pad(x_sample, ((0, 0), (0, P - Ts), (0, 0))).reshape(Bs * P, D_MODEL)
    hs_, pgs, logas, pas = _project(xs_pad, attn_norm_w[0], w_gla_bf, w_att_bf, wlr_pad, wup_pad, b_gk[0], Bs * P,
                                    head_major=False)
    qkv_s = pas.reshape(Bs, P, 3, N_ATT_HEADS, ATT_HEAD_DIM)[:, :Ts]
    gla_o_s, gla_state_s = _gla(pgs, logas, state_gla[0], batch=Bs, t_len=P, tb=P, t_valid=Ts)
    att_s = _att_sample(qkv_s, cache_k_win[0], cache_v_win[0], batch=Bs, t_new=Ts).reshape(Bs * Ts, ATT_WIDTH)
    gla_o_s = gla_o_s.reshape(Bs, P, GLA_WIDTH)[:, :Ts].reshape(Bs * Ts, GLA_WIDTH)
    gg_s = pgs.reshape(Bs, P, GLA_COLS)[:, :Ts, 2 * GLA_KEY_WIDTH + GLA_WIDTH:].reshape(Bs * Ts, GLA_WIDTH)
    xs = x_sample.reshape(Bs * Ts, D_MODEL)
    x1s = _outproj(xs, gla_o_s, gg_s, 0, att_s, gla_norm_w[0], att_out_norm_w[0], w_out_bf, Bs * Ts)
    y_sample = _mlp(x1s, ffn_norm_w[0], w_up_bf, w_down_bf, final_norm_w, Bs * Ts, 512).reshape(Bs, Ts, D_MODEL)

    k_win_prompt = k_win.reshape(1, B, w_p, N_ATT_HEADS, ATT_HEAD_DIM)
    v_win_prompt = v_win.reshape(1, B, w_p, N_ATT_HEADS, ATT_HEAD_DIM)
    k_new_sample = qkv_s[:, :, 1][None]
    v_new_sample = qkv_s[:, :, 2][None]
    return (y_prompt, y_sample, k_win_prompt, v_win_prompt, gla_state_p[None], k_new_sample, v_new_sample,
            gla_state_s[None])
```

```python
import functools

import jax
import jax.numpy as jnp
from jax import lax
from jax.experimental import pallas as pl
from jax.experimental.pallas import tpu as pltpu

f32 = jnp.float32
bf16 = jnp.bfloat16

D_MODEL = 2048
N_GLA_HEADS = 4
GLA_DK = 128
GLA_DV = 256
GLA_KEY_WIDTH = N_GLA_HEADS * GLA_DK
GLA_WIDTH = N_GLA_HEADS * GLA_DV
GLA_GATE_RANK = 16
GLA_GATE_NORM = 16.0
GLA_CHUNK = 16
N_ATT_HEADS = 8
ATT_HEAD_DIM = 128
ATT_WIDTH = N_ATT_HEADS * ATT_HEAD_DIM
DILATIONS = (1, 4, 16)
SUB_WINDOW = 128
ATT_BLOCK = 128
MAX_WINDOW = 2048
D_FF = 4 * D_MODEL
RMS_EPS = 1e-6
GLA_COLS = 2 * GLA_KEY_WIDTH + 2 * GLA_WIDTH
ATT_COLS = 3 * ATT_WIDTH
LANES = 128
VMEM_LIMIT = 56 * 1024 * 1024
NEG_INF = float("-inf")
ALIBI_SLOPES = tuple(2.0 ** (-8.0 * (h + 1) / N_ATT_HEADS) for h in range(N_ATT_HEADS))
ATT_SCALE = ATT_HEAD_DIM ** -0.5

NT_DIMS = (((1,), (1,)), ((), ()))
TN_DIMS = (((0,), (0,)), ((), ()))


def _cparams(*sem):
    return pltpu.CompilerParams(dimension_semantics=sem, vmem_limit_bytes=VMEM_LIMIT)


def _rms(x, w):
    r = lax.rsqrt(jnp.mean(x * x, axis=-1, keepdims=True) + RMS_EPS)
    return (x * r) * w


PROJ_TN = 1024
N_GLA_TILES = GLA_COLS // PROJ_TN
N_PROJ_TILES = (GLA_COLS + ATT_COLS) // PROJ_TN


def _proj_body(x_ref, nw_ref, w_ref, wlr_ref, wup_ref, bgk_ref, pg_ref, loga_ref, pa_ref, *rest,
               head_major, win_tiles, tiles_per_seq):
    h_scr = rest[-1]
    i = pl.program_id(0)
    j = pl.program_id(1)

    @pl.when(j == 0)
    def _():
        h = _rms(x_ref[...], nw_ref[...]).astype(bf16)
        h_scr[...] = h
        glr = jnp.dot(h, wlr_ref[...], preferred_element_type=f32)
        pre = jnp.dot(glr, wup_ref[...], precision=lax.Precision.HIGHEST, preferred_element_type=f32) + bgk_ref[...]
        log_sig = jnp.minimum(pre, 0.0) - jnp.log1p(jnp.exp(-jnp.abs(pre)))
        loga_ref[...] = log_sig * (1.0 / GLA_GATE_NORM)

    res = jnp.dot(h_scr[...], w_ref[...], preferred_element_type=f32)
    tm = res.shape[0]

    @pl.when(j < N_GLA_TILES)
    def _():
        pg_ref[...] = res

    @pl.when(j >= N_GLA_TILES)
    def _():
        if head_major:
            for g in range(N_ATT_HEADS):
                pa_ref[g] = res[:, g * LANES:(g + 1) * LANES]
        else:
            pa_ref[...] = res

    if win_tiles:
        in_window = (i % tiles_per_seq) >= tiles_per_seq - win_tiles
        for which, ref in ((1, rest[0]), (2, rest[1])):
            @pl.when(in_window & (j == N_GLA_TILES + which))
            def _(ref=ref):
                for g in range(N_ATT_HEADS):
                    ref[pl.ds(g, tm, stride=N_ATT_HEADS), :] = res[:, g * LANES:(g + 1) * LANES]


def _project(x2d, norm_w, w_cat_bf, wlr_pad, wup_pad, b_gk, *, tm, head_major, seq=None, window=0):
    n, d = x2d.shape
    nb = n // tm
    nj = N_PROJ_TILES
    att_j = lambda j: jnp.maximum(j - N_GLA_TILES, 0)
    in_specs = [
        pl.BlockSpec((tm, d), lambda i, j: (i, 0)),
        pl.BlockSpec((1, d), lambda i, j: (0, 0)),
        pl.BlockSpec((d, PROJ_TN), lambda i, j: (0, j)),
        pl.BlockSpec((d, LANES), lambda i, j: (0, 0)),
        pl.BlockSpec((LANES, GLA_KEY_WIDTH), lambda i, j: (0, 0)),
        pl.BlockSpec((1, GLA_KEY_WIDTH), lambda i, j: (0, 0)),
    ]
    out_specs = [
        pl.BlockSpec((tm, PROJ_TN), lambda i, j: (i, jnp.minimum(j, N_GLA_TILES - 1))),
        pl.BlockSpec((tm, GLA_KEY_WIDTH), lambda i, j: (i, 0)),
    ]
    out_shape = [jax.ShapeDtypeStruct((n, GLA_COLS), f32), jax.ShapeDtypeStruct((n, GLA_KEY_WIDTH), f32)]
    if head_major:
        out_specs.append(pl.BlockSpec((N_ATT_HEADS, tm, LANES), lambda i, j: (att_j(j), i, 0)))
        out_shape.append(jax.ShapeDtypeStruct((3 * N_ATT_HEADS, n, ATT_HEAD_DIM), f32))
    else:
        out_specs.append(pl.BlockSpec((tm, PROJ_TN), lambda i, j: (i, att_j(j))))
        out_shape.append(jax.ShapeDtypeStruct((n, ATT_COLS), f32))
    win_tiles = tiles_per_seq = 0
    if window:
        assert seq % tm == 0 and window % tm == 0
        tiles_per_seq, win_tiles = seq // tm, window // tm
        H = N_ATT_HEADS

        def win_block(i, j):
            return ((i // tiles_per_seq) * win_tiles + jnp.maximum(i % tiles_per_seq - (tiles_per_seq - win_tiles), 0), 0)

        for _ in range(2):
            out_specs.append(pl.BlockSpec((tm * H, ATT_HEAD_DIM), win_block))
            out_shape.append(jax.ShapeDtypeStruct(((n // seq) * window * H, ATT_HEAD_DIM), f32))
    kern = functools.partial(_proj_body, head_major=head_major, win_tiles=win_tiles, tiles_per_seq=tiles_per_seq)
    return pl.pallas_call(
        kern,
        grid=(nb, nj),
        in_specs=in_specs,
        out_specs=out_specs,
        out_shape=out_shape,
        scratch_shapes=[pltpu.VMEM((tm, d), bf16)],
        compiler_params=_cparams("arbitrary", "arbitrary"),
        name="norm_in_proj",
    )(x2d, norm_w.reshape(1, d), w_cat_bf, wlr_pad, wup_pad, b_gk.reshape(1, GLA_KEY_WIDTH))


GLA_SUB = 128


def _gla_body(q_ref, k_ref, v_ref, g_ref, s0_ref, o_ref, sfin_ref, st_scr, qe_scr, ke_scr, dec_scr, *, tb, t_valid):
    C = GLA_CHUNK
    t_blk = pl.program_id(1)

    @pl.when(t_blk == 0)
    def _():
        for h in range(N_GLA_HEADS):
            st_scr[h] = s0_ref[h].T

    row = lax.broadcasted_iota(jnp.int32, (tb, GLA_KEY_WIDTH), 0)
    rin = row & (C - 1)
    g = g_ref[...]
    if t_valid < C:
        g = jnp.where(rin < t_valid, g, 0.0)
    b = g
    for s in (1, 2, 4, 8):
        b = b + jnp.where(rin >= s, pltpu.roll(b, s, 0), 0.0)
    bl = b
    for s in (1, 2, 4, 8):
        bl = jnp.where(rin + s <= C - 1, pltpu.roll(bl, tb - s, 0), bl)
    q = q_ref[...] * (GLA_DK ** -0.5)
    k = k_ref[...]
    qe_scr[...] = q * jnp.exp(b)
    ke_scr[...] = k * jnp.exp(bl - b)
    dec_scr[...] = jnp.exp(bl)

    sub = min(GLA_SUB, tb)
    srow = lax.broadcasted_iota(jnp.int32, (sub, GLA_DK), 0)
    slane = lax.broadcasted_iota(jnp.int32, (sub, GLA_DK), 1)
    srin = srow & (C - 1)
    on_diag = [((slane - srow) == -d) & (srin >= d) for d in range(C)]
    for sb in range(tb // sub):
        r0 = sb * sub
        for h in range(N_GLA_HEADS):
            ks = slice(h * GLA_DK, (h + 1) * GLA_DK)
            vs = slice(h * GLA_DV, (h + 1) * GLA_DV)
            qh = q[r0:r0 + sub, ks]
            kh = k[r0:r0 + sub, ks]
            bh = b[r0:r0 + sub, ks]
            a = jnp.zeros((sub, LANES), f32)
            for d in range(C):
                k_d = kh if d == 0 else pltpu.roll(kh, d, 0)
                b_d = bh if d == 0 else pltpu.roll(bh, d, 0)
                a_d = jnp.sum(qh * k_d * jnp.exp(bh - b_d), axis=-1, keepdims=True)
                a = jnp.where(on_diag[d], a_d, a)
            vh = v_ref[r0:r0 + sub, vs]
            o_ref[r0:r0 + sub, vs] = jnp.dot(a[:, :sub].astype(bf16), vh.astype(bf16), preferred_element_type=f32)

    def chunk_step(c, carry):
        c0 = pl.multiple_of(c * C, C)
        for h in range(N_GLA_HEADS):
            ks = slice(h * GLA_DK, (h + 1) * GLA_DK)
            vs = slice(h * GLA_DV, (h + 1) * GLA_DV)
            st = st_scr[h]
            qe_c = qe_scr[pl.ds(c0, C), ks]
            ke_c = ke_scr[pl.ds(c0, C), ks]
            v_c = v_ref[pl.ds(c0, C), vs]
            dec = dec_scr[pl.ds(c0, 1), ks]
            o_inter = lax.dot_general(qe_c.astype(bf16), st.astype(bf16), NT_DIMS, preferred_element_type=f32)
            o_ref[pl.ds(c0, C), vs] = o_ref[pl.ds(c0, C), vs] + o_inter
            kv_t = lax.dot_general(v_c.astype(bf16), ke_c.astype(bf16), TN_DIMS, preferred_element_type=f32)
            st_scr[h] = dec * st + kv_t
        return carry

    lax.fori_loop(0, tb // C, chunk_step, 0, unroll=min(4, tb // C))

    @pl.when(t_blk == pl.num_programs(1) - 1)
    def _():
        for h in range(N_GLA_HEADS):
            sfin_ref[h] = st_scr[h].T


def _gla(pg, loga, s0, *, batch, t_len, tb, t_valid):
    nt = t_len // tb
    kern = functools.partial(_gla_body, tb=tb, t_valid=t_valid)
    state_spec = pl.BlockSpec((None, N_GLA_HEADS, GLA_DK, GLA_DV), lambda b, t: (b, 0, 0, 0))
    return pl.pallas_call(
        kern,
        grid=(batch, nt),
        in_specs=[
            pl.BlockSpec((tb, GLA_KEY_WIDTH), lambda b, t: (b * nt + t, 0)),
            pl.BlockSpec((tb, GLA_KEY_WIDTH), lambda b, t: (b * nt + t, 1)),
            pl.BlockSpec((tb, GLA_WIDTH), lambda b, t: (b * nt + t, 1)),
            pl.BlockSpec((tb, GLA_KEY_WIDTH), lambda b, t: (b * nt + t, 0)),
            state_spec,
        ],
        out_specs=[pl.BlockSpec((tb, GLA_WIDTH), lambda b, t: (b * nt + t, 0)), state_spec],
        out_shape=[
            jax.ShapeDtypeStruct((batch * t_len, GLA_WIDTH), f32),
            jax.ShapeDtypeStruct((batch, N_GLA_HEADS, GLA_DK, GLA_DV), f32),
        ],
        scratch_shapes=[
            pltpu.VMEM((N_GLA_HEADS, GLA_DV, GLA_DK), f32),
            pltpu.VMEM((tb, GLA_KEY_WIDTH), f32),
            pltpu.VMEM((tb, GLA_KEY_WIDTH), f32),
            pltpu.VMEM((tb, GLA_KEY_WIDTH), f32),
        ],
        compiler_params=_cparams("parallel", "arbitrary"),
        name="gla",
    )(pg, pg, pg, loga, s0)


ATT_ROWS = ATT_BLOCK * max(DILATIONS)


def _att_prompt_body(q_ref, k_ref, v_ref, o_ref, qd_scr, kd1, vd1, kd4, vd4, kd16, vd16, od_scr, lse_scr):
    U = ATT_BLOCK
    h = pl.program_id(1)
    n = pl.program_id(2)
    kds, vds = (kd1, kd4, kd16), (vd1, vd4, vd16)

    @pl.when(n == 0)
    def _():
        for d, kd, vd in zip(DILATIONS, kds, vds):
            L = ATT_ROWS // d
            for r in range(d):
                kd[r * (U + L):r * (U + L) + U, :] = jnp.zeros((U, ATT_HEAD_DIM), bf16)
                vd[r * (U + L):r * (U + L) + U, :] = jnp.zeros((U, ATT_HEAD_DIM), bf16)

    qi = lax.broadcasted_iota(jnp.int32, (U, 2 * U), 0)
    ki = lax.broadcasted_iota(jnp.int32, (U, 2 * U), 1)
    dist = qi - ki + U
    in_window = (dist >= 0) & (dist <= SUB_WINDOW)
    slope = jnp.exp2(jnp.zeros((U, 2 * U), f32) - (h + 1).astype(f32) * (8.0 / N_ATT_HEADS))
    first_pen = jnp.where(n == 0, NEG_INF, 0.0)

    for di, d in enumerate(DILATIONS):
        L = ATT_ROWS // d
        nu = L // U
        kd, vd = kds[di], vds[di]
        bias = jnp.where(in_window, -(slope * float(d)) * dist.astype(f32), NEG_INF)
        for r in range(d):
            src = pl.ds(r, L, stride=d) if d > 1 else slice(None)
            base = r * (U + L)
            qd_scr[r * L:(r + 1) * L, :] = (q_ref[src, :] * ATT_SCALE).astype(bf16)
            kd[base + U:base + U + L, :] = k_ref[src, :].astype(bf16)
            vd[base + U:base + U + L, :] = v_ref[src, :].astype(bf16)

        def unit(idx, carry, d=d, L=L, nu=nu, kd=kd, vd=vd, bias=bias, di=di):
            r = idx // nu
            u = idx - r * nu
            qoff = pl.multiple_of(r * L + u * U, U)
            koff = pl.multiple_of(r * (U + L) + u * U, U)
            qq = qd_scr[pl.ds(qoff, U), :]
            kk = kd[pl.ds(koff, 2 * U), :]
            vv = vd[pl.ds(koff, 2 * U), :]
            pen = jnp.where(u == 0, first_pen, 0.0)
            s = lax.dot_general(qq, kk, NT_DIMS, preferred_element_type=f32) + bias
            s = s + jnp.where(ki < U, pen, 0.0)
            m = jnp.max(s, axis=-1, keepdims=True)
            p = jnp.exp(s - m)
            l = jnp.sum(p, axis=-1, keepdims=True)
            o = jnp.dot(p.astype(bf16), vv, preferred_element_type=f32) / l
            rows = pl.ds(u * (U * d) + r, U, stride=d) if d > 1 else pl.ds(pl.multiple_of(u * U, U), U)
            od_scr[di, rows, :] = o
            lse_scr[di, rows, :] = jnp.broadcast_to(m + jnp.log(l), (U, ATT_HEAD_DIM))
            return carry

        lax.fori_loop(0, d * nu, unit, 0, unroll=8)
        for r in range(d):
            base = r * (U + L)
            kd[base:base + U, :] = kd[base + L:base + L + U, :]
            vd[base:base + U, :] = vd[base + L:base + L + U, :]

    CH = 512
    for c in range(ATT_ROWS // CH):
        rs = slice(c * CH, (c + 1) * CH)
        ls = [lse_scr[di, rs, :] for di in range(len(DILATIONS))]
        mm = jnp.maximum(jnp.maximum(ls[0], ls[1]), ls[2])
        ws = [jnp.exp(x - mm) for x in ls]
        num = ws[0] * od_scr[0, rs, :] + ws[1] * od_scr[1, rs, :] + ws[2] * od_scr[2, rs, :]
        o_ref[rs, :] = num / (ws[0] + ws[1] + ws[2])


def _att_prompt(pa_hm, *, batch, seq):
    nb = seq // ATT_ROWS
    H, E, U = N_ATT_HEADS, ATT_HEAD_DIM, ATT_BLOCK
    blk = (None, ATT_ROWS, E)
    kv_scratch = []
    for d in DILATIONS:
        kv_scratch += [pltpu.VMEM((d * U + ATT_ROWS, E), bf16)] * 2
    return pl.pallas_call(
        _att_prompt_body,
        grid=(batch, H, nb),
        in_specs=[
            pl.BlockSpec(blk, lambda b, h, n: (h, b * nb + n, 0)),
            pl.BlockSpec(blk, lambda b, h, n: (H + h, b * nb + n, 0)),
            pl.BlockSpec(blk, lambda b, h, n: (2 * H + h, b * nb + n, 0)),
        ],
        out_specs=pl.BlockSpec((ATT_ROWS, E), lambda b, h, n: (b * nb + n, h)),
        out_shape=jax.ShapeDtypeStruct((batch * seq, H * E), f32),
        scratch_shapes=[pltpu.VMEM((ATT_ROWS, E), bf16)] + kv_scratch + [
            pltpu.VMEM((len(DILATIONS), ATT_ROWS, E), f32),
            pltpu.VMEM((len(DILATIONS), ATT_ROWS, E), f32),
        ],
        compiler_params=_cparams("parallel", "parallel", "arbitrary"),
        name="att_prompt",
    )(pa_hm, pa_hm, pa_hm)


SAMPLE_PAD = 16


def _att_sample_body(qkv_ref, k1_ref, v1_ref, k4_ref, v4_ref, k16_ref, v16_ref, o_ref, *, t_new):
    J = SUB_WINDOW
    hidx = lax.broadcasted_iota(jnp.int32, (N_ATT_HEADS, 1), 0)
    slope = jnp.exp2((hidx + 1).astype(f32) * (-8.0 / N_ATT_HEADS))
    slot = lax.broadcasted_iota(jnp.int32, (J, N_ATT_HEADS, 1), 0)
    q = [qkv_ref[i, 0] for i in range(t_new)]
    kn = [qkv_ref[i, 1] for i in range(t_new)]
    vn = [qkv_ref[i, 2] for i in range(t_new)]
    for i in range(t_new):
        s_new = [jnp.sum(q[i] * kn[n], axis=-1, keepdims=True) * ATT_SCALE for n in range(i + 1)]
        outs, lses = [], []
        for dil in DILATIONS:
            if dil == 1:
                kc, vc = k1_ref[...], v1_ref[...]
                dist = (J + i - slot).astype(f32)
                news = [(s_new[n] - slope * float(i - n), vn[n]) for n in range(i + 1)]
            else:
                kres_ref, vres_ref = (k4_ref, v4_ref) if dil == 4 else (k16_ref, v16_ref)
                kc, vc = kres_ref[:, i], vres_ref[:, i]
                dist = (J - slot).astype(f32)
                news = [(s_new[i], vn[i])]
            s = jnp.sum(kc * q[i][None], axis=-1, keepdims=True) * ATT_SCALE - (slope * float(dil))[None] * dist
            if dil == 1:
                s = jnp.where(slot >= i, s, NEG_INF)
            m = jnp.max(s, axis=0)
            for sn, _ in news:
                m = jnp.maximum(m, sn)
            p = jnp.exp(s - m[None])
            l = jnp.sum(p, axis=0)
            acc = jnp.sum(p * vc, axis=0)
            for sn, v in news:
                pn = jnp.exp(sn - m)
                l = l + pn
                acc = acc + pn * v
            outs.append(acc / l)
            lses.append(m + jnp.log(l))
        mm = jnp.maximum(jnp.maximum(lses[0], lses[1]), lses[2])
        ws = [jnp.exp(x - mm) for x in lses]
        tot = ws[0] + ws[1] + ws[2]
        o_ref[i] = (ws[0] * outs[0] + ws[1] * outs[1] + ws[2] * outs[2]) / tot


def _att_sample(qkv_s, cache_k, cache_v, *, batch, t_new):
    w_cache = cache_k.shape[1]
    J = SUB_WINDOW
    he = (N_ATT_HEADS, ATT_HEAD_DIM)
    assert w_cache == max(DILATIONS) * J and t_new <= 4
    specs, views = [], []
    for dil in DILATIONS:
        rows = w_cache // dil
        last = rows // J - 1
        for c in (cache_k, cache_v):
            if dil == 1:
                views.append(c)
                specs.append(pl.BlockSpec((None, J) + he, lambda b, last=last: (b, last, 0, 0)))
            else:
                views.append(c.reshape((batch, rows, dil) + he))
                specs.append(pl.BlockSpec((None, J, 4) + he, lambda b, last=last: (b, last, 0, 0, 0)))
    kern = functools.partial(_att_sample_body, t_new=t_new)
    return pl.pallas_call(
        kern,
        grid=(batch,),
        in_specs=[pl.BlockSpec((None, t_new, 3) + he, lambda b: (b, 0, 0, 0, 0))] + specs,
        out_specs=pl.BlockSpec((None, t_new) + he, lambda b: (b, 0, 0, 0)),
        out_shape=jax.ShapeDtypeStruct((batch, t_new) + he, f32),
        compiler_params=_cparams("parallel"),
        name="att_sample",
    )(qkv_s, *views)


def _outproj_body(x_ref, go_ref, gg_ref, att_ref, gnw_ref, anw_ref, wg_ref, wa_ref, out_ref):
    parts = []
    for h in range(N_GLA_HEADS):
        vs = slice(h * GLA_DV, (h + 1) * GLA_DV)
        y = _rms(go_ref[:, vs], gnw_ref[...])
        gate = gg_ref[:, vs]
        parts.append((y * (gate * jax.nn.sigmoid(gate))).astype(bf16))
    gla_part = jnp.concatenate(parts, axis=-1)
    att_part = _rms(att_ref[...], anw_ref[...]).astype(bf16)
    mix = jnp.dot(gla_part, wg_ref[...], preferred_element_type=f32)
    mix = mix + jnp.dot(att_part, wa_ref[...], preferred_element_type=f32)
    out_ref[...] = x_ref[...] + mix


def _outproj(x, gla_o, gg_src, gg_col, att_o, gla_norm_w, att_norm_w, w_out_bf, tm):
    n = x.shape[0]
    row = lambda i: (i, 0)
    const = lambda i: (0, 0)
    in_specs = [
        pl.BlockSpec((tm, D_MODEL), row),
        pl.BlockSpec((tm, GLA_WIDTH), row),
        pl.BlockSpec((tm, GLA_WIDTH), lambda i: (i, gg_col)),
        pl.BlockSpec((tm, ATT_WIDTH), row),
        pl.BlockSpec((1, GLA_DV), const),
        pl.BlockSpec((1, ATT_WIDTH), const),
        pl.BlockSpec((GLA_WIDTH, D_MODEL), lambda i: (0, 0)),
        pl.BlockSpec((ATT_WIDTH, D_MODEL), lambda i: (1, 0)),
    ]
    return pl.pallas_call(
        _outproj_body,
        grid=(n // tm,),
        in_specs=in_specs,
        out_specs=pl.BlockSpec((tm, D_MODEL), row),
        out_shape=jax.ShapeDtypeStruct((n, D_MODEL), f32),
        compiler_params=_cparams("parallel"),
        name="mixer_out_proj",
    )(x, gla_o, gg_src, att_o, gla_norm_w.reshape(1, GLA_DV), att_norm_w.reshape(1, ATT_WIDTH), w_out_bf, w_out_bf)


def _mlp_body(x_ref, fw_ref, wu_ref, wd_ref, nw_ref, o_ref, h_scr):
    f = pl.program_id(1)

    @pl.when(f == 0)
    def _():
        h_scr[...] = _rms(x_ref[...], fw_ref[...]).astype(bf16)
        o_ref[...] = jnp.zeros_like(o_ref)

    u = jnp.dot(h_scr[...], wu_ref[...], preferred_element_type=f32)
    a = jnp.square(jnp.maximum(u, 0.0)).astype(bf16)
    o_ref[...] += jnp.dot(a, wd_ref[...], preferred_element_type=f32)

    @pl.when(f == pl.num_programs(1) - 1)
    def _():
        o_ref[...] = _rms(x_ref[...] + o_ref[...], nw_ref[...])


def _mlp(x, ffn_norm_w, w_up_bf, w_down_bf, final_norm_w, tm, tf):
    n = x.shape[0]
    return pl.pallas_call(
        _mlp_body,
        grid=(n // tm, D_FF // tf),
        in_specs=[
            pl.BlockSpec((tm, D_MODEL), lambda i, f: (i, 0)),
            pl.BlockSpec((1, D_MODEL), lambda i, f: (0, 0)),
            pl.BlockSpec((D_MODEL, tf), lambda i, f: (0, f)),
            pl.BlockSpec((tf, D_MODEL), lambda i, f: (f, 0)),
            pl.BlockSpec((1, D_MODEL), lambda i, f: (0, 0)),
        ],
        out_specs=pl.BlockSpec((tm, D_MODEL), lambda i, f: (i, 0)),
        out_shape=jax.ShapeDtypeStruct((n, D_MODEL), f32),
        scratch_shapes=[pltpu.VMEM((tm, D_MODEL), bf16)],
        compiler_params=_cparams("parallel", "arbitrary"),
        name="mlp_final_norm",
    )(x, ffn_norm_w.reshape(1, D_MODEL), w_up_bf, w_down_bf, final_norm_w.reshape(1, D_MODEL))


def kernel(x_prompt, x_sample, cache_k_win, cache_v_win, state_gla, attn_norm_w, w_in, w_gk_up, b_gk, gla_norm_w,
           att_out_norm_w, w_out, ffn_norm_w, w_up, w_down, final_norm_w):
    depth = w_in.shape[0]
    assert depth == 1, "single trunk layer"
    B, S, _ = x_prompt.shape
    Bs, Ts, _ = x_sample.shape
    w_p = min(MAX_WINDOW, S)
    assert S % (ATT_BLOCK * max(DILATIONS)) == 0 and Ts <= GLA_CHUNK

    w_in0 = w_in[0]
    c_lr = GLA_COLS
    w_cat_bf = jnp.concatenate([w_in0[:, :c_lr], w_in0[:, c_lr + GLA_GATE_RANK:]], axis=1).astype(bf16)
    wlr_pad = jnp.pad(w_in0[:, c_lr:c_lr + GLA_GATE_RANK], ((0, 0), (0, LANES - GLA_GATE_RANK))).astype(bf16)
    wup_pad = jnp.pad(w_gk_up[0], ((0, LANES - GLA_GATE_RANK), (0, 0)))
    w_out_bf = w_out[0].astype(bf16)
    w_up_bf = w_up[0].astype(bf16)
    w_down_bf = w_down[0].astype(bf16)

    xp = x_prompt.reshape(B * S, D_MODEL)
    pg, loga, pa, k_win, v_win = _project(xp, attn_norm_w[0], w_cat_bf, wlr_pad, wup_pad, b_gk[0], tm=512,
                                          head_major=True, seq=S, window=w_p)
    s0 = jnp.zeros((B, N_GLA_HEADS, GLA_DK, GLA_DV), f32)
    gla_o, gla_state_p = _gla(pg, loga, s0, batch=B, t_len=S, tb=512, t_valid=GLA_CHUNK)
    att = _att_prompt(pa, batch=B, seq=S)
    x1 = _outproj(xp, gla_o, pg, 2, att, gla_norm_w[0], att_out_norm_w[0], w_out_bf, 256)
    y_prompt = _mlp(x1, ffn_norm_w[0], w_up_bf, w_down_bf, final_norm_w, 1024, 512).reshape(B, S, D_MODEL)

    P = SAMPLE_PAD
    xs_pad = jnp.pad(x_sample, ((0, 0), (0, P - Ts), (0, 0))).reshape(Bs * P, D_MODEL)
    pgs, logas, pas = _project(xs_pad, attn_norm_w[0], w_cat_bf, wlr_pad, wup_pad, b_gk[0], tm=Bs * P,
                               head_major=False)
    qkv_s = pas.reshape(Bs, P, 3, N_ATT_HEADS, ATT_HEAD_DIM)[:, :Ts]
    gla_o_s, gla_state_s = _gla(pgs, logas, state_gla[0], batch=Bs, t_len=P, tb=P, t_valid=Ts)
    att_s = _att_sample(qkv_s, cache_k_win[0], cache_v_win[0], batch=Bs, t_new=Ts).reshape(Bs * Ts, ATT_WIDTH)
    gla_o_s = gla_o_s.reshape(Bs, P, GLA_WIDTH)[:, :Ts].reshape(Bs * Ts, GLA_WIDTH)
    gg_s = pgs.reshape(Bs, P, GLA_COLS)[:, :Ts, 2 * GLA_KEY_WIDTH + GLA_WIDTH:].reshape(Bs * Ts, GLA_WIDTH)
    xs = x_sample.reshape(Bs * Ts, D_MODEL)
    x1s = _outproj(xs, gla_o_s, gg_s, 0, att_s, gla_norm_w[0], att_out_norm_w[0], w_out_bf, Bs * Ts)
    y_sample = _mlp(x1s, ffn_norm_w[0], w_up_bf, w_down_bf, final_norm_w, Bs * Ts, 512).reshape(Bs, Ts, D_MODEL)

    k_win_prompt = k_win.reshape(1, B, w_p, N_ATT_HEADS, ATT_HEAD_DIM)
    v_win_prompt = v_win.reshape(1, B, w_p, N_ATT_HEADS, ATT_HEAD_DIM)
    k_new_sample = qkv_s[:, :, 1][None]
    v_new_sample = qkv_s[:, :, 2][None]
    return (y_prompt, y_sample, k_win_prompt, v_win_prompt, gla_state_p[None], k_new_sample, v_new_sample,
            gla_state_s[None])
```

```python
import functools

import jax
import jax.numpy as jnp
from jax import lax
from jax.experimental import pallas as pl
from jax.experimental.pallas import tpu as pltpu

f32 = jnp.float32
bf16 = jnp.bfloat16

D_MODEL = 2048
N_GLA_HEADS = 4
GLA_DK = 128
GLA_DV = 256
GLA_KEY_WIDTH = N_GLA_HEADS * GLA_DK
GLA_WIDTH = N_GLA_HEADS * GLA_DV
GLA_GATE_RANK = 16
GLA_GATE_NORM = 16.0
GLA_CHUNK = 16
N_ATT_HEADS = 8
ATT_HEAD_DIM = 128
ATT_WIDTH = N_ATT_HEADS * ATT_HEAD_DIM
DILATIONS = (1, 4, 16)
SUB_WINDOW = 128
ATT_BLOCK = 128
MAX_WINDOW = 2048
D_FF = 4 * D_MODEL
RMS_EPS = 1e-6
GLA_COLS = 2 * GLA_KEY_WIDTH + 2 * GLA_WIDTH
ATT_COLS = 3 * ATT_WIDTH
LANES = 128
VMEM_LIMIT = 56 * 1024 * 1024
NEG_INF = float("-inf")
ALIBI_SLOPES = tuple(2.0 ** (-8.0 * (h + 1) / N_ATT_HEADS) for h in range(N_ATT_HEADS))
ATT_SCALE = ATT_HEAD_DIM ** -0.5

NT_DIMS = (((1,), (1,)), ((), ()))
TN_DIMS = (((0,), (0,)), ((), ()))


def _cparams(*sem):
    return pltpu.CompilerParams(dimension_semantics=sem, vmem_limit_bytes=VMEM_LIMIT)


def _rms(x, w):
    r = lax.rsqrt(jnp.mean(x * x, axis=-1, keepdims=True) + RMS_EPS)
    return (x * r) * w


PROJ_TN = 1024
N_GLA_TILES = GLA_COLS // PROJ_TN
N_PROJ_TILES = (GLA_COLS + ATT_COLS) // PROJ_TN


def _proj_body(x_ref, nw_ref, w_ref, wlr_ref, wup_ref, bgk_ref, pg_ref, loga_ref, pa_ref, *rest,
               head_major, win_tiles, tiles_per_seq):
    h_scr = rest[-1]
    i = pl.program_id(0)
    j = pl.program_id(1)

    @pl.when(j == 0)
    def _():
        h = _rms(x_ref[...], nw_ref[...]).astype(bf16)
        h_scr[...] = h
        glr = jnp.dot(h, wlr_ref[...], preferred_element_type=f32)
        pre = jnp.dot(glr, wup_ref[...], precision=lax.Precision.HIGHEST, preferred_element_type=f32) + bgk_ref[...]
        log_sig = jnp.minimum(pre, 0.0) - jnp.log1p(jnp.exp(-jnp.abs(pre)))
        loga_ref[...] = log_sig * (1.0 / GLA_GATE_NORM)

    res = jnp.dot(h_scr[...], w_ref[...], preferred_element_type=f32)
    tm = res.shape[0]

    @pl.when(j < N_GLA_TILES)
    def _():
        pg_ref[...] = res

    @pl.when(j >= N_GLA_TILES)
    def _():
        if head_major:
            for g in range(N_ATT_HEADS):
                pa_ref[g] = res[:, g * LANES:(g + 1) * LANES]
        else:
            pa_ref[...] = res

    if win_tiles:
        in_window = (i % tiles_per_seq) >= tiles_per_seq - win_tiles
        for which, ref in ((1, rest[0]), (2, rest[1])):
            @pl.when(in_window & (j == N_GLA_TILES + which))
            def _(ref=ref):
                for g in range(N_ATT_HEADS):
                    ref[pl.ds(g, tm, stride=N_ATT_HEADS), :] = res[:, g * LANES:(g + 1) * LANES]


def _project(x2d, norm_w, w_cat_bf, wlr_pad, wup_pad, b_gk, *, tm, head_major, seq=None, window=0):
    n, d = x2d.shape
    nb = n // tm
    nj = N_PROJ_TILES
    att_j = lambda j: jnp.maximum(j - N_GLA_TILES, 0)
    in_specs = [
        pl.BlockSpec((tm, d), lambda i, j: (i, 0)),
        pl.BlockSpec((1, d), lambda i, j: (0, 0)),
        pl.BlockSpec((d, PROJ_TN), lambda i, j: (0, j)),
        pl.BlockSpec((d, LANES), lambda i, j: (0, 0)),
        pl.BlockSpec((LANES, GLA_KEY_WIDTH), lambda i, j: (0, 0)),
        pl.BlockSpec((1, GLA_KEY_WIDTH), lambda i, j: (0, 0)),
    ]
    out_specs = [
        pl.BlockSpec((tm, PROJ_TN), lambda i, j: (i, jnp.minimum(j, N_GLA_TILES - 1))),
        pl.BlockSpec((tm, GLA_KEY_WIDTH), lambda i, j: (i, 0)),
    ]
    out_shape = [jax.ShapeDtypeStruct((n, GLA_COLS), f32), jax.ShapeDtypeStruct((n, GLA_KEY_WIDTH), f32)]
    if head_major:
        out_specs.append(pl.BlockSpec((N_ATT_HEADS, tm, LANES), lambda i, j: (att_j(j), i, 0)))
        out_shape.append(jax.ShapeDtypeStruct((3 * N_ATT_HEADS, n, ATT_HEAD_DIM), f32))
    else:
        out_specs.append(pl.BlockSpec((tm, PROJ_TN), lambda i, j: (i, att_j(j))))
        out_shape.append(jax.ShapeDtypeStruct((n, ATT_COLS), f32))
    win_tiles = tiles_per_seq = 0
    if window:
        assert seq % tm == 0 and window % tm == 0
        tiles_per_seq, win_tiles = seq // tm, window // tm
        H = N_ATT_HEADS

        def win_block(i, j):
            return ((i // tiles_per_seq) * win_tiles + jnp.maximum(i % tiles_per_seq - (tiles_per_seq - win_tiles), 0), 0)

        for _ in range(2):
            out_specs.append(pl.BlockSpec((tm * H, ATT_HEAD_DIM), win_block))
            out_shape.append(jax.ShapeDtypeStruct(((n // seq) * window * H, ATT_HEAD_DIM), f32))
    kern = functools.partial(_proj_body, head_major=head_major, win_tiles=win_tiles, tiles_per_seq=tiles_per_seq)
    return pl.pallas_call(
        kern,
        grid=(nb, nj),
        in_specs=in_specs,
        out_specs=out_specs,
        out_shape=out_shape,
        scratch_shapes=[pltpu.VMEM((tm, d), bf16)],
        compiler_params=_cparams("arbitrary", "arbitrary"),
        name="norm_in_proj",
    )(x2d, norm_w.reshape(1, d), w_cat_bf, wlr_pad, wup_pad, b_gk.reshape(1, GLA_KEY_WIDTH))


GLA_SUB = 128


def _gla_body(q_ref, k_ref, v_ref, g_ref, s0_ref, o_ref, sfin_ref, st_scr, bp_scr, *, tb, t_valid):
    C = GLA_CHUNK
    t_blk = pl.program_id(1)

    @pl.when(t_blk == 0)
    def _():
        for h in range(N_GLA_HEADS):
            st_scr[h] = s0_ref[h].T

    U = min(GLA_SUB, tb)
    levels = [w for w in (64, 32, 16, 8, 4, 2, 1) if w < U]
    row5 = lax.broadcasted_iota(jnp.int32, (U, GLA_KEY_WIDTH), 0)
    row1 = lax.broadcasted_iota(jnp.int32, (U, GLA_DK), 0)
    rowa = lax.broadcasted_iota(jnp.int32, (U, U), 0)
    cola = lax.broadcasted_iota(jnp.int32, (U, U), 1)
    ltri = (rowa >= cola).astype(bf16)
    off_diag = rowa != cola
    same_group = {w: ((rowa ^ cola) < 2 * w) & off_diag for w in levels}

    for sb in range(tb // U):
        r0 = sb * U
        g = g_ref[r0:r0 + U, :]
        if t_valid < C:
            g = jnp.where((row5 & (C - 1)) < t_valid, g, 0.0)
        g1 = g.astype(bf16)
        e1 = g - g1.astype(f32)
        g2 = e1.astype(bf16)
        g3 = (e1 - g2.astype(f32)).astype(bf16)
        bp = (jnp.dot(ltri, g1, preferred_element_type=f32) + jnp.dot(ltri, g2, preferred_element_type=f32)
              + jnp.dot(ltri, g3, preferred_element_type=f32))
        bp_scr[...] = bp
        b_last = bp[U - 1:U, :]
        q = q_ref[r0:r0 + U, :] * (GLA_DK ** -0.5)
        k = k_ref[r0:r0 + U, :]
        qe = q * jnp.exp(bp)
        ke = k * jnp.exp(b_last - bp)
        dec = jnp.exp(b_last)
        facs = []
        for w in levels:
            if w >= 4:
                bc = jnp.concatenate([jnp.broadcast_to(bp_scr[s + w - 1:s + w, :], (2 * w, GLA_KEY_WIDTH))
                                      for s in range(0, U, 2 * w)], axis=0)
            elif w == 2:
                pos = row5 & 3
                bc = jnp.where(pos == 0, pltpu.roll(bp, U - 1, 0),
                               jnp.where(pos == 1, bp, jnp.where(pos == 2, pltpu.roll(bp, 1, 0), pltpu.roll(bp, 2, 0))))
            else:
                bc = jnp.where((row5 & 1) == 1, pltpu.roll(bp, 1, 0), bp)
            facs.append(jnp.exp(jnp.where((row5 & w) != 0, bp - bc, bc - bp)))
        for h in range(N_GLA_HEADS):
            ks = slice(h * GLA_DK, (h + 1) * GLA_DK)
            vs = slice(h * GLA_DV, (h + 1) * GLA_DV)
            qh, kh = q[:, ks], k[:, ks]
            a = lax.dot_general(qh.astype(bf16), kh.astype(bf16), NT_DIMS, preferred_element_type=f32)
            a = jnp.where(rowa == cola, a, 0.0)
            for w, fac in zip(levels, facs):
                upper = (row1 & w) != 0
                fh = fac[:, ks]
                rq = jnp.where(upper, qh * fh, 0.0).astype(bf16)
                ck = jnp.where(upper, 0.0, kh * fh).astype(bf16)
                p = lax.dot_general(rq, ck, NT_DIMS, preferred_element_type=f32)
                a = jnp.where(same_group[w], p, a)
            vh = v_ref[r0:r0 + U, vs].astype(bf16)
            st = st_scr[h]
            o = jnp.dot(a.astype(bf16), vh, preferred_element_type=f32)
            o = o + lax.dot_general(qe[:, ks].astype(bf16), st.astype(bf16), NT_DIMS, preferred_element_type=f32)
            o_ref[r0:r0 + U, vs] = o
            kv_t = lax.dot_general(vh, ke[:, ks].astype(bf16), TN_DIMS, preferred_element_type=f32)
            st_scr[h] = dec[:, ks] * st + kv_t

    @pl.when(t_blk == pl.num_programs(1) - 1)
    def _():
        for h in range(N_GLA_HEADS):
            sfin_ref[h] = st_scr[h].T


def _gla(pg, loga, s0, *, batch, t_len, tb, t_valid):
    nt = t_len // tb
    kern = functools.partial(_gla_body, tb=tb, t_valid=t_valid)
    state_spec = pl.BlockSpec((None, N_GLA_HEADS, GLA_DK, GLA_DV), lambda b, t: (b, 0, 0, 0))
    return pl.pallas_call(
        kern,
        grid=(batch, nt),
        in_specs=[
            pl.BlockSpec((tb, GLA_KEY_WIDTH), lambda b, t: (b * nt + t, 0)),
            pl.BlockSpec((tb, GLA_KEY_WIDTH), lambda b, t: (b * nt + t, 1)),
            pl.BlockSpec((tb, GLA_WIDTH), lambda b, t: (b * nt + t, 1)),
            pl.BlockSpec((tb, GLA_KEY_WIDTH), lambda b, t: (b * nt + t, 0)),
            state_spec,
        ],
        out_specs=[pl.BlockSpec((tb, GLA_WIDTH), lambda b, t: (b * nt + t, 0)), state_spec],
        out_shape=[
            jax.ShapeDtypeStruct((batch * t_len, GLA_WIDTH), f32),
            jax.ShapeDtypeStruct((batch, N_GLA_HEADS, GLA_DK, GLA_DV), f32),
        ],
        scratch_shapes=[
            pltpu.VMEM((N_GLA_HEADS, GLA_DV, GLA_DK), f32),
            pltpu.VMEM((min(GLA_SUB, tb), GLA_KEY_WIDTH), f32),
        ],
        compiler_params=_cparams("parallel", "arbitrary"),
        name="gla",
    )(pg, pg, pg, loga, s0)


ATT_ROWS = ATT_BLOCK * max(DILATIONS)


def _att_prompt_body(q_ref, k_ref, v_ref, o_ref, qd_scr, kd1, vd1, kd4, vd4, kd16, vd16, od_scr, lse_scr):
    U = ATT_BLOCK
    h = pl.program_id(1)
    n = pl.program_id(2)
    kds, vds = (kd1, kd4, kd16), (vd1, vd4, vd16)

    @pl.when(n == 0)
    def _():
        for d, kd, vd in zip(DILATIONS, kds, vds):
            L = ATT_ROWS // d
            for r in range(d):
                kd[r * (U + L):r * (U + L) + U, :] = jnp.zeros((U, ATT_HEAD_DIM), bf16)
                vd[r * (U + L):r * (U + L) + U, :] = jnp.zeros((U, ATT_HEAD_DIM), bf16)

    qi = lax.broadcasted_iota(jnp.int32, (U, 2 * U), 0)
    ki = lax.broadcasted_iota(jnp.int32, (U, 2 * U), 1)
    dist = qi - ki + U
    in_window = (dist >= 0) & (dist <= SUB_WINDOW)
    slope = jnp.exp2(jnp.zeros((U, 2 * U), f32) - (h + 1).astype(f32) * (8.0 / N_ATT_HEADS))
    first_pen = jnp.where(n == 0, NEG_INF, 0.0)

    for di, d in enumerate(DILATIONS):
        L = ATT_ROWS // d
        nu = L // U
        kd, vd = kds[di], vds[di]
        bias = jnp.where(in_window, -(slope * float(d)) * dist.astype(f32), NEG_INF)
        for r in range(d):
            src = pl.ds(r, L, stride=d) if d > 1 else slice(None)
            base = r * (U + L)
            qd_scr[r * L:(r + 1) * L, :] = (q_ref[src, :] * ATT_SCALE).astype(bf16)
            kd[base + U:base + U + L, :] = k_ref[src, :].astype(bf16)
            vd[base + U:base + U + L, :] = v_ref[src, :].astype(bf16)

        def unit(idx, carry, d=d, L=L, nu=nu, kd=kd, vd=vd, bias=bias, di=di):
            r = idx // nu
            u = idx - r * nu
            qoff = pl.multiple_of(r * L + u * U, U)
            koff = pl.multiple_of(r * (U + L) + u * U, U)
            qq = qd_scr[pl.ds(qoff, U), :]
            kk = kd[pl.ds(koff, 2 * U), :]
            vv = vd[pl.ds(koff, 2 * U), :]
            pen = jnp.where(u == 0, first_pen, 0.0)
            s = lax.dot_general(qq, kk, NT_DIMS, preferred_element_type=f32) + bias
            s = s + jnp.where(ki < U, pen, 0.0)
            m = jnp.max(s, axis=-1, keepdims=True)
            p = jnp.exp(s - m)
            l = jnp.sum(p, axis=-1, keepdims=True)
            o = jnp.dot(p.astype(bf16), vv, preferred_element_type=f32) / l
            rows = pl.ds(u * (U * d) + r, U, stride=d) if d > 1 else pl.ds(pl.multiple_of(u * U, U), U)
            od_scr[di, rows, :] = o
            lse_scr[di, rows, :] = jnp.broadcast_to(m + jnp.log(l), (U, ATT_HEAD_DIM))
            return carry

        lax.fori_loop(0, d * nu, unit, 0, unroll=True)
        for r in range(d):
            base = r * (U + L)
            kd[base:base + U, :] = kd[base + L:base + L + U, :]
            vd[base:base + U, :] = vd[base + L:base + L + U, :]

    CH = 512
    for c in range(ATT_ROWS // CH):
        rs = slice(c * CH, (c + 1) * CH)
        ls = [lse_scr[di, rs, :] for di in range(len(DILATIONS))]
        mm = jnp.maximum(jnp.maximum(ls[0], ls[1]), ls[2])
        ws = [jnp.exp(x - mm) for x in ls]
        num = ws[0] * od_scr[0, rs, :] + ws[1] * od_scr[1, rs, :] + ws[2] * od_scr[2, rs, :]
        o_ref[rs, :] = num / (ws[0] + ws[1] + ws[2])


def _att_prompt(pa_hm, *, batch, seq):
    nb = seq // ATT_ROWS
    H, E, U = N_ATT_HEADS, ATT_HEAD_DIM, ATT_BLOCK
    blk = (None, ATT_ROWS, E)
    kv_scratch = []
    for d in DILATIONS:
        kv_scratch += [pltpu.VMEM((d * U + ATT_ROWS, E), bf16)] * 2
    return pl.pallas_call(
        _att_prompt_body,
        grid=(batch, H, nb),
        in_specs=[
            pl.BlockSpec(blk, lambda b, h, n: (h, b * nb + n, 0)),
            pl.BlockSpec(blk, lambda b, h, n: (H + h, b * nb + n, 0)),
            pl.BlockSpec(blk, lambda b, h, n: (2 * H + h, b * nb + n, 0)),
        ],
        out_specs=pl.BlockSpec((ATT_ROWS, E), lambda b, h, n: (b * nb + n, h)),
        out_shape=jax.ShapeDtypeStruct((batch * seq, H * E), f32),
        scratch_shapes=[pltpu.VMEM((ATT_ROWS, E), bf16)] + kv_scratch + [
            pltpu.VMEM((len(DILATIONS), ATT_ROWS, E), f32),
            pltpu.VMEM((len(DILATIONS), ATT_ROWS, E), f32),
        ],
        compiler_params=_cparams("parallel", "parallel", "arbitrary"),
        name="att_prompt",
    )(pa_hm, pa_hm, pa_hm)


SAMPLE_PAD = 16


def _att_sample_body(qkv_ref, k1_ref, v1_ref, k4_ref, v4_ref, k16_ref, v16_ref, o_ref, *, t_new):
    J = SUB_WINDOW
    hidx = lax.broadcasted_iota(jnp.int32, (N_ATT_HEADS, 1), 0)
    slope = jnp.exp2((hidx + 1).astype(f32) * (-8.0 / N_ATT_HEADS))
    slot = lax.broadcasted_iota(jnp.int32, (J, N_ATT_HEADS, 1), 0)
    q = [qkv_ref[i, 0] for i in range(t_new)]
    kn = [qkv_ref[i, 1] for i in range(t_new)]
    vn = [qkv_ref[i, 2] for i in range(t_new)]
    for i in range(t_new):
        s_new = [jnp.sum(q[i] * kn[n], axis=-1, keepdims=True) * ATT_SCALE for n in range(i + 1)]
        outs, lses = [], []
        for dil in DILATIONS:
            if dil == 1:
                kc, vc = k1_ref[...], v1_ref[...]
                dist = (J + i - slot).astype(f32)
                news = [(s_new[n] - slope * float(i - n), vn[n]) for n in range(i + 1)]
            else:
                kres_ref, vres_ref = (k4_ref, v4_ref) if dil == 4 else (k16_ref, v16_ref)
                kc, vc = kres_ref[:, i], vres_ref[:, i]
                dist = (J - slot).astype(f32)
                news = [(s_new[i], vn[i])]
            s = jnp.sum(kc * q[i][None], axis=-1, keepdims=True) * ATT_SCALE - (slope * float(dil))[None] * dist
            if dil == 1:
                s = jnp.where(slot >= i, s, NEG_INF)
            m = jnp.max(s, axis=0)
            for sn, _ in news:
                m = jnp.maximum(m, sn)
            p = jnp.exp(s - m[None])
            l = jnp.sum(p, axis=0)
            acc = jnp.sum(p * vc, axis=0)
            for sn, v in news:
                pn = jnp.exp(sn - m)
                l = l + pn
                acc = acc + pn * v
            outs.append(acc / l)
            lses.append(m + jnp.log(l))
        mm = jnp.maximum(jnp.maximum(lses[0], lses[1]), lses[2])
        ws = [jnp.exp(x - mm) for x in lses]
        tot = ws[0] + ws[1] + ws[2]
        o_ref[i] = (ws[0] * outs[0] + ws[1] * outs[1] + ws[2] * outs[2]) / tot


def _att_sample(qkv_s, cache_k, cache_v, *, batch, t_new):
    w_cache = cache_k.shape[1]
    J = SUB_WINDOW
    he = (N_ATT_HEADS, ATT_HEAD_DIM)
    assert w_cache == max(DILATIONS) * J and t_new <= 4
    specs, views = [], []
    for dil in DILATIONS:
        rows = w_cache // dil
        last = rows // J - 1
        for c in (cache_k, cache_v):
            if dil == 1:
                views.append(c)
                specs.append(pl.BlockSpec((None, J) + he, lambda b, last=last: (b, last, 0, 0)))
            else:
                views.append(c.reshape((batch, rows, dil) + he))
                specs.append(pl.BlockSpec((None, J, 4) + he, lambda b, last=last: (b, last, 0, 0, 0)))
    kern = functools.partial(_att_sample_body, t_new=t_new)
    return pl.pallas_call(
        kern,
        grid=(batch,),
        in_specs=[pl.BlockSpec((None, t_new, 3) + he, lambda b: (b, 0, 0, 0, 0))] + specs,
        out_specs=pl.BlockSpec((None, t_new) + he, lambda b: (b, 0, 0, 0)),
        out_shape=jax.ShapeDtypeStruct((batch, t_new) + he, f32),
        compiler_params=_cparams("parallel"),
        name="att_sample",
    )(qkv_s, *views)


def _outproj_body(x_ref, go_ref, gg_ref, att_ref, gnw_ref, anw_ref, wg_ref, wa_ref, out_ref):
    parts = []
    for h in range(N_GLA_HEADS):
        vs = slice(h * GLA_DV, (h + 1) * GLA_DV)
        y = _rms(go_ref[:, vs], gnw_ref[...])
        gate = gg_ref[:, vs]
        parts.append((y * (gate * jax.nn.sigmoid(gate))).astype(bf16))
    gla_part = jnp.concatenate(parts, axis=-1)
    att_part = _rms(att_ref[...], anw_ref[...]).astype(bf16)
    mix = jnp.dot(gla_part, wg_ref[...], preferred_element_type=f32)
    mix = mix + jnp.dot(att_part, wa_ref[...], preferred_element_type=f32)
    out_ref[...] = x_ref[...] + mix


def _outproj(x, gla_o, gg_src, gg_col, att_o, gla_norm_w, att_norm_w, w_out_bf, tm):
    n = x.shape[0]
    row = lambda i: (i, 0)
    const = lambda i: (0, 0)
    in_specs = [
        pl.BlockSpec((tm, D_MODEL), row),
        pl.BlockSpec((tm, GLA_WIDTH), row),
        pl.BlockSpec((tm, GLA_WIDTH), lambda i: (i, gg_col)),
        pl.BlockSpec((tm, ATT_WIDTH), row),
        pl.BlockSpec((1, GLA_DV), const),
        pl.BlockSpec((1, ATT_WIDTH), const),
        pl.BlockSpec((GLA_WIDTH, D_MODEL), lambda i: (0, 0)),
        pl.BlockSpec((ATT_WIDTH, D_MODEL), lambda i: (1, 0)),
    ]
    return pl.pallas_call(
        _outproj_body,
        grid=(n // tm,),
        in_specs=in_specs,
        out_specs=pl.BlockSpec((tm, D_MODEL), row),
        out_shape=jax.ShapeDtypeStruct((n, D_MODEL), f32),
        compiler_params=_cparams("parallel"),
        name="mixer_out_proj",
    )(x, gla_o, gg_src, att_o, gla_norm_w.reshape(1, GLA_DV), att_norm_w.reshape(1, ATT_WIDTH), w_out_bf, w_out_bf)


def _mlp_body(x_ref, fw_ref, wu_ref, wd_ref, nw_ref, o_ref, h_scr):
    f = pl.program_id(1)

    @pl.when(f == 0)
    def _():
        h_scr[...] = _rms(x_ref[...], fw_ref[...]).astype(bf16)
        o_ref[...] = jnp.zeros_like(o_ref)

    u = jnp.dot(h_scr[...], wu_ref[...], preferred_element_type=f32)
    a = jnp.square(jnp.maximum(u, 0.0)).astype(bf16)
    o_ref[...] += jnp.dot(a, wd_ref[...], preferred_element_type=f32)

    @pl.when(f == pl.num_programs(1) - 1)
    def _():
        o_ref[...] = _rms(x_ref[...] + o_ref[...], nw_ref[...])


def _mlp(x, ffn_norm_w, w_up_bf, w_down_bf, final_norm_w, tm, tf):
    n = x.shape[0]
    return pl.pallas_call(
        _mlp_body,
        grid=(n // tm, D_FF // tf),
        in_specs=[
            pl.BlockSpec((tm, D_MODEL), lambda i, f: (i, 0)),
            pl.BlockSpec((1, D_MODEL), lambda i, f: (0, 0)),
            pl.BlockSpec((D_MODEL, tf), lambda i, f: (0, f)),
            pl.BlockSpec((tf, D_MODEL), lambda i, f: (f, 0)),
            pl.BlockSpec((1, D_MODEL), lambda i, f: (0, 0)),
        ],
        out_specs=pl.BlockSpec((tm, D_MODEL), lambda i, f: (i, 0)),
        out_shape=jax.ShapeDtypeStruct((n, D_MODEL), f32),
        scratch_shapes=[pltpu.VMEM((tm, D_MODEL), bf16)],
        compiler_params=_cparams("parallel", "arbitrary"),
        name="mlp_final_norm",
    )(x, ffn_norm_w.reshape(1, D_MODEL), w_up_bf, w_down_bf, final_norm_w.reshape(1, D_MODEL))


def kernel(x_prompt, x_sample, cache_k_win, cache_v_win, state_gla, attn_norm_w, w_in, w_gk_up, b_gk, gla_norm_w,
           att_out_norm_w, w_out, ffn_norm_w, w_up, w_down, final_norm_w):
    depth = w_in.shape[0]
    assert depth == 1, "single trunk layer"
    B, S, _ = x_prompt.shape
    Bs, Ts, _ = x_sample.shape
    w_p = min(MAX_WINDOW, S)
    assert S % (ATT_BLOCK * max(DILATIONS)) == 0 and Ts <= GLA_CHUNK

    w_in0 = w_in[0]
    c_lr = GLA_COLS
    w_cat_bf = jnp.concatenate([w_in0[:, :c_lr], w_in0[:, c_lr + GLA_GATE_RANK:]], axis=1).astype(bf16)
    wlr_pad = jnp.pad(w_in0[:, c_lr:c_lr + GLA_GATE_RANK], ((0, 0), (0, LANES - GLA_GATE_RANK))).astype(bf16)
    wup_pad = jnp.pad(w_gk_up[0], ((0, LANES - GLA_GATE_RANK), (0, 0)))
    w_out_bf = w_out[0].astype(bf16)
    w_up_bf = w_up[0].astype(bf16)
    w_down_bf = w_down[0].astype(bf16)

    xp = x_prompt.reshape(B * S, D_MODEL)
    pg, loga, pa, k_win, v_win = _project(xp, attn_norm_w[0], w_cat_bf, wlr_pad, wup_pad, b_gk[0], tm=512,
                                          head_major=True, seq=S, window=w_p)
    s0 = jnp.zeros((B, N_GLA_HEADS, GLA_DK, GLA_DV), f32)
    gla_o, gla_state_p = _gla(pg, loga, s0, batch=B, t_len=S, tb=512, t_valid=GLA_CHUNK)
    att = _att_prompt(pa, batch=B, seq=S)
    x1 = _outproj(xp, gla_o, pg, 2, att, gla_norm_w[0], att_out_norm_w[0], w_out_bf, 256)
    y_prompt = _mlp(x1, ffn_norm_w[0], w_up_bf, w_down_bf, final_norm_w, 1024, 512).reshape(B, S, D_MODEL)

    P = SAMPLE_PAD
    xs_pad = jnp.pad(x_sample, ((0, 0), (0, P - Ts), (0, 0))).reshape(Bs * P, D_MODEL)
    pgs, logas, pas = _project(xs_pad, attn_norm_w[0], w_cat_bf, wlr_pad, wup_pad, b_gk[0], tm=Bs * P,
                               head_major=False)
    qkv_s = pas.reshape(Bs, P, 3, N_ATT_HEADS, ATT_HEAD_DIM)[:, :Ts]
    gla_o_s, gla_state_s = _gla(pgs, logas, state_gla[0], batch=Bs, t_len=P, tb=P, t_valid=Ts)
    att_s = _att_sample(qkv_s, cache_k_win[0], cache_v_win[0], batch=Bs, t_new=Ts).reshape(Bs * Ts, ATT_WIDTH)
    gla_o_s = gla_o_s.reshape(Bs, P, GLA_WIDTH)[:, :Ts].reshape(Bs * Ts, GLA_WIDTH)
    gg_s = pgs.reshape(Bs, P, GLA_COLS)[:, :Ts, 2 * GLA_KEY_WIDTH + GLA_WIDTH:].reshape(Bs * Ts, GLA_WIDTH)
    xs = x_sample.reshape(Bs * Ts, D_MODEL)
    x1s = _outproj(xs, gla_o_s, gg_s, 0, att_s, gla_norm_w[0], att_out_norm_w[0], w_out_bf, Bs * Ts)
    y_sample = _mlp(x1s, ffn_norm_w[0], w_up_bf, w_down_bf, final_norm_w, Bs * Ts, 512).reshape(Bs, Ts, D_MODEL)

    k_win_prompt = k_win.reshape(1, B, w_p, N_ATT_HEADS, ATT_HEAD_DIM)
    v_win_prompt = v_win.reshape(1, B, w_p, N_ATT_HEADS, ATT_HEAD_DIM)
    k_new_sample = qkv_s[:, :, 1][None]
    v_new_sample = qkv_s[:, :, 2][None]
    return (y_prompt, y_sample, k_win_prompt, v_win_prompt, gla_state_p[None], k_new_sample, v_new_sample,
            gla_state_s[None])
```

```python
import functools

import jax
import jax.numpy as jnp
from jax import lax
from jax.experimental import pallas as pl
from jax.experimental.pallas import tpu as pltpu

f32 = jnp.float32
bf16 = jnp.bfloat16

D_MODEL = 2048
N_GLA_HEADS = 4
GLA_DK = 128
GLA_DV = 256
GLA_KEY_WIDTH = N_GLA_HEADS * GLA_DK
GLA_WIDTH = N_GLA_HEADS * GLA_DV
GLA_GATE_RANK = 16
GLA_GATE_NORM = 16.0
GLA_CHUNK = 16
N_ATT_HEADS = 8
ATT_HEAD_DIM = 128
ATT_WIDTH = N_ATT_HEADS * ATT_HEAD_DIM
DILATIONS = (1, 4, 16)
SUB_WINDOW = 128
ATT_BLOCK = 128
MAX_WINDOW = 2048
D_FF = 4 * D_MODEL
RMS_EPS = 1e-6
GLA_COLS = 2 * GLA_KEY_WIDTH + 2 * GLA_WIDTH
ATT_COLS = 3 * ATT_WIDTH
LANES = 128
VMEM_LIMIT = 56 * 1024 * 1024
NEG_INF = float("-inf")
ALIBI_SLOPES = tuple(2.0 ** (-8.0 * (h + 1) / N_ATT_HEADS) for h in range(N_ATT_HEADS))
ATT_SCALE = ATT_HEAD_DIM ** -0.5

NT_DIMS = (((1,), (1,)), ((), ()))
TN_DIMS = (((0,), (0,)), ((), ()))


def _cparams(*sem):
    return pltpu.CompilerParams(dimension_semantics=sem, vmem_limit_bytes=VMEM_LIMIT)


def _rms(x, w):
    r = lax.rsqrt(jnp.mean(x * x, axis=-1, keepdims=True) + RMS_EPS)
    return (x * r) * w


PROJ_TN = 512
N_GLA_TILES = GLA_COLS // PROJ_TN
N_ATT_TILES = ATT_COLS // PROJ_TN
TILES_PER_QKV = ATT_WIDTH // PROJ_TN
HEADS_PER_TILE = PROJ_TN // ATT_HEAD_DIM


def _proj_body(x_ref, nw_ref, wg_ref, wa_ref, wlr_ref, wup_ref, bgk_ref, pg_ref, loga_ref, pa_ref, h_scr, *,
               head_major):
    j = pl.program_id(1)

    @pl.when(j == 0)
    def _():
        h = _rms(x_ref[...], nw_ref[...]).astype(bf16)
        h_scr[...] = h
        glr = jnp.dot(h, wlr_ref[...].astype(bf16), preferred_element_type=f32)
        pre = jnp.dot(glr, wup_ref[...], precision=lax.Precision.HIGHEST, preferred_element_type=f32) + bgk_ref[...]
        log_sig = jnp.minimum(pre, 0.0) - jnp.log1p(jnp.exp(-jnp.abs(pre)))
        loga_ref[...] = log_sig * (1.0 / GLA_GATE_NORM)

    @pl.when(j < N_GLA_TILES)
    def _():
        pg_ref[...] = jnp.dot(h_scr[...], wg_ref[...], preferred_element_type=f32)

    @pl.when(j >= N_GLA_TILES)
    def _():
        res = jnp.dot(h_scr[...], wa_ref[...], preferred_element_type=f32)
        if head_major:
            for g in range(HEADS_PER_TILE):
                pa_ref[g] = res[:, g * LANES:(g + 1) * LANES]
        else:
            pa_ref[...] = res


def _project(x2d, norm_w, w_gla_bf, w_att_bf, w_in0, wup_pad, b_gk, *, tm, head_major):
    n, d = x2d.shape
    nb = n // tm
    nj = N_GLA_TILES + N_ATT_TILES
    gla_j = lambda j: jnp.minimum(j, N_GLA_TILES - 1)
    att_j = lambda j: jnp.maximum(j - N_GLA_TILES, 0)
    assert GLA_COLS % LANES == 0
    in_specs = [
        pl.BlockSpec((tm, d), lambda i, j: (i, 0)),
        pl.BlockSpec((1, d), lambda i, j: (0, 0)),
        pl.BlockSpec((d, PROJ_TN), lambda i, j: (0, gla_j(j))),
        pl.BlockSpec((d, PROJ_TN), lambda i, j: (0, att_j(j))),
        pl.BlockSpec((d, LANES), lambda i, j: (0, GLA_COLS // LANES)),
        pl.BlockSpec((LANES, GLA_KEY_WIDTH), lambda i, j: (0, 0)),
        pl.BlockSpec((1, GLA_KEY_WIDTH), lambda i, j: (0, 0)),
    ]
    out_specs = [
        pl.BlockSpec((tm, PROJ_TN), lambda i, j: (i, gla_j(j))),
        pl.BlockSpec((tm, GLA_KEY_WIDTH), lambda i, j: (i, 0)),
    ]
    out_shape = [jax.ShapeDtypeStruct((n, GLA_COLS), f32), jax.ShapeDtypeStruct((n, GLA_KEY_WIDTH), f32)]
    if head_major:
        out_specs.append(pl.BlockSpec((HEADS_PER_TILE, tm, LANES), lambda i, j: (att_j(j), i, 0)))
        out_shape.append(jax.ShapeDtypeStruct((3 * N_ATT_HEADS, n, ATT_HEAD_DIM), f32))
    else:
        out_specs.append(pl.BlockSpec((tm, PROJ_TN), lambda i, j: (i, att_j(j))))
        out_shape.append(jax.ShapeDtypeStruct((n, ATT_COLS), f32))
    return pl.pallas_call(
        functools.partial(_proj_body, head_major=head_major),
        grid=(nb, nj),
        in_specs=in_specs,
        out_specs=out_specs,
        out_shape=out_shape,
        scratch_shapes=[pltpu.VMEM((tm, d), bf16)],
        compiler_params=_cparams("parallel", "arbitrary"),
        name="norm_in_proj",
    )(x2d, norm_w.reshape(1, d), w_gla_bf, w_att_bf, w_in0, wup_pad, b_gk.reshape(1, GLA_KEY_WIDTH))


def _window_body(kh_ref, vh_ref, ko_ref, vo_ref):
    tr = kh_ref.shape[1]
    for src, dst in ((kh_ref, ko_ref), (vh_ref, vo_ref)):
        for hd in range(N_ATT_HEADS):
            dst[pl.ds(hd, tr, stride=N_ATT_HEADS), :] = src[hd]


def _window_kv(pa_hm, *, batch, seq, window, tr):
    H, E = N_ATT_HEADS, ATT_HEAD_DIM
    assert seq % tr == 0 and window % tr == 0
    per_seq, first = seq // tr, (seq - window) // tr
    nw = window // tr
    src = lambda which: pl.BlockSpec((H, tr, E), lambda b, t: (which, b * per_seq + first + t, 0))
    dst = pl.BlockSpec((tr * H, E), lambda b, t: (b * nw + t, 0))
    shape = jax.ShapeDtypeStruct((batch * window * H, E), f32)
    return pl.pallas_call(
        _window_body,
        grid=(batch, nw),
        in_specs=[src(1), src(2)],
        out_specs=[dst, dst],
        out_shape=[shape, shape],
        compiler_params=_cparams("parallel", "parallel"),
        name="kv_window",
    )(pa_hm, pa_hm)


GLA_SUB = 128


def _gla_body(q_ref, k_ref, v_ref, g_ref, s0_ref, o_ref, sfin_ref, st_scr, bp_scr, *, tb, t_valid):
    C = GLA_CHUNK
    t_blk = pl.program_id(1)

    @pl.when(t_blk == 0)
    def _():
        for h in range(N_GLA_HEADS):
            st_scr[h] = s0_ref[h].T

    U = min(GLA_SUB, tb)
    levels = [w for w in (64, 32, 16, 8, 4, 2, 1) if w < U]
    row5 = lax.broadcasted_iota(jnp.int32, (U, GLA_KEY_WIDTH), 0)
    row1 = lax.broadcasted_iota(jnp.int32, (U, GLA_DK), 0)
    rowa = lax.broadcasted_iota(jnp.int32, (U, U), 0)
    cola = lax.broadcasted_iota(jnp.int32, (U, U), 1)
    ltri = (rowa >= cola).astype(bf16)
    off_diag = rowa != cola
    same_group = {w: ((rowa ^ cola) < 2 * w) & off_diag for w in levels}

    for sb in range(tb // U):
        r0 = sb * U
        g = g_ref[r0:r0 + U, :]
        if t_valid < C:
            g = jnp.where((row5 & (C - 1)) < t_valid, g, 0.0)
        g1 = g.astype(bf16)
        e1 = g - g1.astype(f32)
        g2 = e1.astype(bf16)
        g3 = (e1 - g2.astype(f32)).astype(bf16)
        bp = (jnp.dot(ltri, g1, preferred_element_type=f32) + jnp.dot(ltri, g2, preferred_element_type=f32)
              + jnp.dot(ltri, g3, preferred_element_type=f32))
        bp_scr[...] = bp
        b_last = bp[U - 1:U, :]
        q = q_ref[r0:r0 + U, :] * (GLA_DK ** -0.5)
        k = k_ref[r0:r0 + U, :]
        qe = q * jnp.exp(bp)
        ke = k * jnp.exp(b_last - bp)
        dec = jnp.exp(b_last)
        facs = []
        for w in levels:
            if w >= 4:
                bc = jnp.concatenate([jnp.broadcast_to(bp_scr[s + w - 1:s + w, :], (2 * w, GLA_KEY_WIDTH))
                                      for s in range(0, U, 2 * w)], axis=0)
            elif w == 2:
                pos = row5 & 3
                bc = jnp.where(pos == 0, pltpu.roll(bp, U - 1, 0),
                               jnp.where(pos == 1, bp, jnp.where(pos == 2, pltpu.roll(bp, 1, 0), pltpu.roll(bp, 2, 0))))
            else:
                bc = jnp.where((row5 & 1) == 1, pltpu.roll(bp, 1, 0), bp)
            facs.append(jnp.exp(jnp.where((row5 & w) != 0, bp - bc, bc - bp)))
        for h in range(N_GLA_HEADS):
            ks = slice(h * GLA_DK, (h + 1) * GLA_DK)
            vs = slice(h * GLA_DV, (h + 1) * GLA_DV)
            qh, kh = q[:, ks], k[:, ks]
            a = lax.dot_general(qh.astype(bf16), kh.astype(bf16), NT_DIMS, preferred_element_type=f32)
            a = jnp.where(rowa == cola, a, 0.0)
            for w, fac in zip(levels, facs):
                upper = (row1 & w) != 0
                fh = fac[:, ks]
                rq = jnp.where(upper, qh * fh, 0.0).astype(bf16)
                ck = jnp.where(upper, 0.0, kh * fh).astype(bf16)
                p = lax.dot_general(rq, ck, NT_DIMS, preferred_element_type=f32)
                a = jnp.where(same_group[w], p, a)
            vh = v_ref[r0:r0 + U, vs].astype(bf16)
            st = st_scr[h]
            o = jnp.dot(a.astype(bf16), vh, preferred_element_type=f32)
            o = o + lax.dot_general(qe[:, ks].astype(bf16), st.astype(bf16), NT_DIMS, preferred_element_type=f32)
            o_ref[r0:r0 + U, vs] = o
            kv_t = lax.dot_general(vh, ke[:, ks].astype(bf16), TN_DIMS, preferred_element_type=f32)
            st_scr[h] = dec[:, ks] * st + kv_t

    @pl.when(t_blk == pl.num_programs(1) - 1)
    def _():
        for h in range(N_GLA_HEADS):
            sfin_ref[h] = st_scr[h].T


def _gla(pg, loga, s0, *, batch, t_len, tb, t_valid):
    nt = t_len // tb
    kern = functools.partial(_gla_body, tb=tb, t_valid=t_valid)
    state_spec = pl.BlockSpec((None, N_GLA_HEADS, GLA_DK, GLA_DV), lambda b, t: (b, 0, 0, 0))
    return pl.pallas_call(
        kern,
        grid=(batch, nt),
        in_specs=[
            pl.BlockSpec((tb, GLA_KEY_WIDTH), lambda b, t: (b * nt + t, 0)),
            pl.BlockSpec((tb, GLA_KEY_WIDTH), lambda b, t: (b * nt + t, 1)),
            pl.BlockSpec((tb, GLA_WIDTH), lambda b, t: (b * nt + t, 1)),
            pl.BlockSpec((tb, GLA_KEY_WIDTH), lambda b, t: (b * nt + t, 0)),
            state_spec,
        ],
        out_specs=[pl.BlockSpec((tb, GLA_WIDTH), lambda b, t: (b * nt + t, 0)), state_spec],
        out_shape=[
            jax.ShapeDtypeStruct((batch * t_len, GLA_WIDTH), f32),
            jax.ShapeDtypeStruct((batch, N_GLA_HEADS, GLA_DK, GLA_DV), f32),
        ],
        scratch_shapes=[
            pltpu.VMEM((N_GLA_HEADS, GLA_DV, GLA_DK), f32),
            pltpu.VMEM((min(GLA_SUB, tb), GLA_KEY_WIDTH), f32),
        ],
        compiler_params=_cparams("parallel", "arbitrary"),
        name="gla",
    )(pg, pg, pg, loga, s0)


ATT_ROWS = ATT_BLOCK * max(DILATIONS)


def _att_prompt_body(q_ref, k_ref, v_ref, o_ref, qd_scr, kd1, vd1, kd4, vd4, kd16, vd16, od_scr, lse_scr):
    U = ATT_BLOCK
    h = pl.program_id(1)
    n = pl.program_id(2)
    kds, vds = (kd1, kd4, kd16), (vd1, vd4, vd16)

    @pl.when(n == 0)
    def _():
        for d, kd, vd in zip(DILATIONS, kds, vds):
            L = ATT_ROWS // d
            for r in range(d):
                kd[r * (U + L):r * (U + L) + U, :] = jnp.zeros((U, ATT_HEAD_DIM), bf16)
                vd[r * (U + L):r * (U + L) + U, :] = jnp.zeros((U, ATT_HEAD_DIM), bf16)

    qi = lax.broadcasted_iota(jnp.int32, (U, 2 * U), 0)
    ki = lax.broadcasted_iota(jnp.int32, (U, 2 * U), 1)
    dist = qi - ki + U
    in_window = (dist >= 0) & (dist <= SUB_WINDOW)
    slope = jnp.exp2(jnp.zeros((U, 2 * U), f32) - (h + 1).astype(f32) * (8.0 / N_ATT_HEADS))
    first_pen = jnp.where(n == 0, NEG_INF, 0.0)

    for di, d in enumerate(DILATIONS):
        L = ATT_ROWS // d
        nu = L // U
        kd, vd = kds[di], vds[di]
        bias = jnp.where(in_window, -(slope * float(d)) * dist.astype(f32), NEG_INF)
        for r in range(d):
            src = pl.ds(r, L, stride=d) if d > 1 else slice(None)
            base = r * (U + L)
            qd_scr[r * L:(r + 1) * L, :] = (q_ref[src, :] * ATT_SCALE).astype(bf16)
            kd[base + U:base + U + L, :] = k_ref[src, :].astype(bf16)
            vd[base + U:base + U + L, :] = v_ref[src, :].astype(bf16)

        def unit(idx, carry, d=d, L=L, nu=nu, kd=kd, vd=vd, bias=bias, di=di):
            r = idx // nu
            u = idx - r * nu
            qoff = pl.multiple_of(r * L + u * U, U)
            koff = pl.multiple_of(r * (U + L) + u * U, U)
            qq = qd_scr[pl.ds(qoff, U), :]
            kk = kd[pl.ds(koff, 2 * U), :]
            vv = vd[pl.ds(koff, 2 * U), :]
            pen = jnp.where(u == 0, first_pen, 0.0)
            s = lax.dot_general(qq, kk, NT_DIMS, preferred_element_type=f32) + bias
            s = s + jnp.where(ki < U, pen, 0.0)
            m = jnp.max(s, axis=-1, keepdims=True)
            p = jnp.exp(s - m)
            l = jnp.sum(p, axis=-1, keepdims=True)
            o = jnp.dot(p.astype(bf16), vv, preferred_element_type=f32) / l
            rows = pl.ds(u * (U * d) + r, U, stride=d) if d > 1 else pl.ds(pl.multiple_of(u * U, U), U)
            od_scr[di, rows, :] = o
            lse_scr[di, rows, :] = jnp.broadcast_to(m + jnp.log(l), (U, ATT_HEAD_DIM))
            return carry

        lax.fori_loop(0, d * nu, unit, 0, unroll=True)
        for r in range(d):
            base = r * (U + L)
            kd[base:base + U, :] = kd[base + L:base + L + U, :]
            vd[base:base + U, :] = vd[base + L:base + L + U, :]

    CH = 512
    for c in range(ATT_ROWS // CH):
        rs = slice(c * CH, (c + 1) * CH)
        ls = [lse_scr[di, rs, :] for di in range(len(DILATIONS))]
        mm = jnp.maximum(jnp.maximum(ls[0], ls[1]), ls[2])
        ws = [jnp.exp(x - mm) for x in ls]
        num = ws[0] * od_scr[0, rs, :] + ws[1] * od_scr[1, rs, :] + ws[2] * od_scr[2, rs, :]
        o_ref[rs, :] = num / (ws[0] + ws[1] + ws[2])


def _att_prompt(pa_hm, *, batch, seq):
    nb = seq // ATT_ROWS
    H, E, U = N_ATT_HEADS, ATT_HEAD_DIM, ATT_BLOCK
    blk = (None, ATT_ROWS, E)
    kv_scratch = []
    for d in DILATIONS:
        kv_scratch += [pltpu.VMEM((d * U + ATT_ROWS, E), bf16)] * 2
    return pl.pallas_call(
        _att_prompt_body,
        grid=(batch, H, nb),
        in_specs=[
            pl.BlockSpec(blk, lambda b, h, n: (h, b * nb + n, 0)),
            pl.BlockSpec(blk, lambda b, h, n: (H + h, b * nb + n, 0)),
            pl.BlockSpec(blk, lambda b, h, n: (2 * H + h, b * nb + n, 0)),
        ],
        out_specs=pl.BlockSpec((ATT_ROWS, E), lambda b, h, n: (b * nb + n, h)),
        out_shape=jax.ShapeDtypeStruct((batch * seq, H * E), f32),
        scratch_shapes=[pltpu.VMEM((ATT_ROWS, E), bf16)] + kv_scratch + [
            pltpu.VMEM((len(DILATIONS), ATT_ROWS, E), f32),
            pltpu.VMEM((len(DILATIONS), ATT_ROWS, E), f32),
        ],
        compiler_params=_cparams("parallel", "parallel", "arbitrary"),
        name="att_prompt",
    )(pa_hm, pa_hm, pa_hm)


SAMPLE_PAD = 16


def _att_sample_body(qkv_ref, k1_ref, v1_ref, k4_ref, v4_ref, k16_ref, v16_ref, o_ref, *, t_new):
    J = SUB_WINDOW
    hidx = lax.broadcasted_iota(jnp.int32, (N_ATT_HEADS, 1), 0)
    slope = jnp.exp2((hidx + 1).astype(f32) * (-8.0 / N_ATT_HEADS))
    slot = lax.broadcasted_iota(jnp.int32, (J, N_ATT_HEADS, 1), 0)
    q = [qkv_ref[i, 0] for i in range(t_new)]
    kn = [qkv_ref[i, 1] for i in range(t_new)]
    vn = [qkv_ref[i, 2] for i in range(t_new)]
    for i in range(t_new):
        s_new = [jnp.sum(q[i] * kn[n], axis=-1, keepdims=True) * ATT_SCALE for n in range(i + 1)]
        outs, lses = [], []
        for dil in DILATIONS:
            if dil == 1:
                kc, vc = k1_ref[...], v1_ref[...]
                dist = (J + i - slot).astype(f32)
                news = [(s_new[n] - slope * float(i - n), vn[n]) for n in range(i + 1)]
            else:
                kres_ref, vres_ref = (k4_ref, v4_ref) if dil == 4 else (k16_ref, v16_ref)
                kc, vc = kres_ref[:, i], vres_ref[:, i]
                dist = (J - slot).astype(f32)
                news = [(s_new[i], vn[i])]
            s = jnp.sum(kc * q[i][None], axis=-1, keepdims=True) * ATT_SCALE - (slope * float(dil))[None] * dist
            if dil == 1:
                s = jnp.where(slot >= i, s, NEG_INF)
            m = jnp.max(s, axis=0)
            for sn, _ in news:
                m = jnp.maximum(m, sn)
            p = jnp.exp(s - m[None])
            l = jnp.sum(p, axis=0)
            acc = jnp.sum(p * vc, axis=0)
            for sn, v in news:
                pn = jnp.exp(sn - m)
                l = l + pn
                acc = acc + pn * v
            outs.append(acc / l)
            lses.append(m + jnp.log(l))
        mm = jnp.maximum(jnp.maximum(lses[0], lses[1]), lses[2])
        ws = [jnp.exp(x - mm) for x in lses]
        tot = ws[0] + ws[1] + ws[2]
        o_ref[i] = (ws[0] * outs[0] + ws[1] * outs[1] + ws[2] * outs[2]) / tot


def _att_sample(qkv_s, cache_k, cache_v, *, batch, t_new):
    w_cache = cache_k.shape[1]
    J = SUB_WINDOW
    he = (N_ATT_HEADS, ATT_HEAD_DIM)
    assert w_cache == max(DILATIONS) * J and t_new <= 4
    specs, views = [], []
    for dil in DILATIONS:
        rows = w_cache // dil
        last = rows // J - 1
        for c in (cache_k, cache_v):
            if dil == 1:
                views.append(c)
                specs.append(pl.BlockSpec((None, J) + he, lambda b, last=last: (b, last, 0, 0)))
            else:
                views.append(c.reshape((batch, rows, dil) + he))
                specs.append(pl.BlockSpec((None, J, 4) + he, lambda b, last=last: (b, last, 0, 0, 0)))
    kern = functools.partial(_att_sample_body, t_new=t_new)
    return pl.pallas_call(
        kern,
        grid=(batch,),
        in_specs=[pl.BlockSpec((None, t_new, 3) + he, lambda b: (b, 0, 0, 0, 0))] + specs,
        out_specs=pl.BlockSpec((None, t_new) + he, lambda b: (b, 0, 0, 0)),
        out_shape=jax.ShapeDtypeStruct((batch, t_new) + he, f32),
        compiler_params=_cparams("parallel"),
        name="att_sample",
    )(qkv_s, *views)


def _outproj_body(x_ref, go_ref, gg_ref, att_ref, gnw_ref, anw_ref, wg_ref, wa_ref, out_ref):
    parts = []
    for h in range(N_GLA_HEADS):
        vs = slice(h * GLA_DV, (h + 1) * GLA_DV)
        y = _rms(go_ref[:, vs], gnw_ref[...])
        gate = gg_ref[:, vs]
        parts.append((y * (gate * jax.nn.sigmoid(gate))).astype(bf16))
    gla_part = jnp.concatenate(parts, axis=-1)
    att_part = _rms(att_ref[...], anw_ref[...]).astype(bf16)
    mix = jnp.dot(gla_part, wg_ref[...], preferred_element_type=f32)
    mix = mix + jnp.dot(att_part, wa_ref[...], preferred_element_type=f32)
    out_ref[...] = x_ref[...] + mix


def _outproj(x, gla_o, gg_src, gg_col, att_o, gla_norm_w, att_norm_w, w_out_bf, tm):
    n = x.shape[0]
    row = lambda i: (i, 0)
    const = lambda i: (0, 0)
    in_specs = [
        pl.BlockSpec((tm, D_MODEL), row),
        pl.BlockSpec((tm, GLA_WIDTH), row),
        pl.BlockSpec((tm, GLA_WIDTH), lambda i: (i, gg_col)),
        pl.BlockSpec((tm, ATT_WIDTH), row),
        pl.BlockSpec((1, GLA_DV), const),
        pl.BlockSpec((1, ATT_WIDTH), const),
        pl.BlockSpec((GLA_WIDTH, D_MODEL), lambda i: (0, 0)),
        pl.BlockSpec((ATT_WIDTH, D_MODEL), lambda i: (1, 0)),
    ]
    return pl.pallas_call(
        _outproj_body,
        grid=(n // tm,),
        in_specs=in_specs,
        out_specs=pl.BlockSpec((tm, D_MODEL), row),
        out_shape=jax.ShapeDtypeStruct((n, D_MODEL), f32),
        compiler_params=_cparams("parallel"),
        name="mixer_out_proj",
    )(x, gla_o, gg_src, att_o, gla_norm_w.reshape(1, GLA_DV), att_norm_w.reshape(1, ATT_WIDTH), w_out_bf, w_out_bf)


def _mlp_body(x_ref, fw_ref, wu_ref, wd_ref, nw_ref, o_ref, h_scr):
    f = pl.program_id(1)

    @pl.when(f == 0)
    def _():
        h_scr[...] = _rms(x_ref[...], fw_ref[...]).astype(bf16)
        o_ref[...] = jnp.zeros_like(o_ref)

    u = jnp.dot(h_scr[...], wu_ref[...], preferred_element_type=f32)
    a = jnp.square(jnp.maximum(u, 0.0)).astype(bf16)
    o_ref[...] += jnp.dot(a, wd_ref[...], preferred_element_type=f32)

    @pl.when(f == pl.num_programs(1) - 1)
    def _():
        o_ref[...] = _rms(x_ref[...] + o_ref[...], nw_ref[...])


def _mlp(x, ffn_norm_w, w_up_bf, w_down_bf, final_norm_w, tm, tf):
    n = x.shape[0]
    return pl.pallas_call(
        _mlp_body,
        grid=(n // tm, D_FF // tf),
        in_specs=[
            pl.BlockSpec((tm, D_MODEL), lambda i, f: (i, 0)),
            pl.BlockSpec((1, D_MODEL), lambda i, f: (0, 0)),
            pl.BlockSpec((D_MODEL, tf), lambda i, f: (0, f)),
            pl.BlockSpec((tf, D_MODEL), lambda i, f: (f, 0)),
            pl.BlockSpec((1, D_MODEL), lambda i, f: (0, 0)),
        ],
        out_specs=pl.BlockSpec((tm, D_MODEL), lambda i, f: (i, 0)),
        out_shape=jax.ShapeDtypeStruct((n, D_MODEL), f32),
        scratch_shapes=[pltpu.VMEM((tm, D_MODEL), bf16)],
        compiler_params=_cparams("parallel", "arbitrary"),
        name="mlp_final_norm",
    )(x, ffn_norm_w.reshape(1, D_MODEL), w_up_bf, w_down_bf, final_norm_w.reshape(1, D_MODEL))


def kernel(x_prompt, x_sample, cache_k_win, cache_v_win, state_gla, attn_norm_w, w_in, w_gk_up, b_gk, gla_norm_w,
           att_out_norm_w, w_out, ffn_norm_w, w_up, w_down, final_norm_w):
    depth = w_in.shape[0]
    assert depth == 1, "single trunk layer"
    B, S, _ = x_prompt.shape
    Bs, Ts, _ = x_sample.shape
    w_p = min(MAX_WINDOW, S)
    assert S % (ATT_BLOCK * max(DILATIONS)) == 0 and Ts <= GLA_CHUNK

    w_in0 = w_in[0]
    c_lr = GLA_COLS
    w_gla_bf = w_in0[:, :c_lr].astype(bf16)
    w_att_bf = w_in0[:, c_lr + GLA_GATE_RANK:].astype(bf16)
    wup_pad = jnp.pad(w_gk_up[0], ((0, LANES - GLA_GATE_RANK), (0, 0)))
    w_out_bf = w_out[0].astype(bf16)
    w_up_bf = w_up[0].astype(bf16)
    w_down_bf = w_down[0].astype(bf16)

    xp = x_prompt.reshape(B * S, D_MODEL)
    pg, loga, pa = _project(xp, attn_norm_w[0], w_gla_bf, w_att_bf, w_in0, wup_pad, b_gk[0], tm=1024, head_major=True)
    k_win, v_win = _window_kv(pa, batch=B, seq=S, window=w_p, tr=512)
    s0 = jnp.zeros((B, N_GLA_HEADS, GLA_DK, GLA_DV), f32)
    gla_o, gla_state_p = _gla(pg, loga, s0, batch=B, t_len=S, tb=512, t_valid=GLA_CHUNK)
    att = _att_prompt(pa, batch=B, seq=S)
    x1 = _outproj(xp, gla_o, pg, 2, att, gla_norm_w[0], att_out_norm_w[0], w_out_bf, 512)
    y_prompt = _mlp(x1, ffn_norm_w[0], w_up_bf, w_down_bf, final_norm_w, 1024, 512).reshape(B, S, D_MODEL)

    P = SAMPLE_PAD
    xs_pad = jnp.pad(x_sample, ((0, 0), (0, P - Ts), (0, 0))).reshape(Bs * P, D_MODEL)
    pgs, logas, pas = _project(xs_pad, attn_norm_w[0], w_gla_bf, w_att_bf, w_in0, wup_pad, b_gk[0], tm=Bs * P,
                               head_major=False)
    qkv_s = pas.reshape(Bs, P, 3, N_ATT_HEADS, ATT_HEAD_DIM)[:, :Ts]
    gla_o_s, gla_state_s = _gla(pgs, logas, state_gla[0], batch=Bs, t_len=P, tb=P, t_valid=Ts)
    att_s = _att_sample(qkv_s, cache_k_win[0], cache_v_win[0], batch=Bs, t_new=Ts).reshape(Bs * Ts, ATT_WIDTH)
    gla_o_s = gla_o_s.reshape(Bs, P, GLA_WIDTH)[:, :Ts].reshape(Bs * Ts, GLA_WIDTH)
    gg_s = pgs.reshape(Bs, P, GLA_COLS)[:, :Ts, 2 * GLA_KEY_WIDTH + GLA_WIDTH:].reshape(Bs * Ts, GLA_WIDTH)
    xs = x_sample.reshape(Bs * Ts, D_MODEL)
    x1s = _outproj(xs, gla_o_s, gg_s, 0, att_s, gla_norm_w[0], att_out_norm_w[0], w_out_bf, Bs * Ts)
    y_sample = _mlp(x1s, ffn_norm_w[0], w_up_bf, w_down_bf, final_norm_w, Bs * Ts, 512).reshape(Bs, Ts, D_MODEL)

    k_win_prompt = k_win.reshape(1, B, w_p, N_ATT_HEADS, ATT_HEAD_DIM)
    v_win_prompt = v_win.reshape(1, B, w_p, N_ATT_HEADS, ATT_HEAD_DIM)
    k_new_sample = qkv_s[:, :, 1][None]
    v_new_sample = qkv_s[:, :, 2][None]
    return (y_prompt, y_sample, k_win_prompt, v_win_prompt, gla_state_p[None], k_new_sample, v_new_sample,
            gla_state_s[None])
```

```python
import functools

import jax
import jax.numpy as jnp
from jax import lax
from jax.experimental import pallas as pl
from jax.experimental.pallas import tpu as pltpu

f32 = jnp.float32
bf16 = jnp.bfloat16

D_MODEL = 2048
N_GLA_HEADS = 4
GLA_DK = 128
GLA_DV = 256
GLA_KEY_WIDTH = N_GLA_HEADS * GLA_DK
GLA_WIDTH = N_GLA_HEADS * GLA_DV
GLA_GATE_RANK = 16
GLA_GATE_NORM = 16.0
GLA_CHUNK = 16
N_ATT_HEADS = 8
ATT_HEAD_DIM = 128
ATT_WIDTH = N_ATT_HEADS * ATT_HEAD_DIM
DILATIONS = (1, 4, 16)
SUB_WINDOW = 128
ATT_BLOCK = 128
MAX_WINDOW = 2048
D_FF = 4 * D_MODEL
RMS_EPS = 1e-6
GLA_COLS = 2 * GLA_KEY_WIDTH + 2 * GLA_WIDTH
ATT_COLS = 3 * ATT_WIDTH
LANES = 128
VMEM_LIMIT = 56 * 1024 * 1024
NEG_INF = float("-inf")
ALIBI_SLOPES = tuple(2.0 ** (-8.0 * (h + 1) / N_ATT_HEADS) for h in range(N_ATT_HEADS))
ATT_SCALE = ATT_HEAD_DIM ** -0.5

NT_DIMS = (((1,), (1,)), ((), ()))
TN_DIMS = (((0,), (0,)), ((), ()))


def _cparams(*sem):
    return pltpu.CompilerParams(dimension_semantics=sem, vmem_limit_bytes=VMEM_LIMIT)


def _rms(x, w):
    r = lax.rsqrt(jnp.mean(x * x, axis=-1, keepdims=True) + RMS_EPS)
    return (x * r) * w


def _split_w_in_body(w_ref, g_ref, a_ref):
    g_ref[...] = w_ref[:, :GLA_COLS].astype(bf16)
    a_ref[...] = w_ref[:, GLA_COLS + GLA_GATE_RANK:].astype(bf16)


def _split_w_in(w_in0, rows):
    d, cols = w_in0.shape
    assert cols == GLA_COLS + GLA_GATE_RANK + ATT_COLS and d % rows == 0
    return pl.pallas_call(
        _split_w_in_body,
        grid=(d // rows,),
        in_specs=[pl.BlockSpec((rows, cols), lambda i: (i, 0))],
        out_specs=[pl.BlockSpec((rows, GLA_COLS), lambda i: (i, 0)), pl.BlockSpec((rows, ATT_COLS), lambda i: (i, 0))],
        out_shape=[jax.ShapeDtypeStruct((d, GLA_COLS), bf16), jax.ShapeDtypeStruct((d, ATT_COLS), bf16)],
        compiler_params=_cparams("parallel"),
        name="split_w_in",
    )(w_in0)


PROJ_TN = 512
N_GLA_TILES = GLA_COLS // PROJ_TN
N_ATT_TILES = ATT_COLS // PROJ_TN
TILES_PER_QKV = ATT_WIDTH // PROJ_TN
HEADS_PER_TILE = PROJ_TN // ATT_HEAD_DIM


def _proj_body(x_ref, nw_ref, wg_ref, wa_ref, wlr_ref, wup_ref, bgk_ref, pg_ref, loga_ref, pa_ref, h_scr, *,
               head_major):
    j = pl.program_id(1)

    @pl.when(j == 0)
    def _():
        h = _rms(x_ref[...], nw_ref[...]).astype(bf16)
        h_scr[...] = h
        glr = jnp.dot(h, wlr_ref[...].astype(bf16), preferred_element_type=f32)
        R = GLA_GATE_RANK
        lane = lax.broadcasted_iota(jnp.int32, glr.shape, 1)
        g = jnp.where(lane < R, glr, 0.0)
        g_hi = g.astype(bf16).astype(f32)
        g3 = (g_hi + pltpu.roll(g_hi, R, 1) + pltpu.roll(g - g_hi, 2 * R, 1)).astype(bf16)
        w = wup_ref[...]
        w_hi = w.astype(bf16).astype(f32)
        w3 = (w_hi + pltpu.roll(w - w_hi, R, 0) + pltpu.roll(w_hi, 2 * R, 0)).astype(bf16)
        pre = jnp.dot(g3, w3, preferred_element_type=f32) + bgk_ref[...]
        log_sig = jnp.minimum(pre, 0.0) - jnp.log1p(jnp.exp(-jnp.abs(pre)))
        loga_ref[...] = log_sig * (1.0 / GLA_GATE_NORM)

    @pl.when(j < N_GLA_TILES)
    def _():
        pg_ref[...] = jnp.dot(h_scr[...], wg_ref[...], preferred_element_type=f32)

    @pl.when(j >= N_GLA_TILES)
    def _():
        res = jnp.dot(h_scr[...], wa_ref[...], preferred_element_type=f32)
        if head_major:
            for g in range(HEADS_PER_TILE):
                pa_ref[g] = res[:, g * LANES:(g + 1) * LANES]
        else:
            pa_ref[...] = res


def _project(x2d, norm_w, w_gla_bf, w_att_bf, w_in0, wup_pad, b_gk, *, tm, head_major):
    n, d = x2d.shape
    nb = n // tm
    nj = N_GLA_TILES + N_ATT_TILES
    gla_j = lambda j: jnp.minimum(j, N_GLA_TILES - 1)
    att_j = lambda j: jnp.maximum(j - N_GLA_TILES, 0)
    assert GLA_COLS % LANES == 0
    in_specs = [
        pl.BlockSpec((tm, d), lambda i, j: (i, 0)),
        pl.BlockSpec((1, d), lambda i, j: (0, 0)),
        pl.BlockSpec((d, PROJ_TN), lambda i, j: (0, gla_j(j))),
        pl.BlockSpec((d, PROJ_TN), lambda i, j: (0, att_j(j))),
        pl.BlockSpec((d, LANES), lambda i, j: (0, GLA_COLS // LANES)),
        pl.BlockSpec((LANES, GLA_KEY_WIDTH), lambda i, j: (0, 0)),
        pl.BlockSpec((1, GLA_KEY_WIDTH), lambda i, j: (0, 0)),
    ]
    out_specs = [
        pl.BlockSpec((tm, PROJ_TN), lambda i, j: (i, gla_j(j))),
        pl.BlockSpec((tm, GLA_KEY_WIDTH), lambda i, j: (i, 0)),
    ]
    out_shape = [jax.ShapeDtypeStruct((n, GLA_COLS), f32), jax.ShapeDtypeStruct((n, GLA_KEY_WIDTH), f32)]
    if head_major:
        out_specs.append(pl.BlockSpec((HEADS_PER_TILE, tm, LANES), lambda i, j: (att_j(j), i, 0)))
        out_shape.append(jax.ShapeDtypeStruct((3 * N_ATT_HEADS, n, ATT_HEAD_DIM), f32))
    else:
        out_specs.append(pl.BlockSpec((tm, PROJ_TN), lambda i, j: (i, att_j(j))))
        out_shape.append(jax.ShapeDtypeStruct((n, ATT_COLS), f32))
    return pl.pallas_call(
        functools.partial(_proj_body, head_major=head_major),
        grid=(nb, nj),
        in_specs=in_specs,
        out_specs=out_specs,
        out_shape=out_shape,
        scratch_shapes=[pltpu.VMEM((tm, d), bf16)],
        compiler_params=_cparams("parallel", "arbitrary"),
        name="norm_in_proj",
    )(x2d, norm_w.reshape(1, d), w_gla_bf, w_att_bf, w_in0, wup_pad, b_gk.reshape(1, GLA_KEY_WIDTH))


def _window_body(kh_ref, vh_ref, ko_ref, vo_ref):
    tr = kh_ref.shape[1]
    for src, dst in ((kh_ref, ko_ref), (vh_ref, vo_ref)):
        for hd in range(N_ATT_HEADS):
            dst[pl.ds(hd, tr, stride=N_ATT_HEADS), :] = src[hd]


def _window_kv(pa_hm, *, batch, seq, window, tr):
    H, E = N_ATT_HEADS, ATT_HEAD_DIM
    assert seq % tr == 0 and window % tr == 0
    per_seq, first = seq // tr, (seq - window) // tr
    nw = window // tr
    src = lambda which: pl.BlockSpec((H, tr, E), lambda b, t: (which, b * per_seq + first + t, 0))
    dst = pl.BlockSpec((tr * H, E), lambda b, t: (b * nw + t, 0))
    shape = jax.ShapeDtypeStruct((batch * window * H, E), f32)
    return pl.pallas_call(
        _window_body,
        grid=(batch, nw),
        in_specs=[src(1), src(2)],
        out_specs=[dst, dst],
        out_shape=[shape, shape],
        compiler_params=_cparams("parallel", "parallel"),
        name="kv_window",
    )(pa_hm, pa_hm)


GLA_SUB = 128


def _gla_body(q_ref, k_ref, v_ref, g_ref, s0_ref, o_ref, sfin_ref, st_scr, bp_scr, *, tb, t_valid):
    C = GLA_CHUNK
    t_blk = pl.program_id(1)

    @pl.when(t_blk == 0)
    def _():
        for h in range(N_GLA_HEADS):
            st_scr[h] = s0_ref[h].T

    U = min(GLA_SUB, tb)
    levels = [w for w in (64, 32, 16, 8, 4, 2, 1) if w < U]
    row5 = lax.broadcasted_iota(jnp.int32, (U, GLA_KEY_WIDTH), 0)
    row1 = lax.broadcasted_iota(jnp.int32, (U, GLA_DK), 0)
    rowa = lax.broadcasted_iota(jnp.int32, (U, U), 0)
    cola = lax.broadcasted_iota(jnp.int32, (U, U), 1)
    ltri = (rowa >= cola).astype(bf16)
    off_diag = rowa != cola
    same_group = {w: ((rowa ^ cola) < 2 * w) & off_diag for w in levels}

    for sb in range(tb // U):
        r0 = sb * U
        g = g_ref[r0:r0 + U, :]
        if t_valid < C:
            g = jnp.where((row5 & (C - 1)) < t_valid, g, 0.0)
        g1 = g.astype(bf16)
        e1 = g - g1.astype(f32)
        g2 = e1.astype(bf16)
        g3 = (e1 - g2.astype(f32)).astype(bf16)
        bp = (jnp.dot(ltri, g1, preferred_element_type=f32) + jnp.dot(ltri, g2, preferred_element_type=f32)
              + jnp.dot(ltri, g3, preferred_element_type=f32))
        bp_scr[...] = bp
        b_last = bp[U - 1:U, :]
        q = q_ref[r0:r0 + U, :] * (GLA_DK ** -0.5)
        k = k_ref[r0:r0 + U, :]
        qe = q * jnp.exp(bp)
        ke = k * jnp.exp(b_last - bp)
        dec = jnp.exp(b_last)
        facs = []
        for w in levels:
            if w >= 4:
                bc = jnp.concatenate([jnp.broadcast_to(bp_scr[s + w - 1:s + w, :], (2 * w, GLA_KEY_WIDTH))
                                      for s in range(0, U, 2 * w)], axis=0)
            elif w == 2:
                pos = row5 & 3
                bc = jnp.where(pos == 0, pltpu.roll(bp, U - 1, 0),
                               jnp.where(pos == 1, bp, jnp.where(pos == 2, pltpu.roll(bp, 1, 0), pltpu.roll(bp, 2, 0))))
            else:
                bc = jnp.where((row5 & 1) == 1, pltpu.roll(bp, 1, 0), bp)
            facs.append(jnp.exp(jnp.where((row5 & w) != 0, bp - bc, bc - bp)))
        for h in range(N_GLA_HEADS):
            ks = slice(h * GLA_DK, (h + 1) * GLA_DK)
            vs = slice(h * GLA_DV, (h + 1) * GLA_DV)
            qh, kh = q[:, ks], k[:, ks]
            a = lax.dot_general(qh.astype(bf16), kh.astype(bf16), NT_DIMS, preferred_element_type=f32)
            a = jnp.where(rowa == cola, a, 0.0)
            for w, fac in zip(levels, facs):
                upper = (row1 & w) != 0
                fh = fac[:, ks]
                rq = jnp.where(upper, qh * fh, 0.0).astype(bf16)
                ck = jnp.where(upper, 0.0, kh * fh).astype(bf16)
                p = lax.dot_general(rq, ck, NT_DIMS, preferred_element_type=f32)
                a = jnp.where(same_group[w], p, a)
            vh = v_ref[r0:r0 + U, vs].astype(bf16)
            st = st_scr[h]
            o = jnp.dot(a.astype(bf16), vh, preferred_element_type=f32)
            o = o + lax.dot_general(qe[:, ks].astype(bf16), st.astype(bf16), NT_DIMS, preferred_element_type=f32)
            o_ref[r0:r0 + U, vs] = o
            kv_t = lax.dot_general(vh, ke[:, ks].astype(bf16), TN_DIMS, preferred_element_type=f32)
            st_scr[h] = dec[:, ks] * st + kv_t

    @pl.when(t_blk == pl.num_programs(1) - 1)
    def _():
        for h in range(N_GLA_HEADS):
            sfin_ref[h] = st_scr[h].T


def _gla(pg, loga, s0, *, batch, t_len, tb, t_valid):
    nt = t_len // tb
    kern = functools.partial(_gla_body, tb=tb, t_valid=t_valid)
    state_spec = pl.BlockSpec((None, N_GLA_HEADS, GLA_DK, GLA_DV), lambda b, t: (b, 0, 0, 0))
    return pl.pallas_call(
        kern,
        grid=(batch, nt),
        in_specs=[
            pl.BlockSpec((tb, GLA_KEY_WIDTH), lambda b, t: (b * nt + t, 0)),
            pl.BlockSpec((tb, GLA_KEY_WIDTH), lambda b, t: (b * nt + t, 1)),
            pl.BlockSpec((tb, GLA_WIDTH), lambda b, t: (b * nt + t, 1)),
            pl.BlockSpec((tb, GLA_KEY_WIDTH), lambda b, t: (b * nt + t, 0)),
            state_spec,
        ],
        out_specs=[pl.BlockSpec((tb, GLA_WIDTH), lambda b, t: (b * nt + t, 0)), state_spec],
        out_shape=[
            jax.ShapeDtypeStruct((batch * t_len, GLA_WIDTH), f32),
            jax.ShapeDtypeStruct((batch, N_GLA_HEADS, GLA_DK, GLA_DV), f32),
        ],
        scratch_shapes=[
            pltpu.VMEM((N_GLA_HEADS, GLA_DV, GLA_DK), f32),
            pltpu.VMEM((min(GLA_SUB, tb), GLA_KEY_WIDTH), f32),
        ],
        compiler_params=_cparams("parallel", "arbitrary"),
        name="gla",
    )(pg, pg, pg, loga, s0)


ATT_ROWS = ATT_BLOCK * max(DILATIONS)


def _att_prompt_body(q_ref, k_ref, v_ref, o_ref, qd_scr, kd1, vd1, kd4, vd4, kd16, vd16, od_scr, lse_scr):
    U = ATT_BLOCK
    h = pl.program_id(1)
    n = pl.program_id(2)
    kds, vds = (kd1, kd4, kd16), (vd1, vd4, vd16)

    @pl.when(n == 0)
    def _():
        for d, kd, vd in zip(DILATIONS, kds, vds):
            L = ATT_ROWS // d
            for r in range(d):
                kd[r * (U + L):r * (U + L) + U, :] = jnp.zeros((U, ATT_HEAD_DIM), bf16)
                vd[r * (U + L):r * (U + L) + U, :] = jnp.zeros((U, ATT_HEAD_DIM), bf16)

    qi = lax.broadcasted_iota(jnp.int32, (U, 2 * U), 0)
    ki = lax.broadcasted_iota(jnp.int32, (U, 2 * U), 1)
    dist = qi - ki + U
    in_window = (dist >= 0) & (dist <= SUB_WINDOW)
    slope = jnp.exp2(jnp.zeros((U, 2 * U), f32) - (h + 1).astype(f32) * (8.0 / N_ATT_HEADS))
    first_pen = jnp.where(n == 0, NEG_INF, 0.0)

    for di, d in enumerate(DILATIONS):
        L = ATT_ROWS // d
        nu = L // U
        kd, vd = kds[di], vds[di]
        bias = jnp.where(in_window, -(slope * float(d)) * dist.astype(f32), NEG_INF)
        for r in range(d):
            src = pl.ds(r, L, stride=d) if d > 1 else slice(None)
            base = r * (U + L)
            qd_scr[r * L:(r + 1) * L, :] = (q_ref[src, :] * ATT_SCALE).astype(bf16)
            kd[base + U:base + U + L, :] = k_ref[src, :].astype(bf16)
            vd[base + U:base + U + L, :] = v_ref[src, :].astype(bf16)

        def unit(idx, carry, d=d, L=L, nu=nu, kd=kd, vd=vd, bias=bias, di=di):
            r = idx // nu
            u = idx - r * nu
            qoff = pl.multiple_of(r * L + u * U, U)
            koff = pl.multiple_of(r * (U + L) + u * U, U)
            qq = qd_scr[pl.ds(qoff, U), :]
            kk = kd[pl.ds(koff, 2 * U), :]
            vv = vd[pl.ds(koff, 2 * U), :]
            pen = jnp.where(u == 0, first_pen, 0.0)
            s = lax.dot_general(qq, kk, NT_DIMS, preferred_element_type=f32) + bias
            s = s + jnp.where(ki < U, pen, 0.0)
            m = jnp.max(s, axis=-1, keepdims=True)
            p = jnp.exp(s - m)
            l = jnp.sum(p, axis=-1, keepdims=True)
            o = jnp.dot(p.astype(bf16), vv, preferred_element_type=f32) / l
            rows = pl.ds(u * (U * d) + r, U, stride=d) if d > 1 else pl.ds(pl.multiple_of(u * U, U), U)
            od_scr[di, rows, :] = o
            lse_scr[di, rows, :] = jnp.broadcast_to(m + jnp.log(l), (U, ATT_HEAD_DIM))
            return carry

        lax.fori_loop(0, d * nu, unit, 0, unroll=True)
        for r in range(d):
            base = r * (U + L)
            kd[base:base + U, :] = kd[base + L:base + L + U, :]
            vd[base:base + U, :] = vd[base + L:base + L + U, :]

    CH = 512
    for c in range(ATT_ROWS // CH):
        rs = slice(c * CH, (c + 1) * CH)
        ls = [lse_scr[di, rs, :] for di in range(len(DILATIONS))]
        mm = jnp.maximum(jnp.maximum(ls[0], ls[1]), ls[2])
        ws = [jnp.exp(x - mm) for x in ls]
        num = ws[0] * od_scr[0, rs, :] + ws[1] * od_scr[1, rs, :] + ws[2] * od_scr[2, rs, :]
        o_ref[rs, :] = num / (ws[0] + ws[1] + ws[2])


def _att_prompt(pa_hm, *, batch, seq):
    nb = seq // ATT_ROWS
    H, E, U = N_ATT_HEADS, ATT_HEAD_DIM, ATT_BLOCK
    blk = (None, ATT_ROWS, E)
    kv_scratch = []
    for d in DILATIONS:
        kv_scratch += [pltpu.VMEM((d * U + ATT_ROWS, E), bf16)] * 2
    return pl.pallas_call(
        _att_prompt_body,
        grid=(batch, H, nb),
        in_specs=[
            pl.BlockSpec(blk, lambda b, h, n: (h, b * nb + n, 0)),
            pl.BlockSpec(blk, lambda b, h, n: (H + h, b * nb + n, 0)),
            pl.BlockSpec(blk, lambda b, h, n: (2 * H + h, b * nb + n, 0)),
        ],
        out_specs=pl.BlockSpec((ATT_ROWS, E), lambda b, h, n: (b * nb + n, h)),
        out_shape=jax.ShapeDtypeStruct((batch * seq, H * E), f32),
        scratch_shapes=[pltpu.VMEM((ATT_ROWS, E), bf16)] + kv_scratch + [
            pltpu.VMEM((len(DILATIONS), ATT_ROWS, E), f32),
            pltpu.VMEM((len(DILATIONS), ATT_ROWS, E), f32),
        ],
        compiler_params=_cparams("parallel", "parallel", "arbitrary"),
        name="att_prompt",
    )(pa_hm, pa_hm, pa_hm)


SAMPLE_PAD = 16


def _att_sample_body(qkv_ref, k1_ref, v1_ref, k4_ref, v4_ref, k16_ref, v16_ref, o_ref, *, t_new):
    J = SUB_WINDOW
    hidx = lax.broadcasted_iota(jnp.int32, (N_ATT_HEADS, 1), 0)
    slope = jnp.exp2((hidx + 1).astype(f32) * (-8.0 / N_ATT_HEADS))
    slot = lax.broadcasted_iota(jnp.int32, (J, N_ATT_HEADS, 1), 0)
    q = [qkv_ref[i, 0] for i in range(t_new)]
    kn = [qkv_ref[i, 1] for i in range(t_new)]
    vn = [qkv_ref[i, 2] for i in range(t_new)]
    for i in range(t_new):
        s_new = [jnp.sum(q[i] * kn[n], axis=-1, keepdims=True) * ATT_SCALE for n in range(i + 1)]
        outs, lses = [], []
        for dil in DILATIONS:
            if dil == 1:
                kc, vc = k1_ref[...], v1_ref[...]
                dist = (J + i - slot).astype(f32)
                news = [(s_new[n] - slope * float(i - n), vn[n]) for n in range(i + 1)]
            else:
                kres_ref, vres_ref = (k4_ref, v4_ref) if dil == 4 else (k16_ref, v16_ref)
                kc, vc = kres_ref[:, i], vres_ref[:, i]
                dist = (J - slot).astype(f32)
                news = [(s_new[i], vn[i])]
            s = jnp.sum(kc * q[i][None], axis=-1, keepdims=True) * ATT_SCALE - (slope * float(dil))[None] * dist
            if dil == 1:
                s = jnp.where(slot >= i, s, NEG_INF)
            m = jnp.max(s, axis=0)
            for sn, _ in news:
                m = jnp.maximum(m, sn)
            p = jnp.exp(s - m[None])
            l = jnp.sum(p, axis=0)
            acc = jnp.sum(p * vc, axis=0)
            for sn, v in news:
                pn = jnp.exp(sn - m)
                l = l + pn
                acc = acc + pn * v
            outs.append(acc / l)
            lses.append(m + jnp.log(l))
        mm = jnp.maximum(jnp.maximum(lses[0], lses[1]), lses[2])
        ws = [jnp.exp(x - mm) for x in lses]
        tot = ws[0] + ws[1] + ws[2]
        o_ref[i] = (ws[0] * outs[0] + ws[1] * outs[1] + ws[2] * outs[2]) / tot


def _att_sample(qkv_s, cache_k, cache_v, *, batch, t_new):
    w_cache = cache_k.shape[1]
    J = SUB_WINDOW
    he = (N_ATT_HEADS, ATT_HEAD_DIM)
    assert w_cache == max(DILATIONS) * J and t_new <= 4
    specs, views = [], []
    for dil in DILATIONS:
        rows = w_cache // dil
        last = rows // J - 1
        for c in (cache_k, cache_v):
            if dil == 1:
                views.append(c)
                specs.append(pl.BlockSpec((None, J) + he, lambda b, last=last: (b, last, 0, 0)))
            else:
                views.append(c.reshape((batch, rows, dil) + he))
                specs.append(pl.BlockSpec((None, J, 4) + he, lambda b, last=last: (b, last, 0, 0, 0)))
    kern = functools.partial(_att_sample_body, t_new=t_new)
    return pl.pallas_call(
        kern,
        grid=(batch,),
        in_specs=[pl.BlockSpec((None, t_new, 3) + he, lambda b: (b, 0, 0, 0, 0))] + specs,
        out_specs=pl.BlockSpec((None, t_new) + he, lambda b: (b, 0, 0, 0)),
        out_shape=jax.ShapeDtypeStruct((batch, t_new) + he, f32),
        compiler_params=_cparams("parallel"),
        name="att_sample",
    )(qkv_s, *views)


def _outproj_body(x_ref, go_ref, gg_ref, att_ref, gnw_ref, anw_ref, wg_ref, wa_ref, out_ref):
    parts = []
    for h in range(N_GLA_HEADS):
        vs = slice(h * GLA_DV, (h + 1) * GLA_DV)
        y = _rms(go_ref[:, vs], gnw_ref[...])
        gate = gg_ref[:, vs]
        parts.append((y * (gate * jax.nn.sigmoid(gate))).astype(bf16))
    gla_part = jnp.concatenate(parts, axis=-1)
    att_part = _rms(att_ref[...], anw_ref[...]).astype(bf16)
    mix = jnp.dot(gla_part, wg_ref[...], preferred_element_type=f32)
    mix = mix + jnp.dot(att_part, wa_ref[...], preferred_element_type=f32)
    out_ref[...] = x_ref[...] + mix


def _outproj(x, gla_o, gg_src, gg_col, att_o, gla_norm_w, att_norm_w, w_out_bf, tm):
    n = x.shape[0]
    row = lambda i: (i, 0)
    const = lambda i: (0, 0)
    in_specs = [
        pl.BlockSpec((tm, D_MODEL), row),
        pl.BlockSpec((tm, GLA_WIDTH), row),
        pl.BlockSpec((tm, GLA_WIDTH), lambda i: (i, gg_col)),
        pl.BlockSpec((tm, ATT_WIDTH), row),
        pl.BlockSpec((1, GLA_DV), const),
        pl.BlockSpec((1, ATT_WIDTH), const),
        pl.BlockSpec((GLA_WIDTH, D_MODEL), lambda i: (0, 0)),
        pl.BlockSpec((ATT_WIDTH, D_MODEL), lambda i: (1, 0)),
    ]
    return pl.pallas_call(
        _outproj_body,
        grid=(n // tm,),
        in_specs=in_specs,
        out_specs=pl.BlockSpec((tm, D_MODEL), row),
        out_shape=jax.ShapeDtypeStruct((n, D_MODEL), f32),
        compiler_params=_cparams("parallel"),
        name="mixer_out_proj",
    )(x, gla_o, gg_src, att_o, gla_norm_w.reshape(1, GLA_DV), att_norm_w.reshape(1, ATT_WIDTH), w_out_bf, w_out_bf)


def _mlp_body(x_ref, fw_ref, wu_ref, wd_ref, nw_ref, o_ref, h_scr):
    f = pl.program_id(1)

    @pl.when(f == 0)
    def _():
        h_scr[...] = _rms(x_ref[...], fw_ref[...]).astype(bf16)
        o_ref[...] = jnp.zeros_like(o_ref)

    u = jnp.dot(h_scr[...], wu_ref[...], preferred_element_type=f32)
    a = jnp.square(jnp.maximum(u, 0.0)).astype(bf16)
    o_ref[...] += jnp.dot(a, wd_ref[...], preferred_element_type=f32)

    @pl.when(f == pl.num_programs(1) - 1)
    def _():
        o_ref[...] = _rms(x_ref[...] + o_ref[...], nw_ref[...])


def _mlp(x, ffn_norm_w, w_up_bf, w_down_bf, final_norm_w, tm, tf):
    n = x.shape[0]
    return pl.pallas_call(
        _mlp_body,
        grid=(n // tm, D_FF // tf),
        in_specs=[
            pl.BlockSpec((tm, D_MODEL), lambda i, f: (i, 0)),
            pl.BlockSpec((1, D_MODEL), lambda i, f: (0, 0)),
            pl.BlockSpec((D_MODEL, tf), lambda i, f: (0, f)),
            pl.BlockSpec((tf, D_MODEL), lambda i, f: (f, 0)),
            pl.BlockSpec((1, D_MODEL), lambda i, f: (0, 0)),
        ],
        out_specs=pl.BlockSpec((tm, D_MODEL), lambda i, f: (i, 0)),
        out_shape=jax.ShapeDtypeStruct((n, D_MODEL), f32),
        scratch_shapes=[pltpu.VMEM((tm, D_MODEL), bf16)],
        compiler_params=_cparams("parallel", "arbitrary"),
        name="mlp_final_norm",
    )(x, ffn_norm_w.reshape(1, D_MODEL), w_up_bf, w_down_bf, final_norm_w.reshape(1, D_MODEL))


def kernel(x_prompt, x_sample, cache_k_win, cache_v_win, state_gla, attn_norm_w, w_in, w_gk_up, b_gk, gla_norm_w,
           att_out_norm_w, w_out, ffn_norm_w, w_up, w_down, final_norm_w):
    depth = w_in.shape[0]
    assert depth == 1, "single trunk layer"
    B, S, _ = x_prompt.shape
    Bs, Ts, _ = x_sample.shape
    w_p = min(MAX_WINDOW, S)
    assert S % (ATT_BLOCK * max(DILATIONS)) == 0 and Ts <= GLA_CHUNK

    w_in0 = w_in[0]
    c_lr = GLA_COLS
    w_gla_bf, w_att_bf = _split_w_in(w_in0, 256)
    wup_pad = jnp.pad(w_gk_up[0], ((0, LANES - GLA_GATE_RANK), (0, 0)))
    w_out_bf = w_out[0].astype(bf16)
    w_up_bf = w_up[0].astype(bf16)
    w_down_bf = w_down[0].astype(bf16)

    xp = x_prompt.reshape(B * S, D_MODEL)
    pg, loga, pa = _project(xp, attn_norm_w[0], w_gla_bf, w_att_bf, w_in0, wup_pad, b_gk[0], tm=1024, head_major=True)
    k_win, v_win = _window_kv(pa, batch=B, seq=S, window=w_p, tr=512)
    s0 = jnp.zeros((B, N_GLA_HEADS, GLA_DK, GLA_DV), f32)
    gla_o, gla_state_p = _gla(pg, loga, s0, batch=B, t_len=S, tb=512, t_valid=GLA_CHUNK)
    att = _att_prompt(pa, batch=B, seq=S)
    x1 = _outproj(xp, gla_o, pg, 2, att, gla_norm_w[0], att_out_norm_w[0], w_out_bf, 512)
    y_prompt = _mlp(x1, ffn_norm_w[0], w_up_bf, w_down_bf, final_norm_w, 1024, 512).reshape(B, S, D_MODEL)

    P = SAMPLE_PAD
    xs_pad = jnp.pad(x_sample, ((0, 0), (0, P - Ts), (0, 0))).reshape(Bs * P, D_MODEL)
    pgs, logas, pas = _project(xs_pad, attn_norm_w[0], w_gla_bf, w_att_bf, w_in0, wup_pad, b_gk[0], tm=Bs * P,
                               head_major=False)
    qkv_s = pas.reshape(Bs, P, 3, N_ATT_HEADS, ATT_HEAD_DIM)[:, :Ts]
    gla_o_s, gla_state_s = _gla(pgs, logas, state_gla[0], batch=Bs, t_len=P, tb=P, t_valid=Ts)
    att_s = _att_sample(qkv_s, cache_k_win[0], cache_v_win[0], batch=Bs, t_new=Ts).reshape(Bs * Ts, ATT_WIDTH)
    gla_o_s = gla_o_s.reshape(Bs, P, GLA_WIDTH)[:, :Ts].reshape(Bs * Ts, GLA_WIDTH)
    gg_s = pgs.reshape(Bs, P, GLA_COLS)[:, :Ts, 2 * GLA_KEY_WIDTH + GLA_WIDTH:].reshape(Bs * Ts, GLA_WIDTH)
    xs = x_sample.reshape(Bs * Ts, D_MODEL)
    x1s = _outproj(xs, gla_o_s, gg_s, 0, att_s, gla_norm_w[0], att_out_norm_w[0], w_out_bf, Bs * Ts)
    y_sample = _mlp(x1s, ffn_norm_w[0], w_up_bf, w_down_bf, final_norm_w, Bs * Ts, 512).reshape(Bs, Ts, D_MODEL)

    k_win_prompt = k_win.reshape(1, B, w_p, N_ATT_HEADS, ATT_HEAD_DIM)
    v_win_prompt = v_win.reshape(1, B, w_p, N_ATT_HEADS, ATT_HEAD_DIM)
    k_new_sample = qkv_s[:, :, 1][None]
    v_new_sample = qkv_s[:, :, 2][None]
    return (y_prompt, y_sample, k_win_prompt, v_win_prompt, gla_state_p[None], k_new_sample, v_new_sample,
            gla_state_s[None])
```

```python
import functools

import jax
import jax.numpy as jnp
from jax import lax
from jax.experimental import pallas as pl
from jax.experimental.pallas import tpu as pltpu

f32 = jnp.float32
bf16 = jnp.bfloat16

D_MODEL = 2048
N_GLA_HEADS = 4
GLA_DK = 128
GLA_DV = 256
GLA_KEY_WIDTH = N_GLA_HEADS * GLA_DK
GLA_WIDTH = N_GLA_HEADS * GLA_DV
GLA_GATE_RANK = 16
GLA_GATE_NORM = 16.0
GLA_CHUNK = 16
N_ATT_HEADS = 8
ATT_HEAD_DIM = 128
ATT_WIDTH = N_ATT_HEADS * ATT_HEAD_DIM
DILATIONS = (1, 4, 16)
SUB_WINDOW = 128
ATT_BLOCK = 128
MAX_WINDOW = 2048
D_FF = 4 * D_MODEL
RMS_EPS = 1e-6
GLA_COLS = 2 * GLA_KEY_WIDTH + 2 * GLA_WIDTH
ATT_COLS = 3 * ATT_WIDTH
LANES = 128
VMEM_LIMIT = 56 * 1024 * 1024
NEG_INF = float("-inf")
ALIBI_SLOPES = tuple(2.0 ** (-8.0 * (h + 1) / N_ATT_HEADS) for h in range(N_ATT_HEADS))
ATT_SCALE = ATT_HEAD_DIM ** -0.5

NT_DIMS = (((1,), (1,)), ((), ()))
TN_DIMS = (((0,), (0,)), ((), ()))


def _cparams(*sem):
    return pltpu.CompilerParams(dimension_semantics=sem, vmem_limit_bytes=VMEM_LIMIT)


def _rms(x, w):
    r = lax.rsqrt(jnp.mean(x * x, axis=-1, keepdims=True) + RMS_EPS)
    return (x * r) * w


def _split_w_in_body(w_ref, g_ref, a_ref):
    g_ref[...] = w_ref[:GLA_COLS, :].astype(bf16)
    a_ref[...] = w_ref[GLA_COLS + GLA_GATE_RANK:, :].astype(bf16)


def _split_w_in(w_in_t, cols):
    rows, d = w_in_t.shape
    assert rows == GLA_COLS + GLA_GATE_RANK + ATT_COLS and d % cols == 0
    return pl.pallas_call(
        _split_w_in_body,
        grid=(d // cols,),
        in_specs=[pl.BlockSpec((rows, cols), lambda i: (0, i))],
        out_specs=[pl.BlockSpec((GLA_COLS, cols), lambda i: (0, i)), pl.BlockSpec((ATT_COLS, cols), lambda i: (0, i))],
        out_shape=[jax.ShapeDtypeStruct((GLA_COLS, d), bf16), jax.ShapeDtypeStruct((ATT_COLS, d), bf16)],
        compiler_params=_cparams("parallel"),
        name="split_w_in",
    )(w_in_t)


PROJ_TN = 512
N_GLA_TILES = GLA_COLS // PROJ_TN
N_ATT_TILES = ATT_COLS // PROJ_TN
TILES_PER_QKV = ATT_WIDTH // PROJ_TN
HEADS_PER_TILE = PROJ_TN // ATT_HEAD_DIM


def _proj_body(x_ref, nw_ref, wg_ref, wa_ref, wlr_ref, wup_ref, bgk_ref, pg_ref, loga_ref, pa_ref, h_scr, *,
               head_major):
    j = pl.program_id(1)

    @pl.when(j == 0)
    def _():
        h = _rms(x_ref[...], nw_ref[...]).astype(bf16)
        h_scr[...] = h
        glr = lax.dot_general(h, wlr_ref[...].astype(bf16), NT_DIMS, preferred_element_type=f32)
        R = GLA_GATE_RANK
        lane = lax.broadcasted_iota(jnp.int32, glr.shape, 1)
        g = jnp.where(lane < R, glr, 0.0)
        g_hi = g.astype(bf16).astype(f32)
        g3 = (g_hi + pltpu.roll(g_hi, R, 1) + pltpu.roll(g - g_hi, 2 * R, 1)).astype(bf16)
        w = wup_ref[...]
        w_hi = w.astype(bf16).astype(f32)
        w3 = (w_hi + pltpu.roll(w - w_hi, R, 0) + pltpu.roll(w_hi, 2 * R, 0)).astype(bf16)
        pre = jnp.dot(g3, w3, preferred_element_type=f32) + bgk_ref[...]
        log_sig = jnp.minimum(pre, 0.0) - jnp.log1p(jnp.exp(-jnp.abs(pre)))
        loga_ref[...] = log_sig * (1.0 / GLA_GATE_NORM)

    @pl.when(j < N_GLA_TILES)
    def _():
        pg_ref[...] = lax.dot_general(h_scr[...], wg_ref[...], NT_DIMS, preferred_element_type=f32)

    @pl.when(j >= N_GLA_TILES)
    def _():
        res = lax.dot_general(h_scr[...], wa_ref[...], NT_DIMS, preferred_element_type=f32)
        if head_major:
            for g in range(HEADS_PER_TILE):
                pa_ref[g] = res[:, g * LANES:(g + 1) * LANES]
        else:
            pa_ref[...] = res


def _project(x2d, norm_w, w_gla_bf, w_att_bf, w_in0, wup_pad, b_gk, *, tm, head_major):
    n, d = x2d.shape
    nb = n // tm
    nj = N_GLA_TILES + N_ATT_TILES
    gla_j = lambda j: jnp.minimum(j, N_GLA_TILES - 1)
    att_j = lambda j: jnp.maximum(j - N_GLA_TILES, 0)
    assert GLA_COLS % LANES == 0
    in_specs = [
        pl.BlockSpec((tm, d), lambda i, j: (i, 0)),
        pl.BlockSpec((1, d), lambda i, j: (0, 0)),
        pl.BlockSpec((PROJ_TN, d), lambda i, j: (gla_j(j), 0)),
        pl.BlockSpec((PROJ_TN, d), lambda i, j: (att_j(j), 0)),
        pl.BlockSpec((LANES, d), lambda i, j: (GLA_COLS // LANES, 0)),
        pl.BlockSpec((LANES, GLA_KEY_WIDTH), lambda i, j: (0, 0)),
        pl.BlockSpec((1, GLA_KEY_WIDTH), lambda i, j: (0, 0)),
    ]
    out_specs = [
        pl.BlockSpec((tm, PROJ_TN), lambda i, j: (i, gla_j(j))),
        pl.BlockSpec((tm, GLA_KEY_WIDTH), lambda i, j: (i, 0)),
    ]
    out_shape = [jax.ShapeDtypeStruct((n, GLA_COLS), f32), jax.ShapeDtypeStruct((n, GLA_KEY_WIDTH), f32)]
    if head_major:
        out_specs.append(pl.BlockSpec((HEADS_PER_TILE, tm, LANES), lambda i, j: (att_j(j), i, 0)))
        out_shape.append(jax.ShapeDtypeStruct((3 * N_ATT_HEADS, n, ATT_HEAD_DIM), f32))
    else:
        out_specs.append(pl.BlockSpec((tm, PROJ_TN), lambda i, j: (i, att_j(j))))
        out_shape.append(jax.ShapeDtypeStruct((n, ATT_COLS), f32))
    return pl.pallas_call(
        functools.partial(_proj_body, head_major=head_major),
        grid=(nb, nj),
        in_specs=in_specs,
        out_specs=out_specs,
        out_shape=out_shape,
        scratch_shapes=[pltpu.VMEM((tm, d), bf16)],
        compiler_params=_cparams("parallel", "arbitrary"),
        name="norm_in_proj",
    )(x2d, norm_w.reshape(1, d), w_gla_bf, w_att_bf, w_in0, wup_pad, b_gk.reshape(1, GLA_KEY_WIDTH))


def _window_body(kh_ref, vh_ref, ko_ref, vo_ref):
    tr = kh_ref.shape[1]
    for src, dst in ((kh_ref, ko_ref), (vh_ref, vo_ref)):
        for hd in range(N_ATT_HEADS):
            dst[pl.ds(hd, tr, stride=N_ATT_HEADS), :] = src[hd]


def _window_kv(pa_hm, *, batch, seq, window, tr):
    H, E = N_ATT_HEADS, ATT_HEAD_DIM
    assert seq % tr == 0 and window % tr == 0
    per_seq, first = seq // tr, (seq - window) // tr
    nw = window // tr
    src = lambda which: pl.BlockSpec((H, tr, E), lambda b, t: (which, b * per_seq + first + t, 0))
    dst = pl.BlockSpec((tr * H, E), lambda b, t: (b * nw + t, 0))
    shape = jax.ShapeDtypeStruct((batch * window * H, E), f32)
    return pl.pallas_call(
        _window_body,
        grid=(batch, nw),
        in_specs=[src(1), src(2)],
        out_specs=[dst, dst],
        out_shape=[shape, shape],
        compiler_params=_cparams("parallel", "parallel"),
        name="kv_window",
    )(pa_hm, pa_hm)


GLA_SUB = 128


def _gla_body(q_ref, k_ref, v_ref, g_ref, s0_ref, o_ref, sfin_ref, st_scr, bp_scr, *, tb, t_valid):
    C = GLA_CHUNK
    t_blk = pl.program_id(1)

    @pl.when(t_blk == 0)
    def _():
        for h in range(N_GLA_HEADS):
            st_scr[h] = s0_ref[h].T

    U = min(GLA_SUB, tb)
    levels = [w for w in (64, 32, 16, 8, 4, 2, 1) if w < U]
    row5 = lax.broadcasted_iota(jnp.int32, (U, GLA_KEY_WIDTH), 0)
    row1 = lax.broadcasted_iota(jnp.int32, (U, GLA_DK), 0)
    rowa = lax.broadcasted_iota(jnp.int32, (U, U), 0)
    cola = lax.broadcasted_iota(jnp.int32, (U, U), 1)
    ltri = (rowa >= cola).astype(bf16)
    off_diag = rowa != cola
    same_group = {w: ((rowa ^ cola) < 2 * w) & off_diag for w in levels}

    for sb in range(tb // U):
        r0 = sb * U
        g = g_ref[r0:r0 + U, :]
        if t_valid < C:
            g = jnp.where((row5 & (C - 1)) < t_valid, g, 0.0)
        g1 = g.astype(bf16)
        e1 = g - g1.astype(f32)
        g2 = e1.astype(bf16)
        g3 = (e1 - g2.astype(f32)).astype(bf16)
        bp = (jnp.dot(ltri, g1, preferred_element_type=f32) + jnp.dot(ltri, g2, preferred_element_type=f32)
              + jnp.dot(ltri, g3, preferred_element_type=f32))
        bp_scr[...] = bp
        b_last = bp[U - 1:U, :]
        q = q_ref[r0:r0 + U, :] * (GLA_DK ** -0.5)
        k = k_ref[r0:r0 + U, :]
        qe = q * jnp.exp(bp)
        ke = k * jnp.exp(b_last - bp)
        dec = jnp.exp(b_last)
        facs = []
        for w in levels:
            if w >= 4:
                bc = jnp.concatenate([jnp.broadcast_to(bp_scr[s + w - 1:s + w, :], (2 * w, GLA_KEY_WIDTH))
                                      for s in range(0, U, 2 * w)], axis=0)
            elif w == 2:
                pos = row5 & 3
                bc = jnp.where(pos == 0, pltpu.roll(bp, U - 1, 0),
                               jnp.where(pos == 1, bp, jnp.where(pos == 2, pltpu.roll(bp, 1, 0), pltpu.roll(bp, 2, 0))))
            else:
                bc = jnp.where((row5 & 1) == 1, pltpu.roll(bp, 1, 0), bp)
            facs.append(jnp.exp(jnp.where((row5 & w) != 0, bp - bc, bc - bp)))
        for h in range(N_GLA_HEADS):
            ks = slice(h * GLA_DK, (h + 1) * GLA_DK)
            vs = slice(h * GLA_DV, (h + 1) * GLA_DV)
            qh, kh = q[:, ks], k[:, ks]
            a = lax.dot_general(qh.astype(bf16), kh.astype(bf16), NT_DIMS, preferred_element_type=f32)
            a = jnp.where(rowa == cola, a, 0.0)
            for w, fac in zip(levels, facs):
                upper = (row1 & w) != 0
                fh = fac[:, ks]
                rq = jnp.where(upper, qh * fh, 0.0).astype(bf16)
                ck = jnp.where(upper, 0.0, kh * fh).astype(bf16)
                p = lax.dot_general(rq, ck, NT_DIMS, preferred_element_type=f32)
                a = jnp.where(same_group[w], p, a)
            vh = v_ref[r0:r0 + U, vs].astype(bf16)
            st = st_scr[h]
            o = jnp.dot(a.astype(bf16), vh, preferred_element_type=f32)
            o = o + lax.dot_general(qe[:, ks].astype(bf16), st.astype(bf16), NT_DIMS, preferred_element_type=f32)
            o_ref[r0:r0 + U, vs] = o
            kv_t = lax.dot_general(vh, ke[:, ks].astype(bf16), TN_DIMS, preferred_element_type=f32)
            st_scr[h] = dec[:, ks] * st + kv_t

    @pl.when(t_blk == pl.num_programs(1) - 1)
    def _():
        for h in range(N_GLA_HEADS):
            sfin_ref[h] = st_scr[h].T


def _gla(pg, loga, s0, *, batch, t_len, tb, t_valid):
    nt = t_len // tb
    kern = functools.partial(_gla_body, tb=tb, t_valid=t_valid)
    state_spec = pl.BlockSpec((None, N_GLA_HEADS, GLA_DK, GLA_DV), lambda b, t: (b, 0, 0, 0))
    return pl.pallas_call(
        kern,
        grid=(batch, nt),
        in_specs=[
            pl.BlockSpec((tb, GLA_KEY_WIDTH), lambda b, t: (b * nt + t, 0)),
            pl.BlockSpec((tb, GLA_KEY_WIDTH), lambda b, t: (b * nt + t, 1)),
            pl.BlockSpec((tb, GLA_WIDTH), lambda b, t: (b * nt + t, 1)),
            pl.BlockSpec((tb, GLA_KEY_WIDTH), lambda b, t: (b * nt + t, 0)),
            state_spec,
        ],
        out_specs=[pl.BlockSpec((tb, GLA_WIDTH), lambda b, t: (b * nt + t, 0)), state_spec],
        out_shape=[
            jax.ShapeDtypeStruct((batch * t_len, GLA_WIDTH), f32),
            jax.ShapeDtypeStruct((batch, N_GLA_HEADS, GLA_DK, GLA_DV), f32),
        ],
        scratch_shapes=[
            pltpu.VMEM((N_GLA_HEADS, GLA_DV, GLA_DK), f32),
            pltpu.VMEM((min(GLA_SUB, tb), GLA_KEY_WIDTH), f32),
        ],
        compiler_params=_cparams("parallel", "arbitrary"),
        name="gla",
    )(pg, pg, pg, loga, s0)


ATT_ROWS = ATT_BLOCK * max(DILATIONS)


def _att_prompt_body(q_ref, k_ref, v_ref, o_ref, qd_scr, kd1, vd1, kd4, vd4, kd16, vd16, od_scr, lse_scr):
    U = ATT_BLOCK
    h = pl.program_id(1)
    n = pl.program_id(2)
    kds, vds = (kd1, kd4, kd16), (vd1, vd4, vd16)

    @pl.when(n == 0)
    def _():
        for d, kd, vd in zip(DILATIONS, kds, vds):
            L = ATT_ROWS // d
            for r in range(d):
                kd[r * (U + L):r * (U + L) + U, :] = jnp.zeros((U, ATT_HEAD_DIM), bf16)
                vd[r * (U + L):r * (U + L) + U, :] = jnp.zeros((U, ATT_HEAD_DIM), bf16)

    qi = lax.broadcasted_iota(jnp.int32, (U, 2 * U), 0)
    ki = lax.broadcasted_iota(jnp.int32, (U, 2 * U), 1)
    dist = qi - ki + U
    in_window = (dist >= 0) & (dist <= SUB_WINDOW)
    slope = jnp.exp2(jnp.zeros((U, 2 * U), f32) - (h + 1).astype(f32) * (8.0 / N_ATT_HEADS))
    first_pen = jnp.where(n == 0, NEG_INF, 0.0)

    for di, d in enumerate(DILATIONS):
        L = ATT_ROWS // d
        nu = L // U
        kd, vd = kds[di], vds[di]
        bias = jnp.where(in_window, -(slope * float(d)) * dist.astype(f32), NEG_INF)
        for r in range(d):
            src = pl.ds(r, L, stride=d) if d > 1 else slice(None)
            base = r * (U + L)
            qd_scr[r * L:(r + 1) * L, :] = (q_ref[src, :] * ATT_SCALE).astype(bf16)
            kd[base + U:base + U + L, :] = k_ref[src, :].astype(bf16)
            vd[base + U:base + U + L, :] = v_ref[src, :].astype(bf16)

        def unit(idx, carry, d=d, L=L, nu=nu, kd=kd, vd=vd, bias=bias, di=di):
            r = idx // nu
            u = idx - r * nu
            qoff = pl.multiple_of(r * L + u * U, U)
            koff = pl.multiple_of(r * (U + L) + u * U, U)
            qq = qd_scr[pl.ds(qoff, U), :]
            kk = kd[pl.ds(koff, 2 * U), :]
            vv = vd[pl.ds(koff, 2 * U), :]
            pen = jnp.where(u == 0, first_pen, 0.0)
            s = lax.dot_general(qq, kk, NT_DIMS, preferred_element_type=f32) + bias
            s = s + jnp.where(ki < U, pen, 0.0)
            m = jnp.max(s, axis=-1, keepdims=True)
            p = jnp.exp(s - m)
            l = jnp.sum(p, axis=-1, keepdims=True)
            o = jnp.dot(p.astype(bf16), vv, preferred_element_type=f32) / l
            rows = pl.ds(u * (U * d) + r, U, stride=d) if d > 1 else pl.ds(pl.multiple_of(u * U, U), U)
            od_scr[di, rows, :] = o
            lse_scr[di, rows, :] = jnp.broadcast_to(m + jnp.log(l), (U, ATT_HEAD_DIM))
            return carry

        lax.fori_loop(0, d * nu, unit, 0, unroll=True)
        for r in range(d):
            base = r * (U + L)
            kd[base:base + U, :] = kd[base + L:base + L + U, :]
            vd[base:base + U, :] = vd[base + L:base + L + U, :]

    CH = 512
    for c in range(ATT_ROWS // CH):
        rs = slice(c * CH, (c + 1) * CH)
        ls = [lse_scr[di, rs, :] for di in range(len(DILATIONS))]
        mm = jnp.maximum(jnp.maximum(ls[0], ls[1]), ls[2])
        ws = [jnp.exp(x - mm) for x in ls]
        num = ws[0] * od_scr[0, rs, :] + ws[1] * od_scr[1, rs, :] + ws[2] * od_scr[2, rs, :]
        o_ref[rs, :] = num / (ws[0] + ws[1] + ws[2])


def _att_prompt(pa_hm, *, batch, seq):
    nb = seq // ATT_ROWS
    H, E, U = N_ATT_HEADS, ATT_HEAD_DIM, ATT_BLOCK
    blk = (None, ATT_ROWS, E)
    kv_scratch = []
    for d in DILATIONS:
        kv_scratch += [pltpu.VMEM((d * U + ATT_ROWS, E), bf16)] * 2
    return pl.pallas_call(
        _att_prompt_body,
        grid=(batch, H, nb),
        in_specs=[
            pl.BlockSpec(blk, lambda b, h, n: (h, b * nb + n, 0)),
            pl.BlockSpec(blk, lambda b, h, n: (H + h, b * nb + n, 0)),
            pl.BlockSpec(blk, lambda b, h, n: (2 * H + h, b * nb + n, 0)),
        ],
        out_specs=pl.BlockSpec((ATT_ROWS, E), lambda b, h, n: (b * nb + n, h)),
        out_shape=jax.ShapeDtypeStruct((batch * seq, H * E), f32),
        scratch_shapes=[pltpu.VMEM((ATT_ROWS, E), bf16)] + kv_scratch + [
            pltpu.VMEM((len(DILATIONS), ATT_ROWS, E), f32),
            pltpu.VMEM((len(DILATIONS), ATT_ROWS, E), f32),
        ],
        compiler_params=_cparams("parallel", "parallel", "arbitrary"),
        name="att_prompt",
    )(pa_hm, pa_hm, pa_hm)


SAMPLE_PAD = 16


def _att_sample_body(qkv_ref, k1_ref, v1_ref, k4_ref, v4_ref, k16_ref, v16_ref, o_ref, *, t_new):
    J = SUB_WINDOW
    hidx = lax.broadcasted_iota(jnp.int32, (N_ATT_HEADS, 1), 0)
    slope = jnp.exp2((hidx + 1).astype(f32) * (-8.0 / N_ATT_HEADS))
    slot = lax.broadcasted_iota(jnp.int32, (J, N_ATT_HEADS, 1), 0)
    q = [qkv_ref[i, 0] for i in range(t_new)]
    kn = [qkv_ref[i, 1] for i in range(t_new)]
    vn = [qkv_ref[i, 2] for i in range(t_new)]
    for i in range(t_new):
        s_new = [jnp.sum(q[i] * kn[n], axis=-1, keepdims=True) * ATT_SCALE for n in range(i + 1)]
        outs, lses = [], []
        for dil in DILATIONS:
            if dil == 1:
                kc, vc = k1_ref[...], v1_ref[...]
                dist = (J + i - slot).astype(f32)
                news = [(s_new[n] - slope * float(i - n), vn[n]) for n in range(i + 1)]
            else:
                kres_ref, vres_ref = (k4_ref, v4_ref) if dil == 4 else (k16_ref, v16_ref)
                kc, vc = kres_ref[:, i], vres_ref[:, i]
                dist = (J - slot).astype(f32)
                news = [(s_new[i], vn[i])]
            s = jnp.sum(kc * q[i][None], axis=-1, keepdims=True) * ATT_SCALE - (slope * float(dil))[None] * dist
            if dil == 1:
                s = jnp.where(slot >= i, s, NEG_INF)
            m = jnp.max(s, axis=0)
            for sn, _ in news:
                m = jnp.maximum(m, sn)
            p = jnp.exp(s - m[None])
            l = jnp.sum(p, axis=0)
            acc = jnp.sum(p * vc, axis=0)
            for sn, v in news:
                pn = jnp.exp(sn - m)
                l = l + pn
                acc = acc + pn * v
            outs.append(acc / l)
            lses.append(m + jnp.log(l))
        mm = jnp.maximum(jnp.maximum(lses[0], lses[1]), lses[2])
        ws = [jnp.exp(x - mm) for x in lses]
        tot = ws[0] + ws[1] + ws[2]
        o_ref[i] = (ws[0] * outs[0] + ws[1] * outs[1] + ws[2] * outs[2]) / tot


def _att_sample(qkv_s, cache_k, cache_v, *, batch, t_new):
    w_cache = cache_k.shape[1]
    J = SUB_WINDOW
    he = (N_ATT_HEADS, ATT_HEAD_DIM)
    assert w_cache == max(DILATIONS) * J and t_new <= 4
    specs, views = [], []
    for dil in DILATIONS:
        rows = w_cache // dil
        last = rows // J - 1
        for c in (cache_k, cache_v):
            if dil == 1:
                views.append(c)
                specs.append(pl.BlockSpec((None, J) + he, lambda b, last=last: (b, last, 0, 0)))
            else:
                views.append(c.reshape((batch, rows, dil) + he))
                specs.append(pl.BlockSpec((None, J, 4) + he, lambda b, last=last: (b, last, 0, 0, 0)))
    kern = functools.partial(_att_sample_body, t_new=t_new)
    return pl.pallas_call(
        kern,
        grid=(batch,),
        in_specs=[pl.BlockSpec((None, t_new, 3) + he, lambda b: (b, 0, 0, 0, 0))] + specs,
        out_specs=pl.BlockSpec((None, t_new) + he, lambda b: (b, 0, 0, 0)),
        out_shape=jax.ShapeDtypeStruct((batch, t_new) + he, f32),
        compiler_params=_cparams("parallel"),
        name="att_sample",
    )(qkv_s, *views)


def _outproj_body(x_ref, go_ref, gg_ref, att_ref, gnw_ref, anw_ref, wg_ref, wa_ref, out_ref):
    parts = []
    for h in range(N_GLA_HEADS):
        vs = slice(h * GLA_DV, (h + 1) * GLA_DV)
        y = _rms(go_ref[:, vs], gnw_ref[...])
        gate = gg_ref[:, vs]
        parts.append((y * (gate * jax.nn.sigmoid(gate))).astype(bf16))
    gla_part = jnp.concatenate(parts, axis=-1)
    att_part = _rms(att_ref[...], anw_ref[...]).astype(bf16)
    mix = jnp.dot(gla_part, wg_ref[...], preferred_element_type=f32)
    mix = mix + jnp.dot(att_part, wa_ref[...], preferred_element_type=f32)
    out_ref[...] = x_ref[...] + mix


def _outproj(x, gla_o, gg_src, gg_col, att_o, gla_norm_w, att_norm_w, w_out_bf, tm):
    n = x.shape[0]
    row = lambda i: (i, 0)
    const = lambda i: (0, 0)
    in_specs = [
        pl.BlockSpec((tm, D_MODEL), row),
        pl.BlockSpec((tm, GLA_WIDTH), row),
        pl.BlockSpec((tm, GLA_WIDTH), lambda i: (i, gg_col)),
        pl.BlockSpec((tm, ATT_WIDTH), row),
        pl.BlockSpec((1, GLA_DV), const),
        pl.BlockSpec((1, ATT_WIDTH), const),
        pl.BlockSpec((GLA_WIDTH, D_MODEL), lambda i: (0, 0)),
        pl.BlockSpec((ATT_WIDTH, D_MODEL), lambda i: (1, 0)),
    ]
    return pl.pallas_call(
        _outproj_body,
        grid=(n // tm,),
        in_specs=in_specs,
        out_specs=pl.BlockSpec((tm, D_MODEL), row),
        out_shape=jax.ShapeDtypeStruct((n, D_MODEL), f32),
        compiler_params=_cparams("parallel"),
        name="mixer_out_proj",
    )(x, gla_o, gg_src, att_o, gla_norm_w.reshape(1, GLA_DV), att_norm_w.reshape(1, ATT_WIDTH), w_out_bf, w_out_bf)


def _mlp_body(x_ref, fw_ref, wu_ref, wd_ref, nw_ref, o_ref, h_scr):
    f = pl.program_id(1)

    @pl.when(f == 0)
    def _():
        h_scr[...] = _rms(x_ref[...], fw_ref[...]).astype(bf16)
        o_ref[...] = jnp.zeros_like(o_ref)

    u = jnp.dot(h_scr[...], wu_ref[...], preferred_element_type=f32)
    a = jnp.square(jnp.maximum(u, 0.0)).astype(bf16)
    o_ref[...] += jnp.dot(a, wd_ref[...], preferred_element_type=f32)

    @pl.when(f == pl.num_programs(1) - 1)
    def _():
        o_ref[...] = _rms(x_ref[...] + o_ref[...], nw_ref[...])


def _mlp(x, ffn_norm_w, w_up_bf, w_down_bf, final_norm_w, tm, tf):
    n = x.shape[0]
    return pl.pallas_call(
        _mlp_body,
        grid=(n // tm, D_FF // tf),
        in_specs=[
            pl.BlockSpec((tm, D_MODEL), lambda i, f: (i, 0)),
            pl.BlockSpec((1, D_MODEL), lambda i, f: (0, 0)),
            pl.BlockSpec((D_MODEL, tf), lambda i, f: (0, f)),
            pl.BlockSpec((tf, D_MODEL), lambda i, f: (f, 0)),
            pl.BlockSpec((1, D_MODEL), lambda i, f: (0, 0)),
        ],
        out_specs=pl.BlockSpec((tm, D_MODEL), lambda i, f: (i, 0)),
        out_shape=jax.ShapeDtypeStruct((n, D_MODEL), f32),
        scratch_shapes=[pltpu.VMEM((tm, D_MODEL), bf16)],
        compiler_params=_cparams("parallel", "arbitrary"),
        name="mlp_final_norm",
    )(x, ffn_norm_w.reshape(1, D_MODEL), w_up_bf, w_down_bf, final_norm_w.reshape(1, D_MODEL))


def kernel(x_prompt, x_sample, cache_k_win, cache_v_win, state_gla, attn_norm_w, w_in, w_gk_up, b_gk, gla_norm_w,
           att_out_norm_w, w_out, ffn_norm_w, w_up, w_down, final_norm_w):
    depth = w_in.shape[0]
    assert depth == 1, "single trunk layer"
    B, S, _ = x_prompt.shape
    Bs, Ts, _ = x_sample.shape
    w_p = min(MAX_WINDOW, S)
    assert S % (ATT_BLOCK * max(DILATIONS)) == 0 and Ts <= GLA_CHUNK

    w_in0 = jnp.swapaxes(w_in[0], 0, 1)
    w_gla_bf, w_att_bf = _split_w_in(w_in0, 256)
    wup_pad = jnp.pad(w_gk_up[0], ((0, LANES - GLA_GATE_RANK), (0, 0)))
    w_out_bf = w_out[0].astype(bf16)
    w_up_bf = w_up[0].astype(bf16)
    w_down_bf = w_down[0].astype(bf16)

    xp = x_prompt.reshape(B * S, D_MODEL)
    pg, loga, pa = _project(xp, attn_norm_w[0], w_gla_bf, w_att_bf, w_in0, wup_pad, b_gk[0], tm=1024, head_major=True)
    k_win, v_win = _window_kv(pa, batch=B, seq=S, window=w_p, tr=512)
    s0 = jnp.zeros((B, N_GLA_HEADS, GLA_DK, GLA_DV), f32)
    gla_o, gla_state_p = _gla(pg, loga, s0, batch=B, t_len=S, tb=512, t_valid=GLA_CHUNK)
    att = _att_prompt(pa, batch=B, seq=S)
    x1 = _outproj(xp, gla_o, pg, 2, att, gla_norm_w[0], att_out_norm_w[0], w_out_bf, 512)
    y_prompt = _mlp(x1, ffn_norm_w[0], w_up_bf, w_down_bf, final_norm_w, 1024, 512).reshape(B, S, D_MODEL)

    P = SAMPLE_PAD
    xs_pad = jnp.pad(x_sample, ((0, 0), (0, P - Ts), (0, 0))).reshape(Bs * P, D_MODEL)
    pgs, logas, pas = _project(xs_pad, attn_norm_w[0], w_gla_bf, w_att_bf, w_in0, wup_pad, b_gk[0], tm=Bs * P,
                               head_major=False)
    qkv_s = pas.reshape(Bs, P, 3, N_ATT_HEADS, ATT_HEAD_DIM)[:, :Ts]
    gla_o_s, gla_state_s = _gla(pgs, logas, state_gla[0], batch=Bs, t_len=P, tb=P, t_valid=Ts)
    att_s = _att_sample(qkv_s, cache_k_win[0], cache_v_win[0], batch=Bs, t_new=Ts).reshape(Bs * Ts, ATT_WIDTH)
    gla_o_s = gla_o_s.reshape(Bs, P, GLA_WIDTH)[:, :Ts].reshape(Bs * Ts, GLA_WIDTH)
    gg_s = pgs.reshape(Bs, P, GLA_COLS)[:, :Ts, 2 * GLA_KEY_WIDTH + GLA_WIDTH:].reshape(Bs * Ts, GLA_WIDTH)
    xs = x_sample.reshape(Bs * Ts, D_MODEL)
    x1s = _outproj(xs, gla_o_s, gg_s, 0, att_s, gla_norm_w[0], att_out_norm_w[0], w_out_bf, Bs * Ts)
    y_sample = _mlp(x1s, ffn_norm_w[0], w_up_bf, w_down_bf, final_norm_w, Bs * Ts, 512).reshape(Bs, Ts, D_MODEL)

    k_win_prompt = k_win.reshape(1, B, w_p, N_ATT_HEADS, ATT_HEAD_DIM)
    v_win_prompt = v_win.reshape(1, B, w_p, N_ATT_HEADS, ATT_HEAD_DIM)
    k_new_sample = qkv_s[:, :, 1][None]
    v_new_sample = qkv_s[:, :, 2][None]
    return (y_prompt, y_sample, k_win_prompt, v_win_prompt, gla_state_p[None], k_new_sample, v_new_sample,
            gla_state_s[None])
```

```python
import functools

import jax
import jax.numpy as jnp
from jax import lax
from jax.experimental import pallas as pl
from jax.experimental.pallas import tpu as pltpu

f32 = jnp.float32
bf16 = jnp.bfloat16

D_MODEL = 2048
N_GLA_HEADS = 4
GLA_DK = 128
GLA_DV = 256
GLA_KEY_WIDTH = N_GLA_HEADS * GLA_DK
GLA_WIDTH = N_GLA_HEADS * GLA_DV
GLA_GATE_RANK = 16
GLA_GATE_NORM = 16.0
GLA_CHUNK = 16
N_ATT_HEADS = 8
ATT_HEAD_DIM = 128
ATT_WIDTH = N_ATT_HEADS * ATT_HEAD_DIM
DILATIONS = (1, 4, 16)
SUB_WINDOW = 128
ATT_BLOCK = 128
MAX_WINDOW = 2048
D_FF = 4 * D_MODEL
RMS_EPS = 1e-6
GLA_COLS = 2 * GLA_KEY_WIDTH + 2 * GLA_WIDTH
ATT_COLS = 3 * ATT_WIDTH
LANES = 128
VMEM_LIMIT = 56 * 1024 * 1024
NEG_INF = float("-inf")
ALIBI_SLOPES = tuple(2.0 ** (-8.0 * (h + 1) / N_ATT_HEADS) for h in range(N_ATT_HEADS))
ATT_SCALE = ATT_HEAD_DIM ** -0.5

NT_DIMS = (((1,), (1,)), ((), ()))
TN_DIMS = (((0,), (0,)), ((), ()))


def _cparams(*sem):
    return pltpu.CompilerParams(dimension_semantics=sem, vmem_limit_bytes=VMEM_LIMIT)


def _rms(x, w):
    r = lax.rsqrt(jnp.mean(x * x, axis=-1, keepdims=True) + RMS_EPS)
    return (x * r) * w


def _split_w_in_body(w_ref, g_ref, a_ref):
    g_ref[...] = w_ref[:GLA_COLS, :].astype(bf16)
    a_ref[...] = w_ref[GLA_COLS + GLA_GATE_RANK:, :].astype(bf16)


def _split_w_in(w_in_t, cols):
    rows, d = w_in_t.shape
    assert rows == GLA_COLS + GLA_GATE_RANK + ATT_COLS and d % cols == 0
    return pl.pallas_call(
        _split_w_in_body,
        grid=(d // cols,),
        in_specs=[pl.BlockSpec((rows, cols), lambda i: (0, i))],
        out_specs=[pl.BlockSpec((GLA_COLS, cols), lambda i: (0, i)), pl.BlockSpec((ATT_COLS, cols), lambda i: (0, i))],
        out_shape=[jax.ShapeDtypeStruct((GLA_COLS, d), bf16), jax.ShapeDtypeStruct((ATT_COLS, d), bf16)],
        compiler_params=_cparams("parallel"),
        name="split_w_in",
    )(w_in_t)


PROJ_TN = 512
N_GLA_TILES = GLA_COLS // PROJ_TN
N_ATT_TILES = ATT_COLS // PROJ_TN
TILES_PER_QKV = ATT_WIDTH // PROJ_TN
HEADS_PER_TILE = PROJ_TN // ATT_HEAD_DIM


def _proj_body(x_ref, nw_ref, wg_ref, wa_ref, wlr_ref, wup_ref, bgk_ref, pg_ref, loga_ref, pa_ref, h_scr, *,
               head_major):
    j = pl.program_id(1)

    @pl.when(j == 0)
    def _():
        h = _rms(x_ref[...], nw_ref[...]).astype(bf16)
        h_scr[...] = h
        glr = lax.dot_general(h, wlr_ref[...].astype(bf16), NT_DIMS, preferred_element_type=f32)
        R = GLA_GATE_RANK
        lane = lax.broadcasted_iota(jnp.int32, glr.shape, 1)
        g = jnp.where(lane < R, glr, 0.0)
        g_hi = g.astype(bf16).astype(f32)
        g3 = (g_hi + pltpu.roll(g_hi, R, 1) + pltpu.roll(g - g_hi, 2 * R, 1)).astype(bf16)
        w = wup_ref[...]
        w_hi = w.astype(bf16).astype(f32)
        w3 = (w_hi + pltpu.roll(w - w_hi, R, 0) + pltpu.roll(w_hi, 2 * R, 0)).astype(bf16)
        pre = jnp.dot(g3, w3, preferred_element_type=f32) + bgk_ref[...]
        log_sig = jnp.minimum(pre, 0.0) - jnp.log1p(jnp.exp(-jnp.abs(pre)))
        loga_ref[...] = log_sig * (1.0 / GLA_GATE_NORM)

    @pl.when(j < N_GLA_TILES)
    def _():
        pg_ref[...] = lax.dot_general(h_scr[...], wg_ref[...], NT_DIMS, preferred_element_type=f32)

    @pl.when(j >= N_GLA_TILES)
    def _():
        res = lax.dot_general(h_scr[...], wa_ref[...], NT_DIMS, preferred_element_type=f32)
        if head_major:
            for g in range(HEADS_PER_TILE):
                pa_ref[g] = res[:, g * LANES:(g + 1) * LANES]
        else:
            pa_ref[...] = res


def _project(x2d, norm_w, w_gla_bf, w_att_bf, w_in0, wup_pad, b_gk, *, tm, head_major):
    n, d = x2d.shape
    nb = n // tm
    nj = N_GLA_TILES + N_ATT_TILES
    gla_j = lambda j: jnp.minimum(j, N_GLA_TILES - 1)
    att_j = lambda j: jnp.maximum(j - N_GLA_TILES, 0)
    assert GLA_COLS % LANES == 0
    in_specs = [
        pl.BlockSpec((tm, d), lambda i, j: (i, 0)),
        pl.BlockSpec((1, d), lambda i, j: (0, 0)),
        pl.BlockSpec((PROJ_TN, d), lambda i, j: (gla_j(j), 0)),
        pl.BlockSpec((PROJ_TN, d), lambda i, j: (att_j(j), 0)),
        pl.BlockSpec((LANES, d), lambda i, j: (GLA_COLS // LANES, 0)),
        pl.BlockSpec((LANES, GLA_KEY_WIDTH), lambda i, j: (0, 0)),
        pl.BlockSpec((1, GLA_KEY_WIDTH), lambda i, j: (0, 0)),
    ]
    out_specs = [
        pl.BlockSpec((tm, PROJ_TN), lambda i, j: (i, gla_j(j))),
        pl.BlockSpec((tm, GLA_KEY_WIDTH), lambda i, j: (i, 0)),
    ]
    out_shape = [jax.ShapeDtypeStruct((n, GLA_COLS), f32), jax.ShapeDtypeStruct((n, GLA_KEY_WIDTH), f32)]
    if head_major:
        out_specs.append(pl.BlockSpec((HEADS_PER_TILE, tm, LANES), lambda i, j: (att_j(j), i, 0)))
        out_shape.append(jax.ShapeDtypeStruct((3 * N_ATT_HEADS, n, ATT_HEAD_DIM), f32))
    else:
        out_specs.append(pl.BlockSpec((tm, PROJ_TN), lambda i, j: (i, att_j(j))))
        out_shape.append(jax.ShapeDtypeStruct((n, ATT_COLS), f32))
    return pl.pallas_call(
        functools.partial(_proj_body, head_major=head_major),
        grid=(nb, nj),
        in_specs=in_specs,
        out_specs=out_specs,
        out_shape=out_shape,
        scratch_shapes=[pltpu.VMEM((tm, d), bf16)],
        compiler_params=_cparams("parallel", "arbitrary"),
        name="norm_in_proj",
    )(x2d, norm_w.reshape(1, d), w_gla_bf, w_att_bf, w_in0, wup_pad, b_gk.reshape(1, GLA_KEY_WIDTH))


def _window_body(kh_ref, vh_ref, ko_ref, vo_ref):
    tr = kh_ref.shape[1]
    for src, dst in ((kh_ref, ko_ref), (vh_ref, vo_ref)):
        for hd in range(N_ATT_HEADS):
            dst[pl.ds(hd, tr, stride=N_ATT_HEADS), :] = src[hd]


def _window_kv(pa_hm, *, batch, seq, window, tr):
    H, E = N_ATT_HEADS, ATT_HEAD_DIM
    assert seq % tr == 0 and window % tr == 0
    per_seq, first = seq // tr, (seq - window) // tr
    nw = window // tr
    src = lambda which: pl.BlockSpec((H, tr, E), lambda b, t: (which, b * per_seq + first + t, 0))
    dst = pl.BlockSpec((tr * H, E), lambda b, t: (b * nw + t, 0))
    shape = jax.ShapeDtypeStruct((batch * window * H, E), f32)
    return pl.pallas_call(
        _window_body,
        grid=(batch, nw),
        in_specs=[src(1), src(2)],
        out_specs=[dst, dst],
        out_shape=[shape, shape],
        compiler_params=_cparams("parallel", "parallel"),
        name="kv_window",
    )(pa_hm, pa_hm)


GLA_SUB = 128


def _gla_body(q_ref, k_ref, v_ref, g_ref, s0_ref, o_ref, sfin_ref, st_scr, bp_scr, *, tb, t_valid):
    C = GLA_CHUNK
    t_blk = pl.program_id(1)

    @pl.when(t_blk == 0)
    def _():
        for h in range(N_GLA_HEADS):
            st_scr[h] = s0_ref[h].T

    U = min(GLA_SUB, tb)
    levels = [w for w in (64, 32, 16, 8, 4, 2, 1) if w < U]
    row5 = lax.broadcasted_iota(jnp.int32, (U, GLA_KEY_WIDTH), 0)
    row1 = lax.broadcasted_iota(jnp.int32, (U, GLA_DK), 0)
    rowa = lax.broadcasted_iota(jnp.int32, (U, U), 0)
    cola = lax.broadcasted_iota(jnp.int32, (U, U), 1)
    ltri = (rowa >= cola).astype(bf16)
    off_diag = rowa != cola
    same_group = {w: ((rowa ^ cola) < 2 * w) & off_diag for w in levels}

    for sb in range(tb // U):
        r0 = sb * U
        g = g_ref[r0:r0 + U, :]
        if t_valid < C:
            g = jnp.where((row5 & (C - 1)) < t_valid, g, 0.0)
        g1 = g.astype(bf16)
        e1 = g - g1.astype(f32)
        g2 = e1.astype(bf16)
        g3 = (e1 - g2.astype(f32)).astype(bf16)
        bp = (jnp.dot(ltri, g1, preferred_element_type=f32) + jnp.dot(ltri, g2, preferred_element_type=f32)
              + jnp.dot(ltri, g3, preferred_element_type=f32))
        bp_scr[...] = bp
        b_last = bp[U - 1:U, :]
        q = q_ref[r0:r0 + U, :] * (GLA_DK ** -0.5)
        k = k_ref[r0:r0 + U, :]
        qe = q * jnp.exp(bp)
        ke = k * jnp.exp(b_last - bp)
        dec = jnp.exp(b_last)
        facs = []
        for w in levels:
            if w >= 4:
                bc = jnp.concatenate([jnp.broadcast_to(bp_scr[s + w - 1:s + w, :], (2 * w, GLA_KEY_WIDTH))
                                      for s in range(0, U, 2 * w)], axis=0)
            elif w == 2:
                pos = row5 & 3
                bc = jnp.where(pos == 0, pltpu.roll(bp, U - 1, 0),
                               jnp.where(pos == 1, bp, jnp.where(pos == 2, pltpu.roll(bp, 1, 0), pltpu.roll(bp, 2, 0))))
            else:
                bc = jnp.where((row5 & 1) == 1, pltpu.roll(bp, 1, 0), bp)
            facs.append(jnp.exp(jnp.where((row5 & w) != 0, bp - bc, bc - bp)))
        for h in range(N_GLA_HEADS):
            ks = slice(h * GLA_DK, (h + 1) * GLA_DK)
            vs = slice(h * GLA_DV, (h + 1) * GLA_DV)
            qh, kh = q[:, ks], k[:, ks]
            a = lax.dot_general(qh.astype(bf16), kh.astype(bf16), NT_DIMS, preferred_element_type=f32)
            a = jnp.where(rowa == cola, a, 0.0)
            for w, fac in zip(levels, facs):
                upper = (row1 & w) != 0
                fh = fac[:, ks]
                rq = jnp.where(upper, qh * fh, 0.0).astype(bf16)
                ck = jnp.where(upper, 0.0, kh * fh).astype(bf16)
                p = lax.dot_general(rq, ck, NT_DIMS, preferred_element_type=f32)
                a = jnp.where(same_group[w], p, a)
            vh = v_ref[r0:r0 + U, vs].astype(bf16)
            st = st_scr[h]
            o = jnp.dot(a.astype(bf16), vh, preferred_element_type=f32)
            o = o + lax.dot_general(qe[:, ks].astype(bf16), st.astype(bf16), NT_DIMS, preferred_element_type=f32)
            o_ref[r0:r0 + U, vs] = o
            kv_t = lax.dot_general(vh, ke[:, ks].astype(bf16), TN_DIMS, preferred_element_type=f32)
            st_scr[h] = dec[:, ks] * st + kv_t

    @pl.when(t_blk == pl.num_programs(1) - 1)
    def _():
        for h in range(N_GLA_HEADS):
            sfin_ref[h] = st_scr[h].T


def _gla(pg, loga, s0, *, batch, t_len, tb, t_valid):
    nt = t_len // tb
    kern = functools.partial(_gla_body, tb=tb, t_valid=t_valid)
    state_spec = pl.BlockSpec((None, N_GLA_HEADS, GLA_DK, GLA_DV), lambda b, t: (b, 0, 0, 0))
    return pl.pallas_call(
        kern,
        grid=(batch, nt),
        in_specs=[
            pl.BlockSpec((tb, GLA_KEY_WIDTH), lambda b, t: (b * nt + t, 0)),
            pl.BlockSpec((tb, GLA_KEY_WIDTH), lambda b, t: (b * nt + t, 1)),
            pl.BlockSpec((tb, GLA_WIDTH), lambda b, t: (b * nt + t, 1)),
            pl.BlockSpec((tb, GLA_KEY_WIDTH), lambda b, t: (b * nt + t, 0)),
            state_spec,
        ],
        out_specs=[pl.BlockSpec((tb, GLA_WIDTH), lambda b, t: (b * nt + t, 0)), state_spec],
        out_shape=[
            jax.ShapeDtypeStruct((batch * t_len, GLA_WIDTH), f32),
            jax.ShapeDtypeStruct((batch, N_GLA_HEADS, GLA_DK, GLA_DV), f32),
        ],
        scratch_shapes=[
            pltpu.VMEM((N_GLA_HEADS, GLA_DV, GLA_DK), f32),
            pltpu.VMEM((min(GLA_SUB, tb), GLA_KEY_WIDTH), f32),
        ],
        compiler_params=_cparams("parallel", "arbitrary"),
        name="gla",
    )(pg, pg, pg, loga, s0)


ATT_ROWS = ATT_BLOCK * max(DILATIONS)


def _att_prompt_body(q_ref, k_ref, v_ref, o_ref, qd_scr, kd1, vd1, kd4, vd4, kd16, vd16, od_scr, lse_scr):
    U = ATT_BLOCK
    h = pl.program_id(1)
    n = pl.program_id(2)
    kds, vds = (kd1, kd4, kd16), (vd1, vd4, vd16)

    @pl.when(n == 0)
    def _():
        for d, kd, vd in zip(DILATIONS, kds, vds):
            L = ATT_ROWS // d
            for r in range(d):
                kd[r * (U + L):r * (U + L) + U, :] = jnp.zeros((U, ATT_HEAD_DIM), bf16)
                vd[r * (U + L):r * (U + L) + U, :] = jnp.zeros((U, ATT_HEAD_DIM), bf16)

    qi = lax.broadcasted_iota(jnp.int32, (U, 2 * U), 0)
    ki = lax.broadcasted_iota(jnp.int32, (U, 2 * U), 1)
    dist = qi - ki + U
    in_window = (dist >= 0) & (dist <= SUB_WINDOW)
    slope = jnp.exp2(jnp.zeros((U, 2 * U), f32) - (h + 1).astype(f32) * (8.0 / N_ATT_HEADS))
    first_pen = jnp.where(n == 0, NEG_INF, 0.0)

    for di, d in enumerate(DILATIONS):
        L = ATT_ROWS // d
        nu = L // U
        kd, vd = kds[di], vds[di]
        bias = jnp.where(in_window, -(slope * float(d)) * dist.astype(f32), NEG_INF)
        for r in range(d):
            src = pl.ds(r, L, stride=d) if d > 1 else slice(None)
            base = r * (U + L)
            qd_scr[r * L:(r + 1) * L, :] = (q_ref[src, :] * ATT_SCALE).astype(bf16)
            kd[base + U:base + U + L, :] = k_ref[src, :].astype(bf16)
            vd[base + U:base + U + L, :] = v_ref[src, :].astype(bf16)

        bias_first = bias + jnp.where(ki < U, first_pen, 0.0)
        for r in range(d):
            for u in range(nu):
                qoff = r * L + u * U
                koff = r * (U + L) + u * U
                qq = qd_scr[qoff:qoff + U, :]
                kk = kd[koff:koff + 2 * U, :]
                vv = vd[koff:koff + 2 * U, :]
                s = lax.dot_general(qq, kk, NT_DIMS, preferred_element_type=f32) + (bias_first if u == 0 else bias)
                m = jnp.max(s, axis=-1, keepdims=True)
                p = jnp.exp(s - m)
                l = jnp.sum(p, axis=-1, keepdims=True)
                o = jnp.dot(p.astype(bf16), vv, preferred_element_type=f32) / l
                rows = pl.ds(u * (U * d) + r, U, stride=d) if d > 1 else slice(u * U, (u + 1) * U)
                od_scr[di, rows, :] = o
                lse_scr[di, rows, :] = jnp.broadcast_to(m + jnp.log(l), (U, ATT_HEAD_DIM))
        for r in range(d):
            base = r * (U + L)
            kd[base:base + U, :] = kd[base + L:base + L + U, :]
            vd[base:base + U, :] = vd[base + L:base + L + U, :]

    CH = 512
    for c in range(ATT_ROWS // CH):
        rs = slice(c * CH, (c + 1) * CH)
        ls = [lse_scr[di, rs, :] for di in range(len(DILATIONS))]
        mm = jnp.maximum(jnp.maximum(ls[0], ls[1]), ls[2])
        ws = [jnp.exp(x - mm) for x in ls]
        num = ws[0] * od_scr[0, rs, :] + ws[1] * od_scr[1, rs, :] + ws[2] * od_scr[2, rs, :]
        o_ref[rs, :] = num / (ws[0] + ws[1] + ws[2])


def _att_prompt(pa_hm, *, batch, seq):
    nb = seq // ATT_ROWS
    H, E, U = N_ATT_HEADS, ATT_HEAD_DIM, ATT_BLOCK
    blk = (None, ATT_ROWS, E)
    kv_scratch = []
    for d in DILATIONS:
        kv_scratch += [pltpu.VMEM((d * U + ATT_ROWS, E), bf16)] * 2
    return pl.pallas_call(
        _att_prompt_body,
        grid=(batch, H, nb),
        in_specs=[
            pl.BlockSpec(blk, lambda b, h, n: (h, b * nb + n, 0)),
            pl.BlockSpec(blk, lambda b, h, n: (H + h, b * nb + n, 0)),
            pl.BlockSpec(blk, lambda b, h, n: (2 * H + h, b * nb + n, 0)),
        ],
        out_specs=pl.BlockSpec((ATT_ROWS, E), lambda b, h, n: (b * nb + n, h)),
        out_shape=jax.ShapeDtypeStruct((batch * seq, H * E), f32),
        scratch_shapes=[pltpu.VMEM((ATT_ROWS, E), bf16)] + kv_scratch + [
            pltpu.VMEM((len(DILATIONS), ATT_ROWS, E), f32),
            pltpu.VMEM((len(DILATIONS), ATT_ROWS, E), f32),
        ],
        compiler_params=_cparams("parallel", "parallel", "arbitrary"),
        name="att_prompt",
    )(pa_hm, pa_hm, pa_hm)


SAMPLE_PAD = 16


def _att_sample_body(qkv_ref, k1_ref, v1_ref, k4_ref, v4_ref, k16_ref, v16_ref, o_ref, *, t_new):
    J = SUB_WINDOW
    hidx = lax.broadcasted_iota(jnp.int32, (N_ATT_HEADS, 1), 0)
    slope = jnp.exp2((hidx + 1).astype(f32) * (-8.0 / N_ATT_HEADS))
    slot = lax.broadcasted_iota(jnp.int32, (J, N_ATT_HEADS, 1), 0)
    q = [qkv_ref[i, 0] * ATT_SCALE for i in range(t_new)]
    kn = [qkv_ref[i, 1] for i in range(t_new)]
    vn = [qkv_ref[i, 2] for i in range(t_new)]
    far_bias = {dil: (slope * float(dil))[None] * (J - slot).astype(f32) for dil in DILATIONS if dil > 1}
    for i in range(t_new):
        s_new = [jnp.sum(q[i] * kn[n], axis=-1, keepdims=True) for n in range(i + 1)]
        outs, lses = [], []
        for dil in DILATIONS:
            if dil == 1:
                kc, vc = k1_ref[...], v1_ref[...]
                bias = slope[None] * (J + i - slot).astype(f32)
                news = [(s_new[n] - slope * float(i - n), vn[n]) for n in range(i + 1)]
            else:
                kres_ref, vres_ref = (k4_ref, v4_ref) if dil == 4 else (k16_ref, v16_ref)
                kc, vc = kres_ref[:, i], vres_ref[:, i]
                bias = far_bias[dil]
                news = [(s_new[i], vn[i])]
            s = jnp.sum(kc * q[i][None], axis=-1, keepdims=True) - bias
            if dil == 1:
                s = jnp.where(slot >= i, s, NEG_INF)
            m = jnp.max(s, axis=0)
            for sn, _ in news:
                m = jnp.maximum(m, sn)
            p = jnp.exp(s - m[None])
            l = jnp.sum(p, axis=0)
            acc = jnp.sum(p * vc, axis=0)
            for sn, v in news:
                pn = jnp.exp(sn - m)
                l = l + pn
                acc = acc + pn * v
            outs.append(acc / l)
            lses.append(m + jnp.log(l))
        mm = jnp.maximum(jnp.maximum(lses[0], lses[1]), lses[2])
        ws = [jnp.exp(x - mm) for x in lses]
        tot = ws[0] + ws[1] + ws[2]
        o_ref[i] = (ws[0] * outs[0] + ws[1] * outs[1] + ws[2] * outs[2]) / tot


def _att_sample(qkv_s, cache_k, cache_v, *, batch, t_new):
    w_cache = cache_k.shape[1]
    J = SUB_WINDOW
    he = (N_ATT_HEADS, ATT_HEAD_DIM)
    assert w_cache == max(DILATIONS) * J and t_new <= 4
    specs, views = [], []
    for dil in DILATIONS:
        rows = w_cache // dil
        last = rows // J - 1
        for c in (cache_k, cache_v):
            if dil == 1:
                views.append(c)
                specs.append(pl.BlockSpec((None, J) + he, lambda b, last=last: (b, last, 0, 0)))
            else:
                views.append(c.reshape((batch, rows, dil) + he))
                specs.append(pl.BlockSpec((None, J, 4) + he, lambda b, last=last: (b, last, 0, 0, 0)))
    kern = functools.partial(_att_sample_body, t_new=t_new)
    return pl.pallas_call(
        kern,
        grid=(batch,),
        in_specs=[pl.BlockSpec((None, t_new, 3) + he, lambda b: (b, 0, 0, 0, 0))] + specs,
        out_specs=pl.BlockSpec((None, t_new) + he, lambda b: (b, 0, 0, 0)),
        out_shape=jax.ShapeDtypeStruct((batch, t_new) + he, f32),
        compiler_params=_cparams("parallel"),
        name="att_sample",
    )(qkv_s, *views)


def _outproj_body(x_ref, go_ref, gg_ref, att_ref, gnw_ref, anw_ref, wg_ref, wa_ref, out_ref):
    parts = []
    for h in range(N_GLA_HEADS):
        vs = slice(h * GLA_DV, (h + 1) * GLA_DV)
        y = _rms(go_ref[:, vs], gnw_ref[...])
        gate = gg_ref[:, vs]
        parts.append((y * (gate * jax.nn.sigmoid(gate))).astype(bf16))
    gla_part = jnp.concatenate(parts, axis=-1)
    att_part = _rms(att_ref[...], anw_ref[...]).astype(bf16)
    mix = jnp.dot(gla_part, wg_ref[...], preferred_element_type=f32)
    mix = mix + jnp.dot(att_part, wa_ref[...], preferred_element_type=f32)
    out_ref[...] = x_ref[...] + mix


def _outproj(x, gla_o, gg_src, gg_col, att_o, gla_norm_w, att_norm_w, w_out_bf, tm):
    n = x.shape[0]
    row = lambda i: (i, 0)
    const = lambda i: (0, 0)
    in_specs = [
        pl.BlockSpec((tm, D_MODEL), row),
        pl.BlockSpec((tm, GLA_WIDTH), row),
        pl.BlockSpec((tm, GLA_WIDTH), lambda i: (i, gg_col)),
        pl.BlockSpec((tm, ATT_WIDTH), row),
        pl.BlockSpec((1, GLA_DV), const),
        pl.BlockSpec((1, ATT_WIDTH), const),
        pl.BlockSpec((GLA_WIDTH, D_MODEL), lambda i: (0, 0)),
        pl.BlockSpec((ATT_WIDTH, D_MODEL), lambda i: (1, 0)),
    ]
    return pl.pallas_call(
        _outproj_body,
        grid=(n // tm,),
        in_specs=in_specs,
        out_specs=pl.BlockSpec((tm, D_MODEL), row),
        out_shape=jax.ShapeDtypeStruct((n, D_MODEL), f32),
        compiler_params=_cparams("parallel"),
        name="mixer_out_proj",
    )(x, gla_o, gg_src, att_o, gla_norm_w.reshape(1, GLA_DV), att_norm_w.reshape(1, ATT_WIDTH), w_out_bf, w_out_bf)


def _mlp_body(x_ref, fw_ref, wu_ref, wd_ref, nw_ref, o_ref, *rest, emit_bf16):
    h_scr = rest[-1]
    f = pl.program_id(1)

    @pl.when(f == 0)
    def _():
        h_scr[...] = _rms(x_ref[...], fw_ref[...]).astype(bf16)
        o_ref[...] = jnp.zeros_like(o_ref)

    wu, wd = wu_ref[...], wd_ref[...]
    if emit_bf16:
        wu, wd = wu.astype(bf16), wd.astype(bf16)
        rest[0][...] = wu
        rest[1][...] = wd
    u = jnp.dot(h_scr[...], wu, preferred_element_type=f32)
    a = jnp.square(jnp.maximum(u, 0.0)).astype(bf16)
    o_ref[...] += jnp.dot(a, wd, preferred_element_type=f32)

    @pl.when(f == pl.num_programs(1) - 1)
    def _():
        o_ref[...] = _rms(x_ref[...] + o_ref[...], nw_ref[...])


def _mlp(x, ffn_norm_w, w_up, w_down, final_norm_w, tm, tf, emit_bf16=False):
    n = x.shape[0]
    assert not emit_bf16 or n == tm
    out_specs = [pl.BlockSpec((tm, D_MODEL), lambda i, f: (i, 0))]
    out_shape = [jax.ShapeDtypeStruct((n, D_MODEL), f32)]
    if emit_bf16:
        out_specs += [pl.BlockSpec((D_MODEL, tf), lambda i, f: (0, f)), pl.BlockSpec((tf, D_MODEL), lambda i, f: (f, 0))]
        out_shape += [jax.ShapeDtypeStruct((D_MODEL, D_FF), bf16), jax.ShapeDtypeStruct((D_FF, D_MODEL), bf16)]
    res = pl.pallas_call(
        functools.partial(_mlp_body, emit_bf16=emit_bf16),
        grid=(n // tm, D_FF // tf),
        in_specs=[
            pl.BlockSpec((tm, D_MODEL), lambda i, f: (i, 0)),
            pl.BlockSpec((1, D_MODEL), lambda i, f: (0, 0)),
            pl.BlockSpec((D_MODEL, tf), lambda i, f: (0, f)),
            pl.BlockSpec((tf, D_MODEL), lambda i, f: (f, 0)),
            pl.BlockSpec((1, D_MODEL), lambda i, f: (0, 0)),
        ],
        out_specs=out_specs,
        out_shape=out_shape,
        scratch_shapes=[pltpu.VMEM((tm, D_MODEL), bf16)],
        compiler_params=_cparams("parallel", "arbitrary"),
        name="mlp_final_norm",
    )(x, ffn_norm_w.reshape(1, D_MODEL), w_up, w_down, final_norm_w.reshape(1, D_MODEL))
    return res if emit_bf16 else res[0]


def kernel(x_prompt, x_sample, cache_k_win, cache_v_win, state_gla, attn_norm_w, w_in, w_gk_up, b_gk, gla_norm_w,
           att_out_norm_w, w_out, ffn_norm_w, w_up, w_down, final_norm_w):
    depth = w_in.shape[0]
    assert depth == 1, "single trunk layer"
    B, S, _ = x_prompt.shape
    Bs, Ts, _ = x_sample.shape
    w_p = min(MAX_WINDOW, S)
    assert S % (ATT_BLOCK * max(DILATIONS)) == 0 and Ts <= GLA_CHUNK

    w_in0 = jnp.swapaxes(w_in[0], 0, 1)
    w_gla_bf, w_att_bf = _split_w_in(w_in0, 256)
    wup_pad = jnp.pad(w_gk_up[0], ((0, LANES - GLA_GATE_RANK), (0, 0)))
    w_out_bf = w_out[0].astype(bf16)

    xp = x_prompt.reshape(B * S, D_MODEL)
    pg, loga, pa = _project(xp, attn_norm_w[0], w_gla_bf, w_att_bf, w_in0, wup_pad, b_gk[0], tm=1024, head_major=True)
    k_win, v_win = _window_kv(pa, batch=B, seq=S, window=w_p, tr=512)
    s0 = jnp.zeros((B, N_GLA_HEADS, GLA_DK, GLA_DV), f32)
    gla_o, gla_state_p = _gla(pg, loga, s0, batch=B, t_len=S, tb=512, t_valid=GLA_CHUNK)
    att = _att_prompt(pa, batch=B, seq=S)
    x1 = _outproj(xp, gla_o, pg, 2, att, gla_norm_w[0], att_out_norm_w[0], w_out_bf, 512)

    P = SAMPLE_PAD
    xs_pad = jnp.pad(x_sample, ((0, 0), (0, P - Ts), (0, 0))).reshape(Bs * P, D_MODEL)
    pgs, logas, pas = _project(xs_pad, attn_norm_w[0], w_gla_bf, w_att_bf, w_in0, wup_pad, b_gk[0], tm=Bs * P,
                               head_major=False)
    qkv_s = pas.reshape(Bs, P, 3, N_ATT_HEADS, ATT_HEAD_DIM)[:, :Ts]
    gla_o_s, gla_state_s = _gla(pgs, logas, state_gla[0], batch=Bs, t_len=P, tb=P, t_valid=Ts)
    att_s = _att_sample(qkv_s, cache_k_win[0], cache_v_win[0], batch=Bs, t_new=Ts).reshape(Bs * Ts, ATT_WIDTH)
    gla_o_s = gla_o_s.reshape(Bs, P, GLA_WIDTH)[:, :Ts].reshape(Bs * Ts, GLA_WIDTH)
    gg_s = pgs.reshape(Bs, P, GLA_COLS)[:, :Ts, 2 * GLA_KEY_WIDTH + GLA_WIDTH:].reshape(Bs * Ts, GLA_WIDTH)
    xs = x_sample.reshape(Bs * Ts, D_MODEL)
    x1s = _outproj(xs, gla_o_s, gg_s, 0, att_s, gla_norm_w[0], att_out_norm_w[0], w_out_bf, Bs * Ts)
    y_sample, w_up_bf, w_down_bf = _mlp(x1s, ffn_norm_w[0], w_up[0], w_down[0], final_norm_w, Bs * Ts, 512,
                                        emit_bf16=True)
    y_sample = y_sample.reshape(Bs, Ts, D_MODEL)
    y_prompt = _mlp(x1, ffn_norm_w[0], w_up_bf, w_down_bf, final_norm_w, 1024, 512).reshape(B, S, D_MODEL)

    k_win_prompt = k_win.reshape(1, B, w_p, N_ATT_HEADS, ATT_HEAD_DIM)
    v_win_prompt = v_win.reshape(1, B, w_p, N_ATT_HEADS, ATT_HEAD_DIM)
    k_new_sample = qkv_s[:, :, 1][None]
    v_new_sample = qkv_s[:, :, 2][None]
    return (y_prompt, y_sample, k_win_prompt, v_win_prompt, gla_state_p[None], k_new_sample, v_new_sample,
            gla_state_s[None])
```

```python
import functools

import jax
import jax.numpy as jnp
from jax import lax
from jax.experimental import pallas as pl
from jax.experimental.pallas import tpu as pltpu

f32 = jnp.float32
bf16 = jnp.bfloat16

D_MODEL = 2048
N_GLA_HEADS = 4
GLA_DK = 128
GLA_DV = 256
GLA_KEY_WIDTH = N_GLA_HEADS * GLA_DK
GLA_WIDTH = N_GLA_HEADS * GLA_DV
GLA_GATE_RANK = 16
GLA_GATE_NORM = 16.0
GLA_CHUNK = 16
N_ATT_HEADS = 8
ATT_HEAD_DIM = 128
ATT_WIDTH = N_ATT_HEADS * ATT_HEAD_DIM
DILATIONS = (1, 4, 16)
SUB_WINDOW = 128
ATT_BLOCK = 128
MAX_WINDOW = 2048
D_FF = 4 * D_MODEL
RMS_EPS = 1e-6
GLA_COLS = 2 * GLA_KEY_WIDTH + 2 * GLA_WIDTH
ATT_COLS = 3 * ATT_WIDTH
LANES = 128
VMEM_LIMIT = 56 * 1024 * 1024
NEG_INF = float("-inf")
ALIBI_SLOPES = tuple(2.0 ** (-8.0 * (h + 1) / N_ATT_HEADS) for h in range(N_ATT_HEADS))
ATT_SCALE = ATT_HEAD_DIM ** -0.5

NT_DIMS = (((1,), (1,)), ((), ()))
TN_DIMS = (((0,), (0,)), ((), ()))


def _cparams(*sem):
    return pltpu.CompilerParams(dimension_semantics=sem, vmem_limit_bytes=VMEM_LIMIT)


def _rms(x, w):
    r = lax.rsqrt(jnp.mean(x * x, axis=-1, keepdims=True) + RMS_EPS)
    return (x * r) * w


def _split_w_in_body(w_ref, g_ref, a_ref):
    g_ref[...] = w_ref[:GLA_COLS, :].astype(bf16)
    a_ref[...] = w_ref[GLA_COLS + GLA_GATE_RANK:, :].astype(bf16)


def _split_w_in(w_in_t, cols):
    rows, d = w_in_t.shape
    assert rows == GLA_COLS + GLA_GATE_RANK + ATT_COLS and d % cols == 0
    return pl.pallas_call(
        _split_w_in_body,
        grid=(d // cols,),
        in_specs=[pl.BlockSpec((rows, cols), lambda i: (0, i))],
        out_specs=[pl.BlockSpec((GLA_COLS, cols), lambda i: (0, i)), pl.BlockSpec((ATT_COLS, cols), lambda i: (0, i))],
        out_shape=[jax.ShapeDtypeStruct((GLA_COLS, d), bf16), jax.ShapeDtypeStruct((ATT_COLS, d), bf16)],
        compiler_params=_cparams("parallel"),
        name="split_w_in",
    )(w_in_t)


PROJ_TN = 512
N_GLA_TILES = GLA_COLS // PROJ_TN
N_ATT_TILES = ATT_COLS // PROJ_TN
TILES_PER_QKV = ATT_WIDTH // PROJ_TN
HEADS_PER_TILE = PROJ_TN // ATT_HEAD_DIM


def _proj_body(x_ref, nw_ref, wg_ref, wa_ref, wlr_ref, wup_ref, bgk_ref, pg_ref, loga_ref, pa_ref, h_scr, *,
               head_major):
    j = pl.program_id(1)

    @pl.when(j == 0)
    def _():
        h = _rms(x_ref[...], nw_ref[...]).astype(bf16)
        h_scr[...] = h
        glr = lax.dot_general(h, wlr_ref[...].astype(bf16), NT_DIMS, preferred_element_type=f32)
        R = GLA_GATE_RANK
        lane = lax.broadcasted_iota(jnp.int32, glr.shape, 1)
        g = jnp.where(lane < R, glr, 0.0)
        g_hi = g.astype(bf16).astype(f32)
        g3 = (g_hi + pltpu.roll(g_hi, R, 1) + pltpu.roll(g - g_hi, 2 * R, 1)).astype(bf16)
        w = wup_ref[...]
        w_hi = w.astype(bf16).astype(f32)
        w3 = (w_hi + pltpu.roll(w - w_hi, R, 0) + pltpu.roll(w_hi, 2 * R, 0)).astype(bf16)
        pre = jnp.dot(g3, w3, preferred_element_type=f32) + bgk_ref[...]
        log_sig = jnp.minimum(pre, 0.0) - jnp.log1p(jnp.exp(-jnp.abs(pre)))
        loga_ref[...] = log_sig * (1.0 / GLA_GATE_NORM)

    @pl.when(j < N_GLA_TILES)
    def _():
        pg_ref[...] = lax.dot_general(h_scr[...], wg_ref[...], NT_DIMS, preferred_element_type=f32)

    @pl.when(j >= N_GLA_TILES)
    def _():
        res = lax.dot_general(h_scr[...], wa_ref[...], NT_DIMS, preferred_element_type=f32)
        if head_major:
            for g in range(HEADS_PER_TILE):
                pa_ref[g] = res[:, g * LANES:(g + 1) * LANES]
        else:
            pa_ref[...] = res


def _project(x2d, norm_w, w_gla_bf, w_att_bf, w_in0, wup_pad, b_gk, *, tm, head_major):
    n, d = x2d.shape
    nb = n // tm
    nj = N_GLA_TILES + N_ATT_TILES
    gla_j = lambda j: jnp.minimum(j, N_GLA_TILES - 1)
    att_j = lambda j: jnp.maximum(j - N_GLA_TILES, 0)
    assert GLA_COLS % LANES == 0
    in_specs = [
        pl.BlockSpec((tm, d), lambda i, j: (i, 0)),
        pl.BlockSpec((1, d), lambda i, j: (0, 0)),
        pl.BlockSpec((PROJ_TN, d), lambda i, j: (gla_j(j), 0)),
        pl.BlockSpec((PROJ_TN, d), lambda i, j: (att_j(j), 0)),
        pl.BlockSpec((LANES, d), lambda i, j: (GLA_COLS // LANES, 0)),
        pl.BlockSpec((LANES, GLA_KEY_WIDTH), lambda i, j: (0, 0)),
        pl.BlockSpec((1, GLA_KEY_WIDTH), lambda i, j: (0, 0)),
    ]
    out_specs = [
        pl.BlockSpec((tm, PROJ_TN), lambda i, j: (i, gla_j(j))),
        pl.BlockSpec((tm, GLA_KEY_WIDTH), lambda i, j: (i, 0)),
    ]
    out_shape = [jax.ShapeDtypeStruct((n, GLA_COLS), f32), jax.ShapeDtypeStruct((n, GLA_KEY_WIDTH), f32)]
    if head_major:
        out_specs.append(pl.BlockSpec((HEADS_PER_TILE, tm, LANES), lambda i, j: (att_j(j), i, 0)))
        out_shape.append(jax.ShapeDtypeStruct((3 * N_ATT_HEADS, n, ATT_HEAD_DIM), f32))
    else:
        out_specs.append(pl.BlockSpec((tm, PROJ_TN), lambda i, j: (i, att_j(j))))
        out_shape.append(jax.ShapeDtypeStruct((n, ATT_COLS), f32))
    return pl.pallas_call(
        functools.partial(_proj_body, head_major=head_major),
        grid=(nb, nj),
        in_specs=in_specs,
        out_specs=out_specs,
        out_shape=out_shape,
        scratch_shapes=[pltpu.VMEM((tm, d), bf16)],
        compiler_params=_cparams("parallel", "arbitrary"),
        name="norm_in_proj",
    )(x2d, norm_w.reshape(1, d), w_gla_bf, w_att_bf, w_in0, wup_pad, b_gk.reshape(1, GLA_KEY_WIDTH))


def _window_body(kh_ref, vh_ref, ko_ref, vo_ref):
    tr = kh_ref.shape[1]
    for src, dst in ((kh_ref, ko_ref), (vh_ref, vo_ref)):
        for hd in range(N_ATT_HEADS):
            dst[pl.ds(hd, tr, stride=N_ATT_HEADS), :] = src[hd]


def _window_kv(pa_hm, *, batch, seq, window, tr):
    H, E = N_ATT_HEADS, ATT_HEAD_DIM
    assert seq % tr == 0 and window % tr == 0
    per_seq, first = seq // tr, (seq - window) // tr
    nw = window // tr
    src = lambda which: pl.BlockSpec((H, tr, E), lambda b, t: (which, b * per_seq + first + t, 0))
    dst = pl.BlockSpec((tr * H, E), lambda b, t: (b * nw + t, 0))
    shape = jax.ShapeDtypeStruct((batch * window * H, E), f32)
    return pl.pallas_call(
        _window_body,
        grid=(batch, nw),
        in_specs=[src(1), src(2)],
        out_specs=[dst, dst],
        out_shape=[shape, shape],
        compiler_params=_cparams("parallel", "parallel"),
        name="kv_window",
    )(pa_hm, pa_hm)


GLA_SUB = 128


def _gla_body(q_ref, k_ref, v_ref, g_ref, s0_ref, o_ref, sfin_ref, st_scr, bp_scr, *, tb, t_valid):
    C = GLA_CHUNK
    t_blk = pl.program_id(1)

    @pl.when(t_blk == 0)
    def _():
        for h in range(N_GLA_HEADS):
            st_scr[h] = s0_ref[h].T

    U = min(GLA_SUB, tb)
    levels = [w for w in (64, 32, 16, 8, 4, 2, 1) if w < U]
    row5 = lax.broadcasted_iota(jnp.int32, (U, GLA_KEY_WIDTH), 0)
    row1 = lax.broadcasted_iota(jnp.int32, (U, GLA_DK), 0)
    rowa = lax.broadcasted_iota(jnp.int32, (U, U), 0)
    cola = lax.broadcasted_iota(jnp.int32, (U, U), 1)
    ltri = (rowa >= cola).astype(bf16)
    off_diag = rowa != cola
    same_group = {w: ((rowa ^ cola) < 2 * w) & off_diag for w in levels}

    for sb in range(tb // U):
        r0 = sb * U
        g = g_ref[r0:r0 + U, :]
        if t_valid < C:
            g = jnp.where((row5 & (C - 1)) < t_valid, g, 0.0)
        g1 = g.astype(bf16)
        e1 = g - g1.astype(f32)
        g2 = e1.astype(bf16)
        g3 = (e1 - g2.astype(f32)).astype(bf16)
        bp = (jnp.dot(ltri, g1, preferred_element_type=f32) + jnp.dot(ltri, g2, preferred_element_type=f32)
              + jnp.dot(ltri, g3, preferred_element_type=f32))
        bp_scr[...] = bp
        b_last = bp[U - 1:U, :]
        q = q_ref[r0:r0 + U, :] * (GLA_DK ** -0.5)
        k = k_ref[r0:r0 + U, :]
        qe = q * jnp.exp(bp)
        ke = k * jnp.exp(b_last - bp)
        dec = jnp.exp(b_last)
        facs = []
        for w in levels:
            if w >= 4:
                bc = jnp.concatenate([jnp.broadcast_to(bp_scr[s + w - 1:s + w, :], (2 * w, GLA_KEY_WIDTH))
                                      for s in range(0, U, 2 * w)], axis=0)
            elif w == 2:
                pos = row5 & 3
                bc = jnp.where(pos == 0, pltpu.roll(bp, U - 1, 0),
                               jnp.where(pos == 1, bp, jnp.where(pos == 2, pltpu.roll(bp, 1, 0), pltpu.roll(bp, 2, 0))))
            else:
                bc = jnp.where((row5 & 1) == 1, pltpu.roll(bp, 1, 0), bp)
            facs.append(jnp.exp(jnp.where((row5 & w) != 0, bp - bc, bc - bp)))
        for h in range(N_GLA_HEADS):
            ks = slice(h * GLA_DK, (h + 1) * GLA_DK)
            vs = slice(h * GLA_DV, (h + 1) * GLA_DV)
            qh, kh = q[:, ks], k[:, ks]
            a = lax.dot_general(qh.astype(bf16), kh.astype(bf16), NT_DIMS, preferred_element_type=f32)
            a = jnp.where(rowa == cola, a, 0.0)
            for w, fac in zip(levels, facs):
                upper = (row1 & w) != 0
                fh = fac[:, ks]
                rq = jnp.where(upper, qh * fh, 0.0).astype(bf16)
                ck = jnp.where(upper, 0.0, kh * fh).astype(bf16)
                p = lax.dot_general(rq, ck, NT_DIMS, preferred_element_type=f32)
                a = jnp.where(same_group[w], p, a)
            vh = v_ref[r0:r0 + U, vs].astype(bf16)
            st = st_scr[h]
            o = jnp.dot(a.astype(bf16), vh, preferred_element_type=f32)
            o = o + lax.dot_general(qe[:, ks].astype(bf16), st.astype(bf16), NT_DIMS, preferred_element_type=f32)
            o_ref[r0:r0 + U, vs] = o
            kv_t = lax.dot_general(vh, ke[:, ks].astype(bf16), TN_DIMS, preferred_element_type=f32)
            st_scr[h] = dec[:, ks] * st + kv_t

    @pl.when(t_blk == pl.num_programs(1) - 1)
    def _():
        for h in range(N_GLA_HEADS):
            sfin_ref[h] = st_scr[h].T


def _gla(pg, loga, s0, *, batch, t_len, tb, t_valid):
    nt = t_len // tb
    kern = functools.partial(_gla_body, tb=tb, t_valid=t_valid)
    state_spec = pl.BlockSpec((None, N_GLA_HEADS, GLA_DK, GLA_DV), lambda b, t: (b, 0, 0, 0))
    return pl.pallas_call(
        kern,
        grid=(batch, nt),
        in_specs=[
            pl.BlockSpec((tb, GLA_KEY_WIDTH), lambda b, t: (b * nt + t, 0)),
            pl.BlockSpec((tb, GLA_KEY_WIDTH), lambda b, t: (b * nt + t, 1)),
            pl.BlockSpec((tb, GLA_WIDTH), lambda b, t: (b * nt + t, 1)),
            pl.BlockSpec((tb, GLA_KEY_WIDTH), lambda b, t: (b * nt + t, 0)),
            state_spec,
        ],
        out_specs=[pl.BlockSpec((tb, GLA_WIDTH), lambda b, t: (b * nt + t, 0)), state_spec],
        out_shape=[
            jax.ShapeDtypeStruct((batch * t_len, GLA_WIDTH), f32),
            jax.ShapeDtypeStruct((batch, N_GLA_HEADS, GLA_DK, GLA_DV), f32),
        ],
        scratch_shapes=[
            pltpu.VMEM((N_GLA_HEADS, GLA_DV, GLA_DK), f32),
            pltpu.VMEM((min(GLA_SUB, tb), GLA_KEY_WIDTH), f32),
        ],
        compiler_params=_cparams("parallel", "arbitrary"),
        name="gla",
    )(pg, pg, pg, loga, s0)


ATT_ROWS = ATT_BLOCK * max(DILATIONS)


assert DILATIONS == (1, 4, 16)


def _att_prompt_body(q_ref, k_ref, v_ref, o_ref, qd_scr, kd1, vd1, kd4, vd4, kd16, vd16, od_scr, lse_scr, by4_scr):
    U = ATT_BLOCK
    h = pl.program_id(1)
    n = pl.program_id(2)
    kds, vds = (kd1, kd4, kd16), (vd1, vd4, vd16)

    @pl.when(n == 0)
    def _():
        for d, kd, vd in zip(DILATIONS, kds, vds):
            L = ATT_ROWS // d
            for r in range(d):
                kd[r * (U + L):r * (U + L) + U, :] = jnp.zeros((U, ATT_HEAD_DIM), bf16)
                vd[r * (U + L):r * (U + L) + U, :] = jnp.zeros((U, ATT_HEAD_DIM), bf16)

    qi = lax.broadcasted_iota(jnp.int32, (U, 2 * U), 0)
    ki = lax.broadcasted_iota(jnp.int32, (U, 2 * U), 1)
    dist = qi - ki + U
    in_window = (dist >= 0) & (dist <= SUB_WINDOW)
    slope = jnp.exp2(jnp.zeros((U, 2 * U), f32) - (h + 1).astype(f32) * (8.0 / N_ATT_HEADS))
    first_pen = jnp.where(n == 0, NEG_INF, 0.0)

    for di, d in enumerate(DILATIONS):
        L = ATT_ROWS // d
        nu = L // U
        kd, vd = kds[di], vds[di]
        bias = jnp.where(in_window, -(slope * float(d)) * dist.astype(f32), NEG_INF)
        for r in range(d):
            base = r * (U + L)
            for src_ref, by4, dst, off, scale in ((q_ref, by4_scr.at[0], qd_scr, r * L, ATT_SCALE),
                                                  (k_ref, by4_scr.at[1], kd, base + U, None),
                                                  (v_ref, by4_scr.at[2], vd, base + U, None)):
                if d == 1:
                    x = src_ref[...]
                elif d == 4:
                    x = src_ref[pl.ds(r, L, stride=4), :]
                    by4[r * L:(r + 1) * L, :] = x
                else:
                    x = by4[pl.ds((r % 4) * (ATT_ROWS // 4) + r // 4, L, stride=4), :]
                if scale is not None:
                    x = x * scale
                dst[off:off + L, :] = x.astype(bf16)

        bias_first = bias + jnp.where(ki < U, first_pen, 0.0)
        for r in range(d):
            for u in range(nu):
                qoff = r * L + u * U
                koff = r * (U + L) + u * U
                qq = qd_scr[qoff:qoff + U, :]
                kk = kd[koff:koff + 2 * U, :]
                vv = vd[koff:koff + 2 * U, :]
                s = lax.dot_general(qq, kk, NT_DIMS, preferred_element_type=f32) + (bias_first if u == 0 else bias)
                m = jnp.max(s, axis=-1, keepdims=True)
                p = jnp.exp(s - m)
                l = jnp.sum(p, axis=-1, keepdims=True)
                o = jnp.dot(p.astype(bf16), vv, preferred_element_type=f32) / l
                rows = pl.ds(u * (U * d) + r, U, stride=d) if d > 1 else slice(u * U, (u + 1) * U)
                od_scr[di, rows, :] = o
                lse_scr[di, rows, :] = jnp.broadcast_to(m + jnp.log(l), (U, ATT_HEAD_DIM))
        for r in range(d):
            base = r * (U + L)
            kd[base:base + U, :] = kd[base + L:base + L + U, :]
            vd[base:base + U, :] = vd[base + L:base + L + U, :]

    CH = 512
    for c in range(ATT_ROWS // CH):
        rs = slice(c * CH, (c + 1) * CH)
        ls = [lse_scr[di, rs, :] for di in range(len(DILATIONS))]
        mm = jnp.maximum(jnp.maximum(ls[0], ls[1]), ls[2])
        ws = [jnp.exp(x - mm) for x in ls]
        num = ws[0] * od_scr[0, rs, :] + ws[1] * od_scr[1, rs, :] + ws[2] * od_scr[2, rs, :]
        o_ref[rs, :] = num / (ws[0] + ws[1] + ws[2])


def _att_prompt(pa_hm, *, batch, seq):
    nb = seq // ATT_ROWS
    H, E, U = N_ATT_HEADS, ATT_HEAD_DIM, ATT_BLOCK
    blk = (None, ATT_ROWS, E)
    kv_scratch = []
    for d in DILATIONS:
        kv_scratch += [pltpu.VMEM((d * U + ATT_ROWS, E), bf16)] * 2
    return pl.pallas_call(
        _att_prompt_body,
        grid=(batch, H, nb),
        in_specs=[
            pl.BlockSpec(blk, lambda b, h, n: (h, b * nb + n, 0)),
            pl.BlockSpec(blk, lambda b, h, n: (H + h, b * nb + n, 0)),
            pl.BlockSpec(blk, lambda b, h, n: (2 * H + h, b * nb + n, 0)),
        ],
        out_specs=pl.BlockSpec((ATT_ROWS, E), lambda b, h, n: (b * nb + n, h)),
        out_shape=jax.ShapeDtypeStruct((batch * seq, H * E), f32),
        scratch_shapes=[pltpu.VMEM((ATT_ROWS, E), bf16)] + kv_scratch + [
            pltpu.VMEM((len(DILATIONS), ATT_ROWS, E), f32),
            pltpu.VMEM((len(DILATIONS), ATT_ROWS, E), f32),
            pltpu.VMEM((3, ATT_ROWS, E), f32),
        ],
        compiler_params=_cparams("parallel", "parallel", "arbitrary"),
        name="att_prompt",
    )(pa_hm, pa_hm, pa_hm)


SAMPLE_PAD = 16


def _att_sample_body(qkv_ref, k1_ref, v1_ref, k4_ref, v4_ref, k16_ref, v16_ref, o_ref, *, t_new):
    J = SUB_WINDOW
    hidx = lax.broadcasted_iota(jnp.int32, (N_ATT_HEADS, 1), 0)
    slope = jnp.exp2((hidx + 1).astype(f32) * (-8.0 / N_ATT_HEADS))
    slot = lax.broadcasted_iota(jnp.int32, (J, N_ATT_HEADS, 1), 0)
    q = [qkv_ref[i, 0] * ATT_SCALE for i in range(t_new)]
    kn = [qkv_ref[i, 1] for i in range(t_new)]
    vn = [qkv_ref[i, 2] for i in range(t_new)]
    far_bias = {dil: (slope * float(dil))[None] * (J - slot).astype(f32) for dil in DILATIONS if dil > 1}
    for i in range(t_new):
        s_new = [jnp.sum(q[i] * kn[n], axis=-1, keepdims=True) for n in range(i + 1)]
        outs, lses = [], []
        for dil in DILATIONS:
            if dil == 1:
                kc, vc = k1_ref[...], v1_ref[...]
                bias = slope[None] * (J + i - slot).astype(f32)
                news = [(s_new[n] - slope * float(i - n), vn[n]) for n in range(i + 1)]
            else:
                kres_ref, vres_ref = (k4_ref, v4_ref) if dil == 4 else (k16_ref, v16_ref)
                kc, vc = kres_ref[:, i], vres_ref[:, i]
                bias = far_bias[dil]
                news = [(s_new[i], vn[i])]
            s = jnp.sum(kc * q[i][None], axis=-1, keepdims=True) - bias
            if dil == 1:
                s = jnp.where(slot >= i, s, NEG_INF)
            m = jnp.max(s, axis=0)
            for sn, _ in news:
                m = jnp.maximum(m, sn)
            p = jnp.exp(s - m[None])
            l = jnp.sum(p, axis=0)
            acc = jnp.sum(p * vc, axis=0)
            for sn, v in news:
                pn = jnp.exp(sn - m)
                l = l + pn
                acc = acc + pn * v
            outs.append(acc / l)
            lses.append(m + jnp.log(l))
        mm = jnp.maximum(jnp.maximum(lses[0], lses[1]), lses[2])
        ws = [jnp.exp(x - mm) for x in lses]
        tot = ws[0] + ws[1] + ws[2]
        o_ref[i] = (ws[0] * outs[0] + ws[1] * outs[1] + ws[2] * outs[2]) / tot


def _att_sample(qkv_s, cache_k, cache_v, *, batch, t_new):
    w_cache = cache_k.shape[1]
    J = SUB_WINDOW
    he = (N_ATT_HEADS, ATT_HEAD_DIM)
    assert w_cache == max(DILATIONS) * J and t_new <= 4
    specs, views = [], []
    for dil in DILATIONS:
        rows = w_cache // dil
        last = rows // J - 1
        for c in (cache_k, cache_v):
            if dil == 1:
                views.append(c)
                specs.append(pl.BlockSpec((None, J) + he, lambda b, last=last: (b, last, 0, 0)))
            else:
                views.append(c.reshape((batch, rows, dil) + he))
                specs.append(pl.BlockSpec((None, J, 4) + he, lambda b, last=last: (b, last, 0, 0, 0)))
    kern = functools.partial(_att_sample_body, t_new=t_new)
    return pl.pallas_call(
        kern,
        grid=(batch,),
        in_specs=[pl.BlockSpec((None, t_new, 3) + he, lambda b: (b, 0, 0, 0, 0))] + specs,
        out_specs=pl.BlockSpec((None, t_new) + he, lambda b: (b, 0, 0, 0)),
        out_shape=jax.ShapeDtypeStruct((batch, t_new) + he, f32),
        compiler_params=_cparams("parallel"),
        name="att_sample",
    )(qkv_s, *views)


def _outproj_body(x_ref, go_ref, gg_ref, att_ref, gnw_ref, anw_ref, wg_ref, wa_ref, out_ref):
    parts = []
    for h in range(N_GLA_HEADS):
        vs = slice(h * GLA_DV, (h + 1) * GLA_DV)
        y = _rms(go_ref[:, vs], gnw_ref[...])
        gate = gg_ref[:, vs]
        parts.append((y * (gate * jax.nn.sigmoid(gate))).astype(bf16))
    gla_part = jnp.concatenate(parts, axis=-1)
    att_part = _rms(att_ref[...], anw_ref[...]).astype(bf16)
    mix = jnp.dot(gla_part, wg_ref[...], preferred_element_type=f32)
    mix = mix + jnp.dot(att_part, wa_ref[...], preferred_element_type=f32)
    out_ref[...] = x_ref[...] + mix


def _outproj(x, gla_o, gg_src, gg_col, att_o, gla_norm_w, att_norm_w, w_out_bf, tm):
    n = x.shape[0]
    row = lambda i: (i, 0)
    const = lambda i: (0, 0)
    in_specs = [
        pl.BlockSpec((tm, D_MODEL), row),
        pl.BlockSpec((tm, GLA_WIDTH), row),
        pl.BlockSpec((tm, GLA_WIDTH), lambda i: (i, gg_col)),
        pl.BlockSpec((tm, ATT_WIDTH), row),
        pl.BlockSpec((1, GLA_DV), const),
        pl.BlockSpec((1, ATT_WIDTH), const),
        pl.BlockSpec((GLA_WIDTH, D_MODEL), lambda i: (0, 0)),
        pl.BlockSpec((ATT_WIDTH, D_MODEL), lambda i: (1, 0)),
    ]
    return pl.pallas_call(
        _outproj_body,
        grid=(n // tm,),
        in_specs=in_specs,
        out_specs=pl.BlockSpec((tm, D_MODEL), row),
        out_shape=jax.ShapeDtypeStruct((n, D_MODEL), f32),
        compiler_params=_cparams("parallel"),
        name="mixer_out_proj",
    )(x, gla_o, gg_src, att_o, gla_norm_w.reshape(1, GLA_DV), att_norm_w.reshape(1, ATT_WIDTH), w_out_bf, w_out_bf)


def _mlp_body(x_ref, fw_ref, wu_ref, wd_ref, nw_ref, o_ref, *rest, emit_bf16):
    h_scr = rest[-1]
    f = pl.program_id(1)

    @pl.when(f == 0)
    def _():
        h_scr[...] = _rms(x_ref[...], fw_ref[...]).astype(bf16)
        o_ref[...] = jnp.zeros_like(o_ref)

    wu, wd = wu_ref[...], wd_ref[...]
    if emit_bf16:
        wu, wd = wu.astype(bf16), wd.astype(bf16)
        rest[0][...] = wu
        rest[1][...] = wd
    u = jnp.dot(h_scr[...], wu, preferred_element_type=f32)
    a = jnp.square(jnp.maximum(u, 0.0)).astype(bf16)
    o_ref[...] += jnp.dot(a, wd, preferred_element_type=f32)

    @pl.when(f == pl.num_programs(1) - 1)
    def _():
        o_ref[...] = _rms(x_ref[...] + o_ref[...], nw_ref[...])


def _mlp(x, ffn_norm_w, w_up, w_down, final_norm_w, tm, tf, emit_bf16=False):
    n = x.shape[0]
    assert not emit_bf16 or n == tm
    out_specs = [pl.BlockSpec((tm, D_MODEL), lambda i, f: (i, 0))]
    out_shape = [jax.ShapeDtypeStruct((n, D_MODEL), f32)]
    if emit_bf16:
        out_specs += [pl.BlockSpec((D_MODEL, tf), lambda i, f: (0, f)), pl.BlockSpec((tf, D_MODEL), lambda i, f: (f, 0))]
        out_shape += [jax.ShapeDtypeStruct((D_MODEL, D_FF), bf16), jax.ShapeDtypeStruct((D_FF, D_MODEL), bf16)]
    res = pl.pallas_call(
        functools.partial(_mlp_body, emit_bf16=emit_bf16),
        grid=(n // tm, D_FF // tf),
        in_specs=[
            pl.BlockSpec((tm, D_MODEL), lambda i, f: (i, 0)),
            pl.BlockSpec((1, D_MODEL), lambda i, f: (0, 0)),
            pl.BlockSpec((D_MODEL, tf), lambda i, f: (0, f)),
            pl.BlockSpec((tf, D_MODEL), lambda i, f: (f, 0)),
            pl.BlockSpec((1, D_MODEL), lambda i, f: (0, 0)),
        ],
        out_specs=out_specs,
        out_shape=out_shape,
        scratch_shapes=[pltpu.VMEM((tm, D_MODEL), bf16)],
        compiler_params=_cparams("parallel", "arbitrary"),
        name="mlp_final_norm",
    )(x, ffn_norm_w.reshape(1, D_MODEL), w_up, w_down, final_norm_w.reshape(1, D_MODEL))
    return res if emit_bf16 else res[0]


def kernel(x_prompt, x_sample, cache_k_win, cache_v_win, state_gla, attn_norm_w, w_in, w_gk_up, b_gk, gla_norm_w,
           att_out_norm_w, w_out, ffn_norm_w, w_up, w_down, final_norm_w):
    depth = w_in.shape[0]
    assert depth == 1, "single trunk layer"
    B, S, _ = x_prompt.shape
    Bs, Ts, _ = x_sample.shape
    w_p = min(MAX_WINDOW, S)
    assert S % (ATT_BLOCK * max(DILATIONS)) == 0 and Ts <= GLA_CHUNK

    w_in0 = jnp.swapaxes(w_in[0], 0, 1)
    w_gla_bf, w_att_bf = _split_w_in(w_in0, 256)
    wup_pad = jnp.pad(w_gk_up[0], ((0, LANES - GLA_GATE_RANK), (0, 0)))
    w_out_bf = w_out[0].astype(bf16)

    xp = x_prompt.reshape(B * S, D_MODEL)
    pg, loga, pa = _project(xp, attn_norm_w[0], w_gla_bf, w_att_bf, w_in0, wup_pad, b_gk[0], tm=1024, head_major=True)
    k_win, v_win = _window_kv(pa, batch=B, seq=S, window=w_p, tr=512)
    s0 = jnp.zeros((B, N_GLA_HEADS, GLA_DK, GLA_DV), f32)
    gla_o, gla_state_p = _gla(pg, loga, s0, batch=B, t_len=S, tb=512, t_valid=GLA_CHUNK)
    att = _att_prompt(pa, batch=B, seq=S)
    x1 = _outproj(xp, gla_o, pg, 2, att, gla_norm_w[0], att_out_norm_w[0], w_out_bf, 512)

    P = SAMPLE_PAD
    xs_pad = jnp.pad(x_sample, ((0, 0), (0, P - Ts), (0, 0))).reshape(Bs * P, D_MODEL)
    pgs, logas, pas = _project(xs_pad, attn_norm_w[0], w_gla_bf, w_att_bf, w_in0, wup_pad, b_gk[0], tm=Bs * P,
                               head_major=False)
    qkv_s = pas.reshape(Bs, P, 3, N_ATT_HEADS, ATT_HEAD_DIM)[:, :Ts]
    gla_o_s, gla_state_s = _gla(pgs, logas, state_gla[0], batch=Bs, t_len=P, tb=P, t_valid=Ts)
    att_s = _att_sample(qkv_s, cache_k_win[0], cache_v_win[0], batch=Bs, t_new=Ts).reshape(Bs * Ts, ATT_WIDTH)
    gla_o_s = gla_o_s.reshape(Bs, P, GLA_WIDTH)[:, :Ts].reshape(Bs * Ts, GLA_WIDTH)
    gg_s = pgs.reshape(Bs, P, GLA_COLS)[:, :Ts, 2 * GLA_KEY_WIDTH + GLA_WIDTH:].reshape(Bs * Ts, GLA_WIDTH)
    xs = x_sample.reshape(Bs * Ts, D_MODEL)
    x1s = _outproj(xs, gla_o_s, gg_s, 0, att_s, gla_norm_w[0], att_out_norm_w[0], w_out_bf, Bs * Ts)
    y_sample, w_up_bf, w_down_bf = _mlp(x1s, ffn_norm_w[0], w_up[0], w_down[0], final_norm_w, Bs * Ts, 512,
                                        emit_bf16=True)
    y_sample = y_sample.reshape(Bs, Ts, D_MODEL)
    y_prompt = _mlp(x1, ffn_norm_w[0], w_up_bf, w_down_bf, final_norm_w, 1024, 512).reshape(B, S, D_MODEL)

    k_win_prompt = k_win.reshape(1, B, w_p, N_ATT_HEADS, ATT_HEAD_DIM)
    v_win_prompt = v_win.reshape(1, B, w_p, N_ATT_HEADS, ATT_HEAD_DIM)
    k_new_sample = qkv_s[:, :, 1][None]
    v_new_sample = qkv_s[:, :, 2][None]
    return (y_prompt, y_sample, k_win_prompt, v_win_prompt, gla_state_p[None], k_new_sample, v_new_sample,
            gla_state_s[None])
```

```python
import functools

import jax
import jax.numpy as jnp
from jax import lax
from jax.experimental import pallas as pl
from jax.experimental.pallas import tpu as pltpu

f32 = jnp.float32
bf16 = jnp.bfloat16

D_MODEL = 2048
N_GLA_HEADS = 4
GLA_DK = 128
GLA_DV = 256
GLA_KEY_WIDTH = N_GLA_HEADS * GLA_DK
GLA_WIDTH = N_GLA_HEADS * GLA_DV
GLA_GATE_RANK = 16
GLA_GATE_NORM = 16.0
GLA_CHUNK = 16
N_ATT_HEADS = 8
ATT_HEAD_DIM = 128
ATT_WIDTH = N_ATT_HEADS * ATT_HEAD_DIM
DILATIONS = (1, 4, 16)
SUB_WINDOW = 128
ATT_BLOCK = 128
MAX_WINDOW = 2048
D_FF = 4 * D_MODEL
RMS_EPS = 1e-6
GLA_COLS = 2 * GLA_KEY_WIDTH + 2 * GLA_WIDTH
ATT_COLS = 3 * ATT_WIDTH
LANES = 128
VMEM_LIMIT = 56 * 1024 * 1024
NEG_INF = float("-inf")
ALIBI_SLOPES = tuple(2.0 ** (-8.0 * (h + 1) / N_ATT_HEADS) for h in range(N_ATT_HEADS))
ATT_SCALE = ATT_HEAD_DIM ** -0.5

NT_DIMS = (((1,), (1,)), ((), ()))
TN_DIMS = (((0,), (0,)), ((), ()))


def _cparams(*sem):
    return pltpu.CompilerParams(dimension_semantics=sem, vmem_limit_bytes=VMEM_LIMIT)


def _rms(x, w):
    r = lax.rsqrt(jnp.mean(x * x, axis=-1, keepdims=True) + RMS_EPS)
    return (x * r) * w


def _split_w_in_body(w_ref, g_ref, a_ref):
    g_ref[...] = w_ref[:GLA_COLS, :].astype(bf16)
    a_ref[...] = w_ref[GLA_COLS + GLA_GATE_RANK:, :].astype(bf16)


def _split_w_in(w_in_t, cols):
    rows, d = w_in_t.shape
    assert rows == GLA_COLS + GLA_GATE_RANK + ATT_COLS and d % cols == 0
    return pl.pallas_call(
        _split_w_in_body,
        grid=(d // cols,),
        in_specs=[pl.BlockSpec((rows, cols), lambda i: (0, i))],
        out_specs=[pl.BlockSpec((GLA_COLS, cols), lambda i: (0, i)), pl.BlockSpec((ATT_COLS, cols), lambda i: (0, i))],
        out_shape=[jax.ShapeDtypeStruct((GLA_COLS, d), bf16), jax.ShapeDtypeStruct((ATT_COLS, d), bf16)],
        compiler_params=_cparams("parallel"),
        name="split_w_in",
    )(w_in_t)


PROJ_TN = 512
N_GLA_TILES = GLA_COLS // PROJ_TN
N_ATT_TILES = ATT_COLS // PROJ_TN
TILES_PER_QKV = ATT_WIDTH // PROJ_TN
HEADS_PER_TILE = PROJ_TN // ATT_HEAD_DIM


def _proj_body(x_ref, nw_ref, wg_ref, wa_ref, wlr_ref, wup_ref, bgk_ref, pg_ref, loga_ref, pa_ref, h_scr, *,
               head_major):
    j = pl.program_id(1)

    @pl.when(j == 0)
    def _():
        h = _rms(x_ref[...], nw_ref[...]).astype(bf16)
        h_scr[...] = h
        glr = lax.dot_general(h, wlr_ref[...].astype(bf16), NT_DIMS, preferred_element_type=f32)
        R = GLA_GATE_RANK
        lane = lax.broadcasted_iota(jnp.int32, glr.shape, 1)
        g = jnp.where(lane < R, glr, 0.0)
        g_hi = g.astype(bf16).astype(f32)
        g3 = (g_hi + pltpu.roll(g_hi, R, 1) + pltpu.roll(g - g_hi, 2 * R, 1)).astype(bf16)
        w = wup_ref[...]
        w_hi = w.astype(bf16).astype(f32)
        w3 = (w_hi + pltpu.roll(w - w_hi, R, 0) + pltpu.roll(w_hi, 2 * R, 0)).astype(bf16)
        pre = jnp.dot(g3, w3, preferred_element_type=f32) + bgk_ref[...]
        log_sig = jnp.minimum(pre, 0.0) - jnp.log1p(jnp.exp(-jnp.abs(pre)))
        loga_ref[...] = log_sig * (1.0 / GLA_GATE_NORM)

    @pl.when(j < N_GLA_TILES)
    def _():
        pg_ref[...] = lax.dot_general(h_scr[...], wg_ref[...], NT_DIMS, preferred_element_type=f32)

    @pl.when(j >= N_GLA_TILES)
    def _():
        res = lax.dot_general(h_scr[...], wa_ref[...], NT_DIMS, preferred_element_type=f32)
        if head_major:
            for g in range(HEADS_PER_TILE):
                pa_ref[g] = res[:, g * LANES:(g + 1) * LANES]
        else:
            pa_ref[...] = res


def _project(x2d, norm_w, w_gla_bf, w_att_bf, w_in0, wup_pad, b_gk, *, tm, head_major):
    assert GLA_KEY_WIDTH == PROJ_TN and 2 * GLA_DV == PROJ_TN
    n, d = x2d.shape
    nb = n // tm
    nj = N_GLA_TILES + N_ATT_TILES
    gla_j = lambda j: jnp.minimum(j, N_GLA_TILES - 1)
    att_j = lambda j: jnp.maximum(j - N_GLA_TILES, 0)
    assert GLA_COLS % LANES == 0
    in_specs = [
        pl.BlockSpec((tm, d), lambda i, j: (i, 0)),
        pl.BlockSpec((1, d), lambda i, j: (0, 0)),
        pl.BlockSpec((PROJ_TN, d), lambda i, j: (gla_j(j), 0)),
        pl.BlockSpec((PROJ_TN, d), lambda i, j: (att_j(j), 0)),
        pl.BlockSpec((LANES, d), lambda i, j: (GLA_COLS // LANES, 0)),
        pl.BlockSpec((LANES, GLA_KEY_WIDTH), lambda i, j: (0, 0)),
        pl.BlockSpec((1, GLA_KEY_WIDTH), lambda i, j: (0, 0)),
    ]
    out_specs = [
        pl.BlockSpec((None, tm, PROJ_TN), lambda i, j: (gla_j(j), i, 0)),
        pl.BlockSpec((tm, GLA_KEY_WIDTH), lambda i, j: (i, 0)),
    ]
    out_shape = [jax.ShapeDtypeStruct((N_GLA_TILES, n, PROJ_TN), f32), jax.ShapeDtypeStruct((n, GLA_KEY_WIDTH), f32)]
    if head_major:
        out_specs.append(pl.BlockSpec((HEADS_PER_TILE, tm, LANES), lambda i, j: (att_j(j), i, 0)))
        out_shape.append(jax.ShapeDtypeStruct((3 * N_ATT_HEADS, n, ATT_HEAD_DIM), f32))
    else:
        out_specs.append(pl.BlockSpec((tm, PROJ_TN), lambda i, j: (i, att_j(j))))
        out_shape.append(jax.ShapeDtypeStruct((n, ATT_COLS), f32))
    return pl.pallas_call(
        functools.partial(_proj_body, head_major=head_major),
        grid=(nb, nj),
        in_specs=in_specs,
        out_specs=out_specs,
        out_shape=out_shape,
        scratch_shapes=[pltpu.VMEM((tm, d), bf16)],
        compiler_params=_cparams("parallel", "arbitrary"),
        name="norm_in_proj",
    )(x2d, norm_w.reshape(1, d), w_gla_bf, w_att_bf, w_in0, wup_pad, b_gk.reshape(1, GLA_KEY_WIDTH))


def _window_body(kh_ref, vh_ref, ko_ref, vo_ref):
    tr = kh_ref.shape[1]
    for src, dst in ((kh_ref, ko_ref), (vh_ref, vo_ref)):
        for hd in range(N_ATT_HEADS):
            dst[pl.ds(hd, tr, stride=N_ATT_HEADS), :] = src[hd]


def _window_kv(pa_hm, *, batch, seq, window, tr):
    H, E = N_ATT_HEADS, ATT_HEAD_DIM
    assert seq % tr == 0 and window % tr == 0
    per_seq, first = seq // tr, (seq - window) // tr
    nw = window // tr
    src = lambda which: pl.BlockSpec((H, tr, E), lambda b, t: (which, b * per_seq + first + t, 0))
    dst = pl.BlockSpec((tr * H, E), lambda b, t: (b * nw + t, 0))
    shape = jax.ShapeDtypeStruct((batch * window * H, E), f32)
    return pl.pallas_call(
        _window_body,
        grid=(batch, nw),
        in_specs=[src(1), src(2)],
        out_specs=[dst, dst],
        out_shape=[shape, shape],
        compiler_params=_cparams("parallel", "parallel"),
        name="kv_window",
    )(pa_hm, pa_hm)


GLA_SUB = 128


def _gla_body(q_ref, k_ref, v01_ref, v23_ref, g_ref, s0_ref, o_ref, sfin_ref, st_scr, bp_scr, *, tb, t_valid):
    C = GLA_CHUNK
    t_blk = pl.program_id(1)

    @pl.when(t_blk == 0)
    def _():
        for h in range(N_GLA_HEADS):
            st_scr[h] = s0_ref[h].T

    U = min(GLA_SUB, tb)
    levels = [w for w in (64, 32, 16, 8, 4, 2, 1) if w < U]
    row5 = lax.broadcasted_iota(jnp.int32, (U, GLA_KEY_WIDTH), 0)
    row1 = lax.broadcasted_iota(jnp.int32, (U, GLA_DK), 0)
    rowa = lax.broadcasted_iota(jnp.int32, (U, U), 0)
    cola = lax.broadcasted_iota(jnp.int32, (U, U), 1)
    ltri = (rowa >= cola).astype(bf16)
    off_diag = rowa != cola
    same_group = {w: ((rowa ^ cola) < 2 * w) & off_diag for w in levels}

    for sb in range(tb // U):
        r0 = sb * U
        g = g_ref[r0:r0 + U, :]
        if t_valid < C:
            g = jnp.where((row5 & (C - 1)) < t_valid, g, 0.0)
        g1 = g.astype(bf16)
        e1 = g - g1.astype(f32)
        g2 = e1.astype(bf16)
        g3 = (e1 - g2.astype(f32)).astype(bf16)
        bp = (jnp.dot(ltri, g1, preferred_element_type=f32) + jnp.dot(ltri, g2, preferred_element_type=f32)
              + jnp.dot(ltri, g3, preferred_element_type=f32))
        bp_scr[...] = bp
        b_last = bp[U - 1:U, :]
        q = q_ref[r0:r0 + U, :] * (GLA_DK ** -0.5)
        k = k_ref[r0:r0 + U, :]
        qe = q * jnp.exp(bp)
        ke = k * jnp.exp(b_last - bp)
        dec = jnp.exp(b_last)
        facs = []
        for w in levels:
            if w >= 4:
                bc = jnp.concatenate([jnp.broadcast_to(bp_scr[s + w - 1:s + w, :], (2 * w, GLA_KEY_WIDTH))
                                      for s in range(0, U, 2 * w)], axis=0)
            elif w == 2:
                pos = row5 & 3
                bc = jnp.where(pos == 0, pltpu.roll(bp, U - 1, 0),
                               jnp.where(pos == 1, bp, jnp.where(pos == 2, pltpu.roll(bp, 1, 0), pltpu.roll(bp, 2, 0))))
            else:
                bc = jnp.where((row5 & 1) == 1, pltpu.roll(bp, 1, 0), bp)
            facs.append(jnp.exp(jnp.where((row5 & w) != 0, bp - bc, bc - bp)))
        for h in range(N_GLA_HEADS):
            ks = slice(h * GLA_DK, (h + 1) * GLA_DK)
            vs = slice(h * GLA_DV, (h + 1) * GLA_DV)
            qh, kh = q[:, ks], k[:, ks]
            a = lax.dot_general(qh.astype(bf16), kh.astype(bf16), NT_DIMS, preferred_element_type=f32)
            a = jnp.where(rowa == cola, a, 0.0)
            for w, fac in zip(levels, facs):
                upper = (row1 & w) != 0
                fh = fac[:, ks]
                rq = jnp.where(upper, qh * fh, 0.0).astype(bf16)
                ck = jnp.where(upper, 0.0, kh * fh).astype(bf16)
                p = lax.dot_general(rq, ck, NT_DIMS, preferred_element_type=f32)
                a = jnp.where(same_group[w], p, a)
            v_ref = (v01_ref, v23_ref)[h // 2]
            vh = v_ref[r0:r0 + U, (h % 2) * GLA_DV:(h % 2 + 1) * GLA_DV].astype(bf16)
            st = st_scr[h]
            o = jnp.dot(a.astype(bf16), vh, preferred_element_type=f32)
            o = o + lax.dot_general(qe[:, ks].astype(bf16), st.astype(bf16), NT_DIMS, preferred_element_type=f32)
            o_ref[r0:r0 + U, vs] = o
            kv_t = lax.dot_general(vh, ke[:, ks].astype(bf16), TN_DIMS, preferred_element_type=f32)
            st_scr[h] = dec[:, ks] * st + kv_t

    @pl.when(t_blk == pl.num_programs(1) - 1)
    def _():
        for h in range(N_GLA_HEADS):
            sfin_ref[h] = st_scr[h].T


def _gla(pg, loga, s0, *, batch, t_len, tb, t_valid):
    nt = t_len // tb
    kern = functools.partial(_gla_body, tb=tb, t_valid=t_valid)
    state_spec = pl.BlockSpec((None, N_GLA_HEADS, GLA_DK, GLA_DV), lambda b, t: (b, 0, 0, 0))
    return pl.pallas_call(
        kern,
        grid=(batch, nt),
        in_specs=[
            pl.BlockSpec((None, tb, PROJ_TN), lambda b, t: (0, b * nt + t, 0)),
            pl.BlockSpec((None, tb, PROJ_TN), lambda b, t: (1, b * nt + t, 0)),
            pl.BlockSpec((None, tb, PROJ_TN), lambda b, t: (2, b * nt + t, 0)),
            pl.BlockSpec((None, tb, PROJ_TN), lambda b, t: (3, b * nt + t, 0)),
            pl.BlockSpec((tb, GLA_KEY_WIDTH), lambda b, t: (b * nt + t, 0)),
            state_spec,
        ],
        out_specs=[pl.BlockSpec((tb, GLA_WIDTH), lambda b, t: (b * nt + t, 0)), state_spec],
        out_shape=[
            jax.ShapeDtypeStruct((batch * t_len, GLA_WIDTH), f32),
            jax.ShapeDtypeStruct((batch, N_GLA_HEADS, GLA_DK, GLA_DV), f32),
        ],
        scratch_shapes=[
            pltpu.VMEM((N_GLA_HEADS, GLA_DV, GLA_DK), f32),
            pltpu.VMEM((min(GLA_SUB, tb), GLA_KEY_WIDTH), f32),
        ],
        compiler_params=_cparams("parallel", "arbitrary"),
        name="gla",
    )(pg, pg, pg, pg, loga, s0)


ATT_ROWS = ATT_BLOCK * max(DILATIONS)


assert DILATIONS == (1, 4, 16)


def _att_prompt_body(q_ref, k_ref, v_ref, o_ref, qd_scr, kd1, vd1, kd4, vd4, kd16, vd16, od_scr, lse_scr, by4_scr):
    U = ATT_BLOCK
    h = pl.program_id(1)
    n = pl.program_id(2)
    kds, vds = (kd1, kd4, kd16), (vd1, vd4, vd16)

    @pl.when(n == 0)
    def _():
        for d, kd, vd in zip(DILATIONS, kds, vds):
            L = ATT_ROWS // d
            for r in range(d):
                kd[r * (U + L):r * (U + L) + U, :] = jnp.zeros((U, ATT_HEAD_DIM), bf16)
                vd[r * (U + L):r * (U + L) + U, :] = jnp.zeros((U, ATT_HEAD_DIM), bf16)

    qi = lax.broadcasted_iota(jnp.int32, (U, 2 * U), 0)
    ki = lax.broadcasted_iota(jnp.int32, (U, 2 * U), 1)
    dist = qi - ki + U
    in_window = (dist >= 0) & (dist <= SUB_WINDOW)
    slope = jnp.exp2(jnp.zeros((U, 2 * U), f32) - (h + 1).astype(f32) * (8.0 / N_ATT_HEADS))
    first_pen = jnp.where(n == 0, NEG_INF, 0.0)

    for di, d in enumerate(DILATIONS):
        L = ATT_ROWS // d
        nu = L // U
        kd, vd = kds[di], vds[di]
        bias = jnp.where(in_window, -(slope * float(d)) * dist.astype(f32), NEG_INF)
        for r in range(d):
            base = r * (U + L)
            for src_ref, by4, dst, off, scale in ((q_ref, by4_scr.at[0], qd_scr, r * L, ATT_SCALE),
                                                  (k_ref, by4_scr.at[1], kd, base + U, None),
                                                  (v_ref, by4_scr.at[2], vd, base + U, None)):
                if d == 1:
                    x = src_ref[...]
                elif d == 4:
                    x = src_ref[pl.ds(r, L, stride=4), :]
                    by4[r * L:(r + 1) * L, :] = x
                else:
                    x = by4[pl.ds((r % 4) * (ATT_ROWS // 4) + r // 4, L, stride=4), :]
                if scale is not None:
                    x = x * scale
                dst[off:off + L, :] = x.astype(bf16)

        bias_first = bias + jnp.where(ki < U, first_pen, 0.0)
        for r in range(d):
            for u in range(nu):
                qoff = r * L + u * U
                koff = r * (U + L) + u * U
                qq = qd_scr[qoff:qoff + U, :]
                kk = kd[koff:koff + 2 * U, :]
                vv = vd[koff:koff + 2 * U, :]
                s = lax.dot_general(qq, kk, NT_DIMS, preferred_element_type=f32) + (bias_first if u == 0 else bias)
                m = jnp.max(s, axis=-1, keepdims=True)
                p = jnp.exp(s - m)
                l = jnp.sum(p, axis=-1, keepdims=True)
                o = jnp.dot(p.astype(bf16), vv, preferred_element_type=f32) / l
                rows = pl.ds(u * (U * d) + r, U, stride=d) if d > 1 else slice(u * U, (u + 1) * U)
                od_scr[di, rows, :] = o
                lse_scr[di, rows, :] = jnp.broadcast_to(m + jnp.log(l), (U, ATT_HEAD_DIM))
        for r in range(d):
            base = r * (U + L)
            kd[base:base + U, :] = kd[base + L:base + L + U, :]
            vd[base:base + U, :] = vd[base + L:base + L + U, :]

    CH = 512
    for c in range(ATT_ROWS // CH):
        rs = slice(c * CH, (c + 1) * CH)
        ls = [lse_scr[di, rs, :] for di in range(len(DILATIONS))]
        mm = jnp.maximum(jnp.maximum(ls[0], ls[1]), ls[2])
        ws = [jnp.exp(x - mm) for x in ls]
        num = ws[0] * od_scr[0, rs, :] + ws[1] * od_scr[1, rs, :] + ws[2] * od_scr[2, rs, :]
        o_ref[rs, :] = num / (ws[0] + ws[1] + ws[2])


def _att_prompt(pa_hm, *, batch, seq):
    nb = seq // ATT_ROWS
    H, E, U = N_ATT_HEADS, ATT_HEAD_DIM, ATT_BLOCK
    blk = (None, ATT_ROWS, E)
    kv_scratch = []
    for d in DILATIONS:
        kv_scratch += [pltpu.VMEM((d * U + ATT_ROWS, E), bf16)] * 2
    return pl.pallas_call(
        _att_prompt_body,
        grid=(batch, H, nb),
        in_specs=[
            pl.BlockSpec(blk, lambda b, h, n: (h, b * nb + n, 0)),
            pl.BlockSpec(blk, lambda b, h, n: (H + h, b * nb + n, 0)),
            pl.BlockSpec(blk, lambda b, h, n: (2 * H + h, b * nb + n, 0)),
        ],
        out_specs=pl.BlockSpec(blk, lambda b, h, n: (h, b * nb + n, 0)),
        out_shape=jax.ShapeDtypeStruct((H, batch * seq, E), f32),
        scratch_shapes=[pltpu.VMEM((ATT_ROWS, E), bf16)] + kv_scratch + [
            pltpu.VMEM((len(DILATIONS), ATT_ROWS, E), f32),
            pltpu.VMEM((len(DILATIONS), ATT_ROWS, E), f32),
            pltpu.VMEM((3, ATT_ROWS, E), f32),
        ],
        compiler_params=_cparams("parallel", "parallel", "arbitrary"),
        name="att_prompt",
    )(pa_hm, pa_hm, pa_hm)


SAMPLE_PAD = 16


def _att_sample_body(qkv_ref, k1_ref, v1_ref, k4_ref, v4_ref, k16_ref, v16_ref, o_ref, *, t_new):
    J = SUB_WINDOW
    hidx = lax.broadcasted_iota(jnp.int32, (N_ATT_HEADS, 1), 0)
    slope = jnp.exp2((hidx + 1).astype(f32) * (-8.0 / N_ATT_HEADS))
    slot = lax.broadcasted_iota(jnp.int32, (J, N_ATT_HEADS, 1), 0)
    q = [qkv_ref[i, 0] * ATT_SCALE for i in range(t_new)]
    kn = [qkv_ref[i, 1] for i in range(t_new)]
    vn = [qkv_ref[i, 2] for i in range(t_new)]
    far_bias = {dil: (slope * float(dil))[None] * (J - slot).astype(f32) for dil in DILATIONS if dil > 1}
    for i in range(t_new):
        s_new = [jnp.sum(q[i] * kn[n], axis=-1, keepdims=True) for n in range(i + 1)]
        outs, lses = [], []
        for dil in DILATIONS:
            if dil == 1:
                kc, vc = k1_ref[...], v1_ref[...]
                bias = slope[None] * (J + i - slot).astype(f32)
                news = [(s_new[n] - slope * float(i - n), vn[n]) for n in range(i + 1)]
            else:
                kres_ref, vres_ref = (k4_ref, v4_ref) if dil == 4 else (k16_ref, v16_ref)
                kc, vc = kres_ref[:, i], vres_ref[:, i]
                bias = far_bias[dil]
                news = [(s_new[i], vn[i])]
            s = jnp.sum(kc * q[i][None], axis=-1, keepdims=True) - bias
            if dil == 1:
                s = jnp.where(slot >= i, s, NEG_INF)
            m = jnp.max(s, axis=0)
            for sn, _ in news:
                m = jnp.maximum(m, sn)
            p = jnp.exp(s - m[None])
            l = jnp.sum(p, axis=0)
            acc = jnp.sum(p * vc, axis=0)
            for sn, v in news:
                pn = jnp.exp(sn - m)
                l = l + pn
                acc = acc + pn * v
            outs.append(acc / l)
            lses.append(m + jnp.log(l))
        mm = jnp.maximum(jnp.maximum(lses[0], lses[1]), lses[2])
        ws = [jnp.exp(x - mm) for x in lses]
        tot = ws[0] + ws[1] + ws[2]
        o_ref[i] = (ws[0] * outs[0] + ws[1] * outs[1] + ws[2] * outs[2]) / tot


def _att_sample(qkv_s, cache_k, cache_v, *, batch, t_new):
    w_cache = cache_k.shape[1]
    J = SUB_WINDOW
    he = (N_ATT_HEADS, ATT_HEAD_DIM)
    assert w_cache == max(DILATIONS) * J and t_new <= 4
    specs, views = [], []
    for dil in DILATIONS:
        rows = w_cache // dil
        last = rows // J - 1
        for c in (cache_k, cache_v):
            if dil == 1:
                views.append(c)
                specs.append(pl.BlockSpec((None, J) + he, lambda b, last=last: (b, last, 0, 0)))
            else:
                views.append(c.reshape((batch, rows, dil) + he))
                specs.append(pl.BlockSpec((None, J, 4) + he, lambda b, last=last: (b, last, 0, 0, 0)))
    kern = functools.partial(_att_sample_body, t_new=t_new)
    return pl.pallas_call(
        kern,
        grid=(batch,),
        in_specs=[pl.BlockSpec((None, t_new, 3) + he, lambda b: (b, 0, 0, 0, 0))] + specs,
        out_specs=pl.BlockSpec((None, t_new) + he, lambda b: (b, 0, 0, 0)),
        out_shape=jax.ShapeDtypeStruct((batch, t_new) + he, f32),
        compiler_params=_cparams("parallel"),
        name="att_sample",
    )(qkv_s, *views)


def _outproj_body(x_ref, go_ref, gg01_ref, gg23_ref, att_ref, gnw_ref, anw_ref, wg_ref, wa_ref, out_ref):
    parts = []
    for h in range(N_GLA_HEADS):
        vs = slice(h * GLA_DV, (h + 1) * GLA_DV)
        y = _rms(go_ref[:, vs], gnw_ref[...])
        gate = (gg01_ref, gg23_ref)[h // 2][:, (h % 2) * GLA_DV:(h % 2 + 1) * GLA_DV]
        parts.append((y * (gate * jax.nn.sigmoid(gate))).astype(bf16))
    gla_part = jnp.concatenate(parts, axis=-1)
    att = jnp.concatenate([att_ref[h] for h in range(N_ATT_HEADS)], axis=-1)
    att_part = _rms(att, anw_ref[...]).astype(bf16)
    mix = jnp.dot(gla_part, wg_ref[...], preferred_element_type=f32)
    mix = mix + jnp.dot(att_part, wa_ref[...], preferred_element_type=f32)
    out_ref[...] = x_ref[...] + mix


def _outproj(x, gla_o, gg_src, gg_tile, att_o, gla_norm_w, att_norm_w, w_out_bf, tm):
    n = x.shape[0]
    row = lambda i: (i, 0)
    const = lambda i: (0, 0)
    in_specs = [
        pl.BlockSpec((tm, D_MODEL), row),
        pl.BlockSpec((tm, GLA_WIDTH), row),
        pl.BlockSpec((None, tm, PROJ_TN), lambda i: (gg_tile, i, 0)),
        pl.BlockSpec((None, tm, PROJ_TN), lambda i: (gg_tile + 1, i, 0)),
        pl.BlockSpec((N_ATT_HEADS, tm, ATT_HEAD_DIM), lambda i: (0, i, 0)),
        pl.BlockSpec((1, GLA_DV), const),
        pl.BlockSpec((1, ATT_WIDTH), const),
        pl.BlockSpec((GLA_WIDTH, D_MODEL), lambda i: (0, 0)),
        pl.BlockSpec((ATT_WIDTH, D_MODEL), lambda i: (1, 0)),
    ]
    return pl.pallas_call(
        _outproj_body,
        grid=(n // tm,),
        in_specs=in_specs,
        out_specs=pl.BlockSpec((tm, D_MODEL), row),
        out_shape=jax.ShapeDtypeStruct((n, D_MODEL), f32),
        compiler_params=_cparams("parallel"),
        name="mixer_out_proj",
    )(x, gla_o, gg_src, gg_src, att_o, gla_norm_w.reshape(1, GLA_DV), att_norm_w.reshape(1, ATT_WIDTH),
      w_out_bf, w_out_bf)


def _mlp_body(x_ref, fw_ref, wu_ref, wd_ref, nw_ref, o_ref, *rest, emit_bf16):
    h_scr = rest[-1]
    f = pl.program_id(1)

    @pl.when(f == 0)
    def _():
        h_scr[...] = _rms(x_ref[...], fw_ref[...]).astype(bf16)
        o_ref[...] = jnp.zeros_like(o_ref)

    wu, wd = wu_ref[...], wd_ref[...]
    if emit_bf16:
        wu, wd = wu.astype(bf16), wd.astype(bf16)
        rest[0][...] = wu
        rest[1][...] = wd
    u = jnp.dot(h_scr[...], wu, preferred_element_type=f32)
    a = jnp.square(jnp.maximum(u, 0.0)).astype(bf16)
    o_ref[...] += jnp.dot(a, wd, preferred_element_type=f32)

    @pl.when(f == pl.num_programs(1) - 1)
    def _():
        o_ref[...] = _rms(x_ref[...] + o_ref[...], nw_ref[...])


def _mlp(x, ffn_norm_w, w_up, w_down, final_norm_w, tm, tf, emit_bf16=False):
    n = x.shape[0]
    assert not emit_bf16 or n == tm
    out_specs = [pl.BlockSpec((tm, D_MODEL), lambda i, f: (i, 0))]
    out_shape = [jax.ShapeDtypeStruct((n, D_MODEL), f32)]
    if emit_bf16:
        out_specs += [pl.BlockSpec((D_MODEL, tf), lambda i, f: (0, f)), pl.BlockSpec((tf, D_MODEL), lambda i, f: (f, 0))]
        out_shape += [jax.ShapeDtypeStruct((D_MODEL, D_FF), bf16), jax.ShapeDtypeStruct((D_FF, D_MODEL), bf16)]
    res = pl.pallas_call(
        functools.partial(_mlp_body, emit_bf16=emit_bf16),
        grid=(n // tm, D_FF // tf),
        in_specs=[
            pl.BlockSpec((tm, D_MODEL), lambda i, f: (i, 0)),
            pl.BlockSpec((1, D_MODEL), lambda i, f: (0, 0)),
            pl.BlockSpec((D_MODEL, tf), lambda i, f: (0, f)),
            pl.BlockSpec((tf, D_MODEL), lambda i, f: (f, 0)),
            pl.BlockSpec((1, D_MODEL), lambda i, f: (0, 0)),
        ],
        out_specs=out_specs,
        out_shape=out_shape,
        scratch_shapes=[pltpu.VMEM((tm, D_MODEL), bf16)],
        compiler_params=_cparams("parallel", "arbitrary"),
        name="mlp_final_norm",
    )(x, ffn_norm_w.reshape(1, D_MODEL), w_up, w_down, final_norm_w.reshape(1, D_MODEL))
    return res if emit_bf16 else res[0]


def kernel(x_prompt, x_sample, cache_k_win, cache_v_win, state_gla, attn_norm_w, w_in, w_gk_up, b_gk, gla_norm_w,
           att_out_norm_w, w_out, ffn_norm_w, w_up, w_down, final_norm_w):
    depth = w_in.shape[0]
    assert depth == 1, "single trunk layer"
    B, S, _ = x_prompt.shape
    Bs, Ts, _ = x_sample.shape
    w_p = min(MAX_WINDOW, S)
    assert S % (ATT_BLOCK * max(DILATIONS)) == 0 and Ts <= GLA_CHUNK

    w_in0 = jnp.swapaxes(w_in[0], 0, 1)
    w_gla_bf, w_att_bf = _split_w_in(w_in0, 256)
    wup_pad = jnp.pad(w_gk_up[0], ((0, LANES - GLA_GATE_RANK), (0, 0)))
    w_out_bf = w_out[0].astype(bf16)

    xp = x_prompt.reshape(B * S, D_MODEL)
    pg, loga, pa = _project(xp, attn_norm_w[0], w_gla_bf, w_att_bf, w_in0, wup_pad, b_gk[0], tm=1024, head_major=True)
    k_win, v_win = _window_kv(pa, batch=B, seq=S, window=w_p, tr=512)
    s0 = jnp.zeros((B, N_GLA_HEADS, GLA_DK, GLA_DV), f32)
    gla_o, gla_state_p = _gla(pg, loga, s0, batch=B, t_len=S, tb=512, t_valid=GLA_CHUNK)
    att = _att_prompt(pa, batch=B, seq=S)
    x1 = _outproj(xp, gla_o, pg, 4, att, gla_norm_w[0], att_out_norm_w[0], w_out_bf, 512)

    P = SAMPLE_PAD
    xs_pad = jnp.pad(x_sample, ((0, 0), (0, P - Ts), (0, 0))).reshape(Bs * P, D_MODEL)
    pgs, logas, pas = _project(xs_pad, attn_norm_w[0], w_gla_bf, w_att_bf, w_in0, wup_pad, b_gk[0], tm=Bs * P,
                               head_major=False)
    qkv_s = pas.reshape(Bs, P, 3, N_ATT_HEADS, ATT_HEAD_DIM)[:, :Ts]
    gla_o_s, gla_state_s = _gla(pgs, logas, state_gla[0], batch=Bs, t_len=P, tb=P, t_valid=Ts)
    att_s = _att_sample(qkv_s, cache_k_win[0], cache_v_win[0], batch=Bs, t_new=Ts)
    att_s = att_s.reshape(Bs * Ts, N_ATT_HEADS, ATT_HEAD_DIM).swapaxes(0, 1)
    gla_o_s = gla_o_s.reshape(Bs, P, GLA_WIDTH)[:, :Ts].reshape(Bs * Ts, GLA_WIDTH)
    gg_s = pgs[4:].reshape(2, Bs, P, PROJ_TN)[:, :, :Ts].reshape(2, Bs * Ts, PROJ_TN)
    xs = x_sample.reshape(Bs * Ts, D_MODEL)
    x1s = _outproj(xs, gla_o_s, gg_s, 0, att_s, gla_norm_w[0], att_out_norm_w[0], w_out_bf, Bs * Ts)
    y_sample, w_up_bf, w_down_bf = _mlp(x1s, ffn_norm_w[0], w_up[0], w_down[0], final_norm_w, Bs * Ts, 512,
                                        emit_bf16=True)
    y_sample = y_sample.reshape(Bs, Ts, D_MODEL)
    y_prompt = _mlp(x1, ffn_norm_w[0], w_up_bf, w_down_bf, final_norm_w, 1024, 512).reshape(B, S, D_MODEL)

    k_win_prompt = k_win.reshape(1, B, w_p, N_ATT_HEADS, ATT_HEAD_DIM)
    v_win_prompt = v_win.reshape(1, B, w_p, N_ATT_HEADS, ATT_HEAD_DIM)
    k_new_sample = qkv_s[:, :, 1][None]
    v_new_sample = qkv_s[:, :, 2][None]
    return (y_prompt, y_sample, k_win_prompt, v_win_prompt, gla_state_p[None], k_new_sample, v_new_sample,
            gla_state_s[None])
```

```python
import functools

import jax
import jax.numpy as jnp
from jax import lax
from jax.experimental import pallas as pl
from jax.experimental.pallas import tpu as pltpu

f32 = jnp.float32
bf16 = jnp.bfloat16

D_MODEL = 2048
N_GLA_HEADS = 4
GLA_DK = 128
GLA_DV = 256
GLA_KEY_WIDTH = N_GLA_HEADS * GLA_DK
GLA_WIDTH = N_GLA_HEADS * GLA_DV
GLA_GATE_RANK = 16
GLA_GATE_NORM = 16.0
GLA_CHUNK = 16
N_ATT_HEADS = 8
ATT_HEAD_DIM = 128
ATT_WIDTH = N_ATT_HEADS * ATT_HEAD_DIM
DILATIONS = (1, 4, 16)
SUB_WINDOW = 128
ATT_BLOCK = 128
MAX_WINDOW = 2048
D_FF = 4 * D_MODEL
RMS_EPS = 1e-6
GLA_COLS = 2 * GLA_KEY_WIDTH + 2 * GLA_WIDTH
ATT_COLS = 3 * ATT_WIDTH
LANES = 128
VMEM_LIMIT = 56 * 1024 * 1024
NEG_INF = float("-inf")
ALIBI_SLOPES = tuple(2.0 ** (-8.0 * (h + 1) / N_ATT_HEADS) for h in range(N_ATT_HEADS))
ATT_SCALE = ATT_HEAD_DIM ** -0.5

NT_DIMS = (((1,), (1,)), ((), ()))
TN_DIMS = (((0,), (0,)), ((), ()))


def _cparams(*sem):
    return pltpu.CompilerParams(dimension_semantics=sem, vmem_limit_bytes=VMEM_LIMIT)


def _rms(x, w):
    r = lax.rsqrt(jnp.mean(x * x, axis=-1, keepdims=True) + RMS_EPS)
    return (x * r) * w


def _split_w_in_body(w_ref, g_ref, a_ref):
    g_ref[...] = w_ref[:GLA_COLS, :].astype(bf16)
    a_ref[...] = w_ref[GLA_COLS + GLA_GATE_RANK:, :].astype(bf16)


def _split_w_in(w_in_t, cols):
    rows, d = w_in_t.shape
    assert rows == GLA_COLS + GLA_GATE_RANK + ATT_COLS and d % cols == 0
    return pl.pallas_call(
        _split_w_in_body,
        grid=(d // cols,),
        in_specs=[pl.BlockSpec((rows, cols), lambda i: (0, i))],
        out_specs=[pl.BlockSpec((GLA_COLS, cols), lambda i: (0, i)), pl.BlockSpec((ATT_COLS, cols), lambda i: (0, i))],
        out_shape=[jax.ShapeDtypeStruct((GLA_COLS, d), bf16), jax.ShapeDtypeStruct((ATT_COLS, d), bf16)],
        compiler_params=_cparams("parallel"),
        name="split_w_in",
    )(w_in_t)


PROJ_TN = 512
N_GLA_TILES = GLA_COLS // PROJ_TN
N_ATT_TILES = ATT_COLS // PROJ_TN
TILES_PER_QKV = ATT_WIDTH // PROJ_TN
HEADS_PER_TILE = PROJ_TN // ATT_HEAD_DIM


def _proj_body(x_ref, nw_ref, wg_ref, wa_ref, wlr_ref, wup_ref, bgk_ref, pg_ref, loga_ref, pa_ref, h_scr, *,
               head_major):
    j = pl.program_id(1)

    @pl.when(j == 0)
    def _():
        h = _rms(x_ref[...], nw_ref[...]).astype(bf16)
        h_scr[...] = h
        glr = lax.dot_general(h, wlr_ref[...].astype(bf16), NT_DIMS, preferred_element_type=f32)
        R = GLA_GATE_RANK
        lane = lax.broadcasted_iota(jnp.int32, glr.shape, 1)
        g = jnp.where(lane < R, glr, 0.0)
        g_hi = g.astype(bf16).astype(f32)
        g3 = (g_hi + pltpu.roll(g_hi, R, 1) + pltpu.roll(g - g_hi, 2 * R, 1)).astype(bf16)
        w = wup_ref[...]
        w_hi = w.astype(bf16).astype(f32)
        w3 = (w_hi + pltpu.roll(w - w_hi, R, 0) + pltpu.roll(w_hi, 2 * R, 0)).astype(bf16)
        pre = jnp.dot(g3, w3, preferred_element_type=f32) + bgk_ref[...]
        log_sig = jnp.minimum(pre, 0.0) - jnp.log1p(jnp.exp(-jnp.abs(pre)))
        loga_ref[...] = log_sig * (1.0 / GLA_GATE_NORM)

    @pl.when(j < N_GLA_TILES)
    def _():
        pg_ref[...] = lax.dot_general(h_scr[...], wg_ref[...], NT_DIMS, preferred_element_type=f32)

    @pl.when(j >= N_GLA_TILES)
    def _():
        res = lax.dot_general(h_scr[...], wa_ref[...], NT_DIMS, preferred_element_type=f32)
        if head_major:
            for g in range(HEADS_PER_TILE):
                pa_ref[g] = res[:, g * LANES:(g + 1) * LANES]
        else:
            pa_ref[...] = res


def _project(x2d, norm_w, w_gla_bf, w_att_bf, w_in0, wup_pad, b_gk, *, tm, head_major):
    assert GLA_KEY_WIDTH == PROJ_TN and 2 * GLA_DV == PROJ_TN
    n, d = x2d.shape
    nb = n // tm
    nj = N_GLA_TILES + N_ATT_TILES
    gla_j = lambda j: jnp.minimum(j, N_GLA_TILES - 1)
    att_j = lambda j: jnp.maximum(j - N_GLA_TILES, 0)
    assert GLA_COLS % LANES == 0
    in_specs = [
        pl.BlockSpec((tm, d), lambda i, j: (i, 0)),
        pl.BlockSpec((1, d), lambda i, j: (0, 0)),
        pl.BlockSpec((PROJ_TN, d), lambda i, j: (gla_j(j), 0)),
        pl.BlockSpec((PROJ_TN, d), lambda i, j: (att_j(j), 0)),
        pl.BlockSpec((LANES, d), lambda i, j: (GLA_COLS // LANES, 0)),
        pl.BlockSpec((LANES, GLA_KEY_WIDTH), lambda i, j: (0, 0)),
        pl.BlockSpec((1, GLA_KEY_WIDTH), lambda i, j: (0, 0)),
    ]
    out_specs = [
        pl.BlockSpec((None, tm, PROJ_TN), lambda i, j: (gla_j(j), i, 0)),
        pl.BlockSpec((tm, GLA_KEY_WIDTH), lambda i, j: (i, 0)),
    ]
    out_shape = [jax.ShapeDtypeStruct((N_GLA_TILES, n, PROJ_TN), f32), jax.ShapeDtypeStruct((n, GLA_KEY_WIDTH), f32)]
    if head_major:
        out_specs.append(pl.BlockSpec((HEADS_PER_TILE, tm, LANES), lambda i, j: (att_j(j), i, 0)))
        out_shape.append(jax.ShapeDtypeStruct((3 * N_ATT_HEADS, n, ATT_HEAD_DIM), f32))
    else:
        out_specs.append(pl.BlockSpec((tm, PROJ_TN), lambda i, j: (i, att_j(j))))
        out_shape.append(jax.ShapeDtypeStruct((n, ATT_COLS), f32))
    return pl.pallas_call(
        functools.partial(_proj_body, head_major=head_major),
        grid=(nb, nj),
        in_specs=in_specs,
        out_specs=out_specs,
        out_shape=out_shape,
        scratch_shapes=[pltpu.VMEM((tm, d), bf16)],
        compiler_params=_cparams("parallel", "arbitrary"),
        name="norm_in_proj",
    )(x2d, norm_w.reshape(1, d), w_gla_bf, w_att_bf, w_in0, wup_pad, b_gk.reshape(1, GLA_KEY_WIDTH))


def _window_body(kh_ref, vh_ref, ko_ref, vo_ref):
    tr = kh_ref.shape[1]
    for src, dst in ((kh_ref, ko_ref), (vh_ref, vo_ref)):
        for hd in range(N_ATT_HEADS):
            dst[pl.ds(hd, tr, stride=N_ATT_HEADS), :] = src[hd]


def _window_kv(pa_hm, *, batch, seq, window, tr):
    H, E = N_ATT_HEADS, ATT_HEAD_DIM
    assert seq % tr == 0 and window % tr == 0
    per_seq, first = seq // tr, (seq - window) // tr
    nw = window // tr
    src = lambda which: pl.BlockSpec((H, tr, E), lambda b, t: (which, b * per_seq + first + t, 0))
    dst = pl.BlockSpec((tr * H, E), lambda b, t: (b * nw + t, 0))
    shape = jax.ShapeDtypeStruct((batch * window * H, E), f32)
    return pl.pallas_call(
        _window_body,
        grid=(batch, nw),
        in_specs=[src(1), src(2)],
        out_specs=[dst, dst],
        out_shape=[shape, shape],
        compiler_params=_cparams("parallel", "parallel"),
        name="kv_window",
    )(pa_hm, pa_hm)


GLA_SUB = 128


def _gla_body(q_ref, k_ref, v01_ref, v23_ref, g_ref, s0_ref, o_ref, sfin_ref, st_scr, bp_scr, *, tb, t_valid, nseq):
    C = GLA_CHUNK
    t_blk = pl.program_id(1)

    @pl.when(t_blk == 0)
    def _():
        for sq in range(nseq):
            for h in range(N_GLA_HEADS):
                st_scr[sq, h] = s0_ref[sq, h].T

    U = min(GLA_SUB, tb)
    levels = [w for w in (64, 32, 16, 8, 4, 2, 1) if w < U]
    row5 = lax.broadcasted_iota(jnp.int32, (U, GLA_KEY_WIDTH), 0)
    row1 = lax.broadcasted_iota(jnp.int32, (U, GLA_DK), 0)
    rowa = lax.broadcasted_iota(jnp.int32, (U, U), 0)
    cola = lax.broadcasted_iota(jnp.int32, (U, U), 1)
    ltri = (rowa >= cola).astype(bf16)
    off_diag = rowa != cola
    same_group = {w: ((rowa ^ cola) < 2 * w) & off_diag for w in levels}

    for sb in range(nseq * (tb // U)):
        r0 = sb * U
        sq = sb // (tb // U)
        g = g_ref[r0:r0 + U, :]
        if t_valid < C:
            g = jnp.where((row5 & (C - 1)) < t_valid, g, 0.0)
        g1 = g.astype(bf16)
        e1 = g - g1.astype(f32)
        g2 = e1.astype(bf16)
        g3 = (e1 - g2.astype(f32)).astype(bf16)
        bp = (jnp.dot(ltri, g1, preferred_element_type=f32) + jnp.dot(ltri, g2, preferred_element_type=f32)
              + jnp.dot(ltri, g3, preferred_element_type=f32))
        bp_scr[sq] = bp
        b_last = bp[U - 1:U, :]
        q = q_ref[r0:r0 + U, :] * (GLA_DK ** -0.5)
        k = k_ref[r0:r0 + U, :]
        qe = q * jnp.exp(bp)
        ke = k * jnp.exp(b_last - bp)
        dec = jnp.exp(b_last)
        facs = []
        for w in levels:
            if w >= 4:
                bc = jnp.concatenate([jnp.broadcast_to(bp_scr[sq, s + w - 1:s + w, :], (2 * w, GLA_KEY_WIDTH))
                                      for s in range(0, U, 2 * w)], axis=0)
            elif w == 2:
                pos = row5 & 3
                bc = jnp.where(pos == 0, pltpu.roll(bp, U - 1, 0),
                               jnp.where(pos == 1, bp, jnp.where(pos == 2, pltpu.roll(bp, 1, 0), pltpu.roll(bp, 2, 0))))
            else:
                bc = jnp.where((row5 & 1) == 1, pltpu.roll(bp, 1, 0), bp)
            facs.append(jnp.exp(jnp.where((row5 & w) != 0, bp - bc, bc - bp)))
        for h in range(N_GLA_HEADS):
            ks = slice(h * GLA_DK, (h + 1) * GLA_DK)
            vs = slice(h * GLA_DV, (h + 1) * GLA_DV)
            qh, kh = q[:, ks], k[:, ks]
            a = lax.dot_general(qh.astype(bf16), kh.astype(bf16), NT_DIMS, preferred_element_type=f32)
            a = jnp.where(rowa == cola, a, 0.0)
            for w, fac in zip(levels, facs):
                upper = (row1 & w) != 0
                fh = fac[:, ks]
                rq = jnp.where(upper, qh * fh, 0.0).astype(bf16)
                ck = jnp.where(upper, 0.0, kh * fh).astype(bf16)
                p = lax.dot_general(rq, ck, NT_DIMS, preferred_element_type=f32)
                a = jnp.where(same_group[w], p, a)
            v_ref = (v01_ref, v23_ref)[h // 2]
            vh = v_ref[r0:r0 + U, (h % 2) * GLA_DV:(h % 2 + 1) * GLA_DV].astype(bf16)
            st = st_scr[sq, h]
            o = jnp.dot(a.astype(bf16), vh, preferred_element_type=f32)
            o = o + lax.dot_general(qe[:, ks].astype(bf16), st.astype(bf16), NT_DIMS, preferred_element_type=f32)
            o_ref[r0:r0 + U, vs] = o
            kv_t = lax.dot_general(vh, ke[:, ks].astype(bf16), TN_DIMS, preferred_element_type=f32)
            st_scr[sq, h] = dec[:, ks] * st + kv_t

    @pl.when(t_blk == pl.num_programs(1) - 1)
    def _():
        for sq in range(nseq):
            for h in range(N_GLA_HEADS):
                sfin_ref[sq, h] = st_scr[sq, h].T


def _gla(pg, loga, s0, *, batch, t_len, tb, t_valid, nseq=1):
    nt = t_len // tb
    assert batch % nseq == 0 and (nseq == 1 or nt == 1)
    rb = nseq * tb
    kern = functools.partial(_gla_body, tb=tb, t_valid=t_valid, nseq=nseq)
    state_spec = pl.BlockSpec((nseq, N_GLA_HEADS, GLA_DK, GLA_DV), lambda b, t: (b, 0, 0, 0))
    return pl.pallas_call(
        kern,
        grid=(batch // nseq, nt),
        in_specs=[
            pl.BlockSpec((None, rb, PROJ_TN), lambda b, t: (0, b * nt + t, 0)),
            pl.BlockSpec((None, rb, PROJ_TN), lambda b, t: (1, b * nt + t, 0)),
            pl.BlockSpec((None, rb, PROJ_TN), lambda b, t: (2, b * nt + t, 0)),
            pl.BlockSpec((None, rb, PROJ_TN), lambda b, t: (3, b * nt + t, 0)),
            pl.BlockSpec((rb, GLA_KEY_WIDTH), lambda b, t: (b * nt + t, 0)),
            state_spec,
        ],
        out_specs=[pl.BlockSpec((rb, GLA_WIDTH), lambda b, t: (b * nt + t, 0)), state_spec],
        out_shape=[
            jax.ShapeDtypeStruct((batch * t_len, GLA_WIDTH), f32),
            jax.ShapeDtypeStruct((batch, N_GLA_HEADS, GLA_DK, GLA_DV), f32),
        ],
        scratch_shapes=[
            pltpu.VMEM((nseq, N_GLA_HEADS, GLA_DV, GLA_DK), f32),
            pltpu.VMEM((nseq, min(GLA_SUB, tb), GLA_KEY_WIDTH), f32),
        ],
        compiler_params=_cparams("parallel", "arbitrary"),
        name="gla",
    )(pg, pg, pg, pg, loga, s0)


ATT_ROWS = ATT_BLOCK * max(DILATIONS)


assert DILATIONS == (1, 4, 16)


def _att_prompt_body(q_ref, k_ref, v_ref, o_ref, qd_scr, kd1, vd1, kd4, vd4, kd16, vd16, od_scr, lse_scr, by4_scr):
    U = ATT_BLOCK
    h = pl.program_id(1)
    n = pl.program_id(2)
    kds, vds = (kd1, kd4, kd16), (vd1, vd4, vd16)

    @pl.when(n == 0)
    def _():
        for d, kd, vd in zip(DILATIONS, kds, vds):
            L = ATT_ROWS // d
            for r in range(d):
                kd[r * (U + L):r * (U + L) + U, :] = jnp.zeros((U, ATT_HEAD_DIM), bf16)
                vd[r * (U + L):r * (U + L) + U, :] = jnp.zeros((U, ATT_HEAD_DIM), bf16)

    qi = lax.broadcasted_iota(jnp.int32, (U, 2 * U), 0)
    ki = lax.broadcasted_iota(jnp.int32, (U, 2 * U), 1)
    dist = qi - ki + U
    in_window = (dist >= 0) & (dist <= SUB_WINDOW)
    slope = jnp.exp2(jnp.zeros((U, 2 * U), f32) - (h + 1).astype(f32) * (8.0 / N_ATT_HEADS))
    first_pen = jnp.where(n == 0, NEG_INF, 0.0)

    for di, d in enumerate(DILATIONS):
        L = ATT_ROWS // d
        nu = L // U
        kd, vd = kds[di], vds[di]
        bias = jnp.where(in_window, -(slope * float(d)) * dist.astype(f32), NEG_INF)
        for r in range(d):
            base = r * (U + L)
            for src_ref, by4, dst, off, scale in ((q_ref, by4_scr.at[0], qd_scr, r * L, ATT_SCALE),
                                                  (k_ref, by4_scr.at[1], kd, base + U, None),
                                                  (v_ref, by4_scr.at[2], vd, base + U, None)):
                if d == 1:
                    x = src_ref[...]
                elif d == 4:
                    x = src_ref[pl.ds(r, L, stride=4), :]
                    by4[r * L:(r + 1) * L, :] = x
                else:
                    x = by4[pl.ds((r % 4) * (ATT_ROWS // 4) + r // 4, L, stride=4), :]
                if scale is not None:
                    x = x * scale
                dst[off:off + L, :] = x.astype(bf16)

        bias_first = bias + jnp.where(ki < U, first_pen, 0.0)
        for r in range(d):
            for u in range(nu):
                qoff = r * L + u * U
                koff = r * (U + L) + u * U
                qq = qd_scr[qoff:qoff + U, :]
                kk = kd[koff:koff + 2 * U, :]
                vv = vd[koff:koff + 2 * U, :]
                s = lax.dot_general(qq, kk, NT_DIMS, preferred_element_type=f32) + (bias_first if u == 0 else bias)
                m = jnp.max(s, axis=-1, keepdims=True)
                p = jnp.exp(s - m)
                l = jnp.sum(p, axis=-1, keepdims=True)
                o = jnp.dot(p.astype(bf16), vv, preferred_element_type=f32) / l
                rows = pl.ds(u * (U * d) + r, U, stride=d) if d > 1 else slice(u * U, (u + 1) * U)
                od_scr[di, rows, :] = o
                lse_scr[di, rows, :] = jnp.broadcast_to(m + jnp.log(l), (U, ATT_HEAD_DIM))
        for r in range(d):
            base = r * (U + L)
            kd[base:base + U, :] = kd[base + L:base + L + U, :]
            vd[base:base + U, :] = vd[base + L:base + L + U, :]

    CH = 512
    for c in range(ATT_ROWS // CH):
        rs = slice(c * CH, (c + 1) * CH)
        ls = [lse_scr[di, rs, :] for di in range(len(DILATIONS))]
        mm = jnp.maximum(jnp.maximum(ls[0], ls[1]), ls[2])
        ws = [jnp.exp(x - mm) for x in ls]
        num = ws[0] * od_scr[0, rs, :] + ws[1] * od_scr[1, rs, :] + ws[2] * od_scr[2, rs, :]
        o_ref[rs, :] = num / (ws[0] + ws[1] + ws[2])


def _att_prompt(pa_hm, *, batch, seq):
    nb = seq // ATT_ROWS
    H, E, U = N_ATT_HEADS, ATT_HEAD_DIM, ATT_BLOCK
    blk = (None, ATT_ROWS, E)
    kv_scratch = []
    for d in DILATIONS:
        kv_scratch += [pltpu.VMEM((d * U + ATT_ROWS, E), bf16)] * 2
    return pl.pallas_call(
        _att_prompt_body,
        grid=(batch, H, nb),
        in_specs=[
            pl.BlockSpec(blk, lambda b, h, n: (h, b * nb + n, 0)),
            pl.BlockSpec(blk, lambda b, h, n: (H + h, b * nb + n, 0)),
            pl.BlockSpec(blk, lambda b, h, n: (2 * H + h, b * nb + n, 0)),
        ],
        out_specs=pl.BlockSpec(blk, lambda b, h, n: (h, b * nb + n, 0)),
        out_shape=jax.ShapeDtypeStruct((H, batch * seq, E), f32),
        scratch_shapes=[pltpu.VMEM((ATT_ROWS, E), bf16)] + kv_scratch + [
            pltpu.VMEM((len(DILATIONS), ATT_ROWS, E), f32),
            pltpu.VMEM((len(DILATIONS), ATT_ROWS, E), f32),
            pltpu.VMEM((3, ATT_ROWS, E), f32),
        ],
        compiler_params=_cparams("parallel", "parallel", "arbitrary"),
        name="att_prompt",
    )(pa_hm, pa_hm, pa_hm)


SAMPLE_PAD = 16


def _att_sample_body(qkv_ref, k1_ref, v1_ref, k4_ref, v4_ref, k16_ref, v16_ref, o_ref, *, t_new):
    J = SUB_WINDOW
    hidx = lax.broadcasted_iota(jnp.int32, (N_ATT_HEADS, 1), 0)
    slope = jnp.exp2((hidx + 1).astype(f32) * (-8.0 / N_ATT_HEADS))
    slot = lax.broadcasted_iota(jnp.int32, (J, N_ATT_HEADS, 1), 0)
    q = [qkv_ref[i, 0] * ATT_SCALE for i in range(t_new)]
    kn = [qkv_ref[i, 1] for i in range(t_new)]
    vn = [qkv_ref[i, 2] for i in range(t_new)]
    far_bias = {dil: (slope * float(dil))[None] * (J - slot).astype(f32) for dil in DILATIONS if dil > 1}
    for i in range(t_new):
        s_new = [jnp.sum(q[i] * kn[n], axis=-1, keepdims=True) for n in range(i + 1)]
        outs, lses = [], []
        for dil in DILATIONS:
            if dil == 1:
                kc, vc = k1_ref[...], v1_ref[...]
                bias = slope[None] * (J + i - slot).astype(f32)
                news = [(s_new[n] - slope * float(i - n), vn[n]) for n in range(i + 1)]
            else:
                kres_ref, vres_ref = (k4_ref, v4_ref) if dil == 4 else (k16_ref, v16_ref)
                kc, vc = kres_ref[:, i], vres_ref[:, i]
                bias = far_bias[dil]
                news = [(s_new[i], vn[i])]
            s = jnp.sum(kc * q[i][None], axis=-1, keepdims=True) - bias
            if dil == 1:
                s = jnp.where(slot >= i, s, NEG_INF)
            m = jnp.max(s, axis=0)
            for sn, _ in news:
                m = jnp.maximum(m, sn)
            p = jnp.exp(s - m[None])
            l = jnp.sum(p, axis=0)
            acc = jnp.sum(p * vc, axis=0)
            for sn, v in news:
                pn = jnp.exp(sn - m)
                l = l + pn
                acc = acc + pn * v
            outs.append(acc / l)
            lses.append(m + jnp.log(l))
        mm = jnp.maximum(jnp.maximum(lses[0], lses[1]), lses[2])
        ws = [jnp.exp(x - mm) for x in lses]
        tot = ws[0] + ws[1] + ws[2]
        o_ref[i] = (ws[0] * outs[0] + ws[1] * outs[1] + ws[2] * outs[2]) / tot


def _att_sample(qkv_s, cache_k, cache_v, *, batch, t_new):
    w_cache = cache_k.shape[1]
    J = SUB_WINDOW
    he = (N_ATT_HEADS, ATT_HEAD_DIM)
    assert w_cache == max(DILATIONS) * J and t_new <= 4
    specs, views = [], []
    for dil in DILATIONS:
        rows = w_cache // dil
        last = rows // J - 1
        for c in (cache_k, cache_v):
            if dil == 1:
                views.append(c)
                specs.append(pl.BlockSpec((None, J) + he, lambda b, last=last: (b, last, 0, 0)))
            else:
                views.append(c.reshape((batch, rows, dil) + he))
                specs.append(pl.BlockSpec((None, J, 4) + he, lambda b, last=last: (b, last, 0, 0, 0)))
    kern = functools.partial(_att_sample_body, t_new=t_new)
    return pl.pallas_call(
        kern,
        grid=(batch,),
        in_specs=[pl.BlockSpec((None, t_new, 3) + he, lambda b: (b, 0, 0, 0, 0))] + specs,
        out_specs=pl.BlockSpec((None, t_new) + he, lambda b: (b, 0, 0, 0)),
        out_shape=jax.ShapeDtypeStruct((batch, t_new) + he, f32),
        compiler_params=_cparams("parallel"),
        name="att_sample",
    )(qkv_s, *views)


def _outproj_body(x_ref, go_ref, gg01_ref, gg23_ref, att_ref, gnw_ref, anw_ref, wg_ref, wa_ref, out_ref):
    parts = []
    for h in range(N_GLA_HEADS):
        vs = slice(h * GLA_DV, (h + 1) * GLA_DV)
        y = _rms(go_ref[:, vs], gnw_ref[...])
        gate = (gg01_ref, gg23_ref)[h // 2][:, (h % 2) * GLA_DV:(h % 2 + 1) * GLA_DV]
        parts.append((y * (gate * jax.nn.sigmoid(gate))).astype(bf16))
    gla_part = jnp.concatenate(parts, axis=-1)
    att = jnp.concatenate([att_ref[h] for h in range(N_ATT_HEADS)], axis=-1)
    att_part = _rms(att, anw_ref[...]).astype(bf16)
    mix = jnp.dot(gla_part, wg_ref[...], preferred_element_type=f32)
    mix = mix + jnp.dot(att_part, wa_ref[...], preferred_element_type=f32)
    out_ref[...] = x_ref[...] + mix


def _outproj(x, gla_o, gg_src, gg_tile, att_o, gla_norm_w, att_norm_w, w_out_bf, tm):
    n = x.shape[0]
    row = lambda i: (i, 0)
    const = lambda i: (0, 0)
    in_specs = [
        pl.BlockSpec((tm, D_MODEL), row),
        pl.BlockSpec((tm, GLA_WIDTH), row),
        pl.BlockSpec((None, tm, PROJ_TN), lambda i: (gg_tile, i, 0)),
        pl.BlockSpec((None, tm, PROJ_TN), lambda i: (gg_tile + 1, i, 0)),
        pl.BlockSpec((N_ATT_HEADS, tm, ATT_HEAD_DIM), lambda i: (0, i, 0)),
        pl.BlockSpec((1, GLA_DV), const),
        pl.BlockSpec((1, ATT_WIDTH), const),
        pl.BlockSpec((GLA_WIDTH, D_MODEL), lambda i: (0, 0)),
        pl.BlockSpec((ATT_WIDTH, D_MODEL), lambda i: (1, 0)),
    ]
    return pl.pallas_call(
        _outproj_body,
        grid=(n // tm,),
        in_specs=in_specs,
        out_specs=pl.BlockSpec((tm, D_MODEL), row),
        out_shape=jax.ShapeDtypeStruct((n, D_MODEL), f32),
        compiler_params=_cparams("parallel"),
        name="mixer_out_proj",
    )(x, gla_o, gg_src, gg_src, att_o, gla_norm_w.reshape(1, GLA_DV), att_norm_w.reshape(1, ATT_WIDTH),
      w_out_bf, w_out_bf)


def _mlp_body(x_ref, fw_ref, wu_ref, wd_ref, nw_ref, o_ref, *rest, emit_bf16):
    h_scr = rest[-1]
    f = pl.program_id(1)

    @pl.when(f == 0)
    def _():
        h_scr[...] = _rms(x_ref[...], fw_ref[...]).astype(bf16)
        o_ref[...] = jnp.zeros_like(o_ref)

    wu, wd = wu_ref[...], wd_ref[...]
    if emit_bf16:
        wu, wd = wu.astype(bf16), wd.astype(bf16)
        rest[0][...] = wu
        rest[1][...] = wd
    u = jnp.dot(h_scr[...], wu, preferred_element_type=f32)
    a = jnp.square(jnp.maximum(u, 0.0)).astype(bf16)
    o_ref[...] += jnp.dot(a, wd, preferred_element_type=f32)

    @pl.when(f == pl.num_programs(1) - 1)
    def _():
        o_ref[...] = _rms(x_ref[...] + o_ref[...], nw_ref[...])


def _mlp(x, ffn_norm_w, w_up, w_down, final_norm_w, tm, tf, emit_bf16=False):
    n = x.shape[0]
    assert not emit_bf16 or n == tm
    out_specs = [pl.BlockSpec((tm, D_MODEL), lambda i, f: (i, 0))]
    out_shape = [jax.ShapeDtypeStruct((n, D_MODEL), f32)]
    if emit_bf16:
        out_specs += [pl.BlockSpec((D_MODEL, tf), lambda i, f: (0, f)), pl.BlockSpec((tf, D_MODEL), lambda i, f: (f, 0))]
        out_shape += [jax.ShapeDtypeStruct((D_MODEL, D_FF), bf16), jax.ShapeDtypeStruct((D_FF, D_MODEL), bf16)]
    res = pl.pallas_call(
        functools.partial(_mlp_body, emit_bf16=emit_bf16),
        grid=(n // tm, D_FF // tf),
        in_specs=[
            pl.BlockSpec((tm, D_MODEL), lambda i, f: (i, 0)),
            pl.BlockSpec((1, D_MODEL), lambda i, f: (0, 0)),
            pl.BlockSpec((D_MODEL, tf), lambda i, f: (0, f)),
            pl.BlockSpec((tf, D_MODEL), lambda i, f: (f, 0)),
            pl.BlockSpec((1, D_MODEL), lambda i, f: (0, 0)),
        ],
        out_specs=out_specs,
        out_shape=out_shape,
        scratch_shapes=[pltpu.VMEM((tm, D_MODEL), bf16)],
        compiler_params=_cparams("parallel", "arbitrary"),
        name="mlp_final_norm",
    )(x, ffn_norm_w.reshape(1, D_MODEL), w_up, w_down, final_norm_w.reshape(1, D_MODEL))
    return res if emit_bf16 else res[0]


def kernel(x_prompt, x_sample, cache_k_win, cache_v_win, state_gla, attn_norm_w, w_in, w_gk_up, b_gk, gla_norm_w,
           att_out_norm_w, w_out, ffn_norm_w, w_up, w_down, final_norm_w):
    depth = w_in.shape[0]
    assert depth == 1, "single trunk layer"
    B, S, _ = x_prompt.shape
    Bs, Ts, _ = x_sample.shape
    w_p = min(MAX_WINDOW, S)
    assert S % (ATT_BLOCK * max(DILATIONS)) == 0 and Ts <= GLA_CHUNK

    w_in0 = jnp.swapaxes(w_in[0], 0, 1)
    w_gla_bf, w_att_bf = _split_w_in(w_in0, 256)
    wup_pad = jnp.pad(w_gk_up[0], ((0, LANES - GLA_GATE_RANK), (0, 0)))
    w_out_bf = w_out[0].astype(bf16)

    xp = x_prompt.reshape(B * S, D_MODEL)
    pg, loga, pa = _project(xp, attn_norm_w[0], w_gla_bf, w_att_bf, w_in0, wup_pad, b_gk[0], tm=1024, head_major=True)
    k_win, v_win = _window_kv(pa, batch=B, seq=S, window=w_p, tr=512)
    s0 = jnp.zeros((B, N_GLA_HEADS, GLA_DK, GLA_DV), f32)
    gla_o, gla_state_p = _gla(pg, loga, s0, batch=B, t_len=S, tb=512, t_valid=GLA_CHUNK)
    att = _att_prompt(pa, batch=B, seq=S)
    x1 = _outproj(xp, gla_o, pg, 4, att, gla_norm_w[0], att_out_norm_w[0], w_out_bf, 512)

    P = SAMPLE_PAD
    xs_pad = jnp.pad(x_sample, ((0, 0), (0, P - Ts), (0, 0))).reshape(Bs * P, D_MODEL)
    pgs, logas, pas = _project(xs_pad, attn_norm_w[0], w_gla_bf, w_att_bf, w_in0, wup_pad, b_gk[0], tm=Bs * P,
                               head_major=False)
    qkv_s = pas.reshape(Bs, P, 3, N_ATT_HEADS, ATT_HEAD_DIM)[:, :Ts]
    gla_o_s, gla_state_s = _gla(pgs, logas, state_gla[0], batch=Bs, t_len=P, tb=P, t_valid=Ts, nseq=4)
    att_s = _att_sample(qkv_s, cache_k_win[0], cache_v_win[0], batch=Bs, t_new=Ts)
    att_s = att_s.reshape(Bs * Ts, N_ATT_HEADS, ATT_HEAD_DIM).swapaxes(0, 1)
    gla_o_s = gla_o_s.reshape(Bs, P, GLA_WIDTH)[:, :Ts].reshape(Bs * Ts, GLA_WIDTH)
    gg_s = pgs[4:].reshape(2, Bs, P, PROJ_TN)[:, :, :Ts].reshape(2, Bs * Ts, PROJ_TN)
    xs = x_sample.reshape(Bs * Ts, D_MODEL)
    x1s = _outproj(xs, gla_o_s, gg_s, 0, att_s, gla_norm_w[0], att_out_norm_w[0], w_out_bf, Bs * Ts)
    y_sample, w_up_bf, w_down_bf = _mlp(x1s, ffn_norm_w[0], w_up[0], w_down[0], final_norm_w, Bs * Ts, 512,
                                        emit_bf16=True)
    y_sample = y_sample.reshape(Bs, Ts, D_MODEL)
    y_prompt = _mlp(x1, ffn_norm_w[0], w_up_bf, w_down_bf, final_norm_w, 1024, 512).reshape(B, S, D_MODEL)

    k_win_prompt = k_win.reshape(1, B, w_p, N_ATT_HEADS, ATT_HEAD_DIM)
    v_win_prompt = v_win.reshape(1, B, w_p, N_ATT_HEADS, ATT_HEAD_DIM)
    k_new_sample = qkv_s[:, :, 1][None]
    v_new_sample = qkv_s[:, :, 2][None]
    return (y_prompt, y_sample, k_win_prompt, v_win_prompt, gla_state_p[None], k_new_sample, v_new_sample,
            gla_state_s[None])
```

```python
import functools

import jax
import jax.numpy as jnp
from jax import lax
from jax.experimental import pallas as pl
from jax.experimental.pallas import tpu as pltpu

f32 = jnp.float32
bf16 = jnp.bfloat16

D_MODEL = 2048
N_GLA_HEADS = 4
GLA_DK = 128
GLA_DV = 256
GLA_KEY_WIDTH = N_GLA_HEADS * GLA_DK
GLA_WIDTH = N_GLA_HEADS * GLA_DV
GLA_GATE_RANK = 16
GLA_GATE_NORM = 16.0
GLA_CHUNK = 16
N_ATT_HEADS = 8
ATT_HEAD_DIM = 128
ATT_WIDTH = N_ATT_HEADS * ATT_HEAD_DIM
DILATIONS = (1, 4, 16)
SUB_WINDOW = 128
ATT_BLOCK = 128
MAX_WINDOW = 2048
D_FF = 4 * D_MODEL
RMS_EPS = 1e-6
GLA_COLS = 2 * GLA_KEY_WIDTH + 2 * GLA_WIDTH
ATT_COLS = 3 * ATT_WIDTH
LANES = 128
VMEM_LIMIT = 56 * 1024 * 1024
NEG_INF = float("-inf")
ATT_SCALE = ATT_HEAD_DIM ** -0.5

NT_DIMS = (((1,), (1,)), ((), ()))
TN_DIMS = (((0,), (0,)), ((), ()))


def _cparams(*sem):
    return pltpu.CompilerParams(dimension_semantics=sem, vmem_limit_bytes=VMEM_LIMIT)


def _rms(x, w):
    r = lax.rsqrt(jnp.mean(x * x, axis=-1, keepdims=True) + RMS_EPS)
    return (x * r) * w


def _split_w_in_body(w_ref, g_ref, a_ref):
    g_ref[...] = w_ref[:GLA_COLS, :].astype(bf16)
    a_ref[...] = w_ref[GLA_COLS + GLA_GATE_RANK:, :].astype(bf16)


def _split_w_in(w_in_t, cols):
    rows, d = w_in_t.shape
    assert rows == GLA_COLS + GLA_GATE_RANK + ATT_COLS and d % cols == 0
    return pl.pallas_call(
        _split_w_in_body,
        grid=(d // cols,),
        in_specs=[pl.BlockSpec((rows, cols), lambda i: (0, i))],
        out_specs=[pl.BlockSpec((GLA_COLS, cols), lambda i: (0, i)), pl.BlockSpec((ATT_COLS, cols), lambda i: (0, i))],
        out_shape=[jax.ShapeDtypeStruct((GLA_COLS, d), bf16), jax.ShapeDtypeStruct((ATT_COLS, d), bf16)],
        compiler_params=_cparams("parallel"),
        name="split_w_in",
    )(w_in_t)


PROJ_TN = 512
N_GLA_TILES = GLA_COLS // PROJ_TN
N_ATT_TILES = ATT_COLS // PROJ_TN
HEADS_PER_TILE = PROJ_TN // ATT_HEAD_DIM


def _proj_body(x_ref, nw_ref, wg_ref, wa_ref, wlr_ref, wup_ref, bgk_ref, pg_ref, loga_ref, pa_ref, h_scr, *,
               head_major):
    j = pl.program_id(1)

    @pl.when(j == 0)
    def _():
        h = _rms(x_ref[...], nw_ref[...]).astype(bf16)
        h_scr[...] = h
        glr = lax.dot_general(h, wlr_ref[...].astype(bf16), NT_DIMS, preferred_element_type=f32)
        R = GLA_GATE_RANK
        lane = lax.broadcasted_iota(jnp.int32, glr.shape, 1)
        g = jnp.where(lane < R, glr, 0.0)
        g_hi = g.astype(bf16).astype(f32)
        g3 = (g_hi + pltpu.roll(g_hi, R, 1) + pltpu.roll(g - g_hi, 2 * R, 1)).astype(bf16)
        w = wup_ref[...]
        w_hi = w.astype(bf16).astype(f32)
        w3 = (w_hi + pltpu.roll(w - w_hi, R, 0) + pltpu.roll(w_hi, 2 * R, 0)).astype(bf16)
        pre = jnp.dot(g3, w3, preferred_element_type=f32) + bgk_ref[...]
        log_sig = jnp.minimum(pre, 0.0) - jnp.log1p(jnp.exp(-jnp.abs(pre)))
        loga_ref[...] = log_sig * (1.0 / GLA_GATE_NORM)

    @pl.when(j < N_GLA_TILES)
    def _():
        pg_ref[...] = lax.dot_general(h_scr[...], wg_ref[...], NT_DIMS, preferred_element_type=f32)

    @pl.when(j >= N_GLA_TILES)
    def _():
        res = lax.dot_general(h_scr[...], wa_ref[...], NT_DIMS, preferred_element_type=f32)
        if head_major:
            for g in range(HEADS_PER_TILE):
                pa_ref[g] = res[:, g * LANES:(g + 1) * LANES]
        else:
            pa_ref[...] = res


def _project(x2d, norm_w, w_gla_bf, w_att_bf, w_in0, wup_pad, b_gk, *, tm, head_major):
    assert GLA_KEY_WIDTH == PROJ_TN and 2 * GLA_DV == PROJ_TN
    n, d = x2d.shape
    nb = n // tm
    nj = N_GLA_TILES + N_ATT_TILES
    gla_j = lambda j: jnp.minimum(j, N_GLA_TILES - 1)
    att_j = lambda j: jnp.maximum(j - N_GLA_TILES, 0)
    assert GLA_COLS % LANES == 0
    in_specs = [
        pl.BlockSpec((tm, d), lambda i, j: (i, 0)),
        pl.BlockSpec((1, d), lambda i, j: (0, 0)),
        pl.BlockSpec((PROJ_TN, d), lambda i, j: (gla_j(j), 0)),
        pl.BlockSpec((PROJ_TN, d), lambda i, j: (att_j(j), 0)),
        pl.BlockSpec((LANES, d), lambda i, j: (GLA_COLS // LANES, 0)),
        pl.BlockSpec((LANES, GLA_KEY_WIDTH), lambda i, j: (0, 0)),
        pl.BlockSpec((1, GLA_KEY_WIDTH), lambda i, j: (0, 0)),
    ]
    out_specs = [
        pl.BlockSpec((None, tm, PROJ_TN), lambda i, j: (gla_j(j), i, 0)),
        pl.BlockSpec((tm, GLA_KEY_WIDTH), lambda i, j: (i, 0)),
    ]
    out_shape = [jax.ShapeDtypeStruct((N_GLA_TILES, n, PROJ_TN), f32), jax.ShapeDtypeStruct((n, GLA_KEY_WIDTH), f32)]
    if head_major:
        out_specs.append(pl.BlockSpec((HEADS_PER_TILE, tm, LANES), lambda i, j: (att_j(j), i, 0)))
        out_shape.append(jax.ShapeDtypeStruct((3 * N_ATT_HEADS, n, ATT_HEAD_DIM), f32))
    else:
        out_specs.append(pl.BlockSpec((tm, PROJ_TN), lambda i, j: (i, att_j(j))))
        out_shape.append(jax.ShapeDtypeStruct((n, ATT_COLS), f32))
    return pl.pallas_call(
        functools.partial(_proj_body, head_major=head_major),
        grid=(nb, nj),
        in_specs=in_specs,
        out_specs=out_specs,
        out_shape=out_shape,
        scratch_shapes=[pltpu.VMEM((tm, d), bf16)],
        compiler_params=_cparams("parallel", "arbitrary"),
        name="norm_in_proj",
    )(x2d, norm_w.reshape(1, d), w_gla_bf, w_att_bf, w_in0, wup_pad, b_gk.reshape(1, GLA_KEY_WIDTH))


def _window_body(kh_ref, vh_ref, ko_ref, vo_ref):
    tr = kh_ref.shape[1]
    for src, dst in ((kh_ref, ko_ref), (vh_ref, vo_ref)):
        for hd in range(N_ATT_HEADS):
            dst[pl.ds(hd, tr, stride=N_ATT_HEADS), :] = src[hd]


def _window_kv(pa_hm, *, batch, seq, window, tr):
    H, E = N_ATT_HEADS, ATT_HEAD_DIM
    assert seq % tr == 0 and window % tr == 0
    per_seq, first = seq // tr, (seq - window) // tr
    nw = window // tr
    src = lambda which: pl.BlockSpec((H, tr, E), lambda b, t: (which, b * per_seq + first + t, 0))
    dst = pl.BlockSpec((tr * H, E), lambda b, t: (b * nw + t, 0))
    shape = jax.ShapeDtypeStruct((batch * window * H, E), f32)
    return pl.pallas_call(
        _window_body,
        grid=(batch, nw),
        in_specs=[src(1), src(2)],
        out_specs=[dst, dst],
        out_shape=[shape, shape],
        compiler_params=_cparams("parallel", "parallel"),
        name="kv_window",
    )(pa_hm, pa_hm)


GLA_SUB = 128


def _gla_body(q_ref, k_ref, v01_ref, v23_ref, g_ref, s0_ref, o_ref, sfin_ref, st_scr, bp_scr, *, tb, t_valid, nseq):
    C = GLA_CHUNK
    t_blk = pl.program_id(1)

    @pl.when(t_blk == 0)
    def _():
        for sq in range(nseq):
            for h in range(N_GLA_HEADS):
                st_scr[sq, h] = s0_ref[sq, h].T

    U = min(GLA_SUB, tb)
    levels = [w for w in (64, 32, 16, 8, 4, 2, 1) if w < U]
    row5 = lax.broadcasted_iota(jnp.int32, (U, GLA_KEY_WIDTH), 0)
    row1 = lax.broadcasted_iota(jnp.int32, (U, GLA_DK), 0)
    rowa = lax.broadcasted_iota(jnp.int32, (U, U), 0)
    cola = lax.broadcasted_iota(jnp.int32, (U, U), 1)
    ltri = (rowa >= cola).astype(bf16)
    off_diag = rowa != cola
    same_group = {w: ((rowa ^ cola) < 2 * w) & off_diag for w in levels}

    for sb in range(nseq * (tb // U)):
        r0 = sb * U
        sq = sb // (tb // U)
        g = g_ref[r0:r0 + U, :]
        if t_valid < C:
            g = jnp.where((row5 & (C - 1)) < t_valid, g, 0.0)
        g1 = g.astype(bf16)
        e1 = g - g1.astype(f32)
        g2 = e1.astype(bf16)
        g3 = (e1 - g2.astype(f32)).astype(bf16)
        bp = (jnp.dot(ltri, g1, preferred_element_type=f32) + jnp.dot(ltri, g2, preferred_element_type=f32)
              + jnp.dot(ltri, g3, preferred_element_type=f32))
        bp_scr[sq] = bp
        b_last = bp[U - 1:U, :]
        q = q_ref[r0:r0 + U, :] * (GLA_DK ** -0.5)
        k = k_ref[r0:r0 + U, :]
        qe = q * jnp.exp(bp)
        ke = k * jnp.exp(b_last - bp)
        dec = jnp.exp(b_last)
        facs = []
        for w in levels:
            if w >= 4:
                bc = jnp.concatenate([jnp.broadcast_to(bp_scr[sq, s + w - 1:s + w, :], (2 * w, GLA_KEY_WIDTH))
                                      for s in range(0, U, 2 * w)], axis=0)
            elif w == 2:
                pos = row5 & 3
                bc = jnp.where(pos == 0, pltpu.roll(bp, U - 1, 0),
                               jnp.where(pos == 1, bp, jnp.where(pos == 2, pltpu.roll(bp, 1, 0), pltpu.roll(bp, 2, 0))))
            else:
                bc = jnp.where((row5 & 1) == 1, pltpu.roll(bp, 1, 0), bp)
            facs.append(jnp.exp(jnp.where((row5 & w) != 0, bp - bc, bc - bp)))
        for h in range(N_GLA_HEADS):
            ks = slice(h * GLA_DK, (h + 1) * GLA_DK)
            vs = slice(h * GLA_DV, (h + 1) * GLA_DV)
            qh, kh = q[:, ks], k[:, ks]
            a = lax.dot_general(qh.astype(bf16), kh.astype(bf16), NT_DIMS, preferred_element_type=f32)
            a = jnp.where(rowa == cola, a, 0.0)
            for w, fac in zip(levels, facs):
                upper = (row1 & w) != 0
                fh = fac[:, ks]
                rq = jnp.where(upper, qh * fh, 0.0).astype(bf16)
                ck = jnp.where(upper, 0.0, kh * fh).astype(bf16)
                p = lax.dot_general(rq, ck, NT_DIMS, preferred_element_type=f32)
                a = jnp.where(same_group[w], p, a)
            v_ref = (v01_ref, v23_ref)[h // 2]
            vh = v_ref[r0:r0 + U, (h % 2) * GLA_DV:(h % 2 + 1) * GLA_DV].astype(bf16)
            st = st_scr[sq, h]
            o = jnp.dot(a.astype(bf16), vh, preferred_element_type=f32)
            o = o + lax.dot_general(qe[:, ks].astype(bf16), st.astype(bf16), NT_DIMS, preferred_element_type=f32)
            o_ref[r0:r0 + U, vs] = o
            kv_t = lax.dot_general(vh, ke[:, ks].astype(bf16), TN_DIMS, preferred_element_type=f32)
            st_scr[sq, h] = dec[:, ks] * st + kv_t

    @pl.when(t_blk == pl.num_programs(1) - 1)
    def _():
        for sq in range(nseq):
            for h in range(N_GLA_HEADS):
                sfin_ref[sq, h] = st_scr[sq, h].T


def _gla(pg, loga, s0, *, batch, t_len, tb, t_valid, nseq=1):
    nt = t_len // tb
    assert batch % nseq == 0 and (nseq == 1 or nt == 1)
    rb = nseq * tb
    kern = functools.partial(_gla_body, tb=tb, t_valid=t_valid, nseq=nseq)
    state_spec = pl.BlockSpec((nseq, N_GLA_HEADS, GLA_DK, GLA_DV), lambda b, t: (b, 0, 0, 0))
    return pl.pallas_call(
        kern,
        grid=(batch // nseq, nt),
        in_specs=[
            pl.BlockSpec((None, rb, PROJ_TN), lambda b, t: (0, b * nt + t, 0)),
            pl.BlockSpec((None, rb, PROJ_TN), lambda b, t: (1, b * nt + t, 0)),
            pl.BlockSpec((None, rb, PROJ_TN), lambda b, t: (2, b * nt + t, 0)),
            pl.BlockSpec((None, rb, PROJ_TN), lambda b, t: (3, b * nt + t, 0)),
            pl.BlockSpec((rb, GLA_KEY_WIDTH), lambda b, t: (b * nt + t, 0)),
            state_spec,
        ],
        out_specs=[pl.BlockSpec((rb, GLA_WIDTH), lambda b, t: (b * nt + t, 0)), state_spec],
        out_shape=[
            jax.ShapeDtypeStruct((batch * t_len, GLA_WIDTH), f32),
            jax.ShapeDtypeStruct((batch, N_GLA_HEADS, GLA_DK, GLA_DV), f32),
        ],
        scratch_shapes=[
            pltpu.VMEM((nseq, N_GLA_HEADS, GLA_DV, GLA_DK), f32),
            pltpu.VMEM((nseq, min(GLA_SUB, tb), GLA_KEY_WIDTH), f32),
        ],
        compiler_params=_cparams("parallel", "arbitrary"),
        name="gla",
    )(pg, pg, pg, pg, loga, s0)


ATT_ROWS = ATT_BLOCK * max(DILATIONS)


assert DILATIONS == (1, 4, 16)


def _att_prompt_body(q_ref, k_ref, v_ref, o_ref, qd_scr, kd1, vd1, kd4, vd4, kd16, vd16, od_scr, lse_scr, by4_scr):
    U = ATT_BLOCK
    h = pl.program_id(1)
    n = pl.program_id(2)
    kds, vds = (kd1, kd4, kd16), (vd1, vd4, vd16)

    @pl.when(n == 0)
    def _():
        for d, kd, vd in zip(DILATIONS, kds, vds):
            L = ATT_ROWS // d
            for r in range(d):
                kd[r * (U + L):r * (U + L) + U, :] = jnp.zeros((U, ATT_HEAD_DIM), bf16)
                vd[r * (U + L):r * (U + L) + U, :] = jnp.zeros((U, ATT_HEAD_DIM), bf16)

    qi = lax.broadcasted_iota(jnp.int32, (U, 2 * U), 0)
    ki = lax.broadcasted_iota(jnp.int32, (U, 2 * U), 1)
    dist = qi - ki + U
    in_window = (dist >= 0) & (dist <= SUB_WINDOW)
    slope = jnp.exp2(jnp.zeros((U, 2 * U), f32) - (h + 1).astype(f32) * (8.0 / N_ATT_HEADS))
    first_pen = jnp.where(n == 0, NEG_INF, 0.0)

    for di, d in enumerate(DILATIONS):
        L = ATT_ROWS // d
        nu = L // U
        kd, vd = kds[di], vds[di]
        bias = jnp.where(in_window, -(slope * float(d)) * dist.astype(f32), NEG_INF)
        for r in range(d):
            base = r * (U + L)
            for src_ref, by4, dst, off, scale in ((q_ref, by4_scr.at[0], qd_scr, r * L, ATT_SCALE),
                                                  (k_ref, by4_scr.at[1], kd, base + U, None),
                                                  (v_ref, by4_scr.at[2], vd, base + U, None)):
                if d == 1:
                    x = src_ref[...]
                elif d == 4:
                    x = src_ref[pl.ds(r, L, stride=4), :]
                    by4[r * L:(r + 1) * L, :] = x
                else:
                    x = by4[pl.ds((r % 4) * (ATT_ROWS // 4) + r // 4, L, stride=4), :]
                if scale is not None:
                    x = x * scale
                dst[off:off + L, :] = x.astype(bf16)

        bias_first = bias + jnp.where(ki < U, first_pen, 0.0)
        for r in range(d):
            for u in range(nu):
                qoff = r * L + u * U
                koff = r * (U + L) + u * U
                qq = qd_scr[qoff:qoff + U, :]
                kk = kd[koff:koff + 2 * U, :]
                vv = vd[koff:koff + 2 * U, :]
                s = lax.dot_general(qq, kk, NT_DIMS, preferred_element_type=f32) + (bias_first if u == 0 else bias)
                m = jnp.max(s, axis=-1, keepdims=True)
                p = jnp.exp(s - m)
                l = jnp.sum(p, axis=-1, keepdims=True)
                o = jnp.dot(p.astype(bf16), vv, preferred_element_type=f32) / l
                rows = pl.ds(u * (U * d) + r, U, stride=d) if d > 1 else slice(u * U, (u + 1) * U)
                od_scr[di, rows, :] = o
                lse_scr[di, rows, :] = jnp.broadcast_to(m + jnp.log(l), (U, ATT_HEAD_DIM))
        for r in range(d):
            base = r * (U + L)
            kd[base:base + U, :] = kd[base + L:base + L + U, :]
            vd[base:base + U, :] = vd[base + L:base + L + U, :]

    CH = 512
    for c in range(ATT_ROWS // CH):
        rs = slice(c * CH, (c + 1) * CH)
        ls = [lse_scr[di, rs, :] for di in range(len(DILATIONS))]
        mm = jnp.maximum(jnp.maximum(ls[0], ls[1]), ls[2])
        ws = [jnp.exp(x - mm) for x in ls]
        num = ws[0] * od_scr[0, rs, :] + ws[1] * od_scr[1, rs, :] + ws[2] * od_scr[2, rs, :]
        o_ref[rs, :] = num / (ws[0] + ws[1] + ws[2])


def _att_prompt(pa_hm, *, batch, seq):
    nb = seq // ATT_ROWS
    H, E, U = N_ATT_HEADS, ATT_HEAD_DIM, ATT_BLOCK
    blk = (None, ATT_ROWS, E)
    kv_scratch = []
    for d in DILATIONS:
        kv_scratch += [pltpu.VMEM((d * U + ATT_ROWS, E), bf16)] * 2
    return pl.pallas_call(
        _att_prompt_body,
        grid=(batch, H, nb),
        in_specs=[
            pl.BlockSpec(blk, lambda b, h, n: (h, b * nb + n, 0)),
            pl.BlockSpec(blk, lambda b, h, n: (H + h, b * nb + n, 0)),
            pl.BlockSpec(blk, lambda b, h, n: (2 * H + h, b * nb + n, 0)),
        ],
        out_specs=pl.BlockSpec(blk, lambda b, h, n: (h, b * nb + n, 0)),
        out_shape=jax.ShapeDtypeStruct((H, batch * seq, E), f32),
        scratch_shapes=[pltpu.VMEM((ATT_ROWS, E), bf16)] + kv_scratch + [
            pltpu.VMEM((len(DILATIONS), ATT_ROWS, E), f32),
            pltpu.VMEM((len(DILATIONS), ATT_ROWS, E), f32),
            pltpu.VMEM((3, ATT_ROWS, E), f32),
        ],
        compiler_params=_cparams("parallel", "parallel", "arbitrary"),
        name="att_prompt",
    )(pa_hm, pa_hm, pa_hm)


SAMPLE_PAD = 16


def _att_sample_body(qkv_ref, k1_ref, v1_ref, k4_ref, v4_ref, k16_ref, v16_ref, o_ref, *, t_new):
    J = SUB_WINDOW
    hidx = lax.broadcasted_iota(jnp.int32, (N_ATT_HEADS, 1), 0)
    slope = jnp.exp2((hidx + 1).astype(f32) * (-8.0 / N_ATT_HEADS))
    slot = lax.broadcasted_iota(jnp.int32, (J, N_ATT_HEADS, 1), 0)
    q = [qkv_ref[i, 0] * ATT_SCALE for i in range(t_new)]
    kn = [qkv_ref[i, 1] for i in range(t_new)]
    vn = [qkv_ref[i, 2] for i in range(t_new)]
    far_bias = {dil: (slope * float(dil))[None] * (J - slot).astype(f32) for dil in DILATIONS if dil > 1}
    for i in range(t_new):
        s_new = [jnp.sum(q[i] * kn[n], axis=-1, keepdims=True) for n in range(i + 1)]
        outs, lses = [], []
        for dil in DILATIONS:
            if dil == 1:
                kc, vc = k1_ref[...], v1_ref[...]
                bias = slope[None] * (J + i - slot).astype(f32)
                news = [(s_new[n] - slope * float(i - n), vn[n]) for n in range(i + 1)]
            else:
                kres_ref, vres_ref = (k4_ref, v4_ref) if dil == 4 else (k16_ref, v16_ref)
                kc, vc = kres_ref[:, i], vres_ref[:, i]
                bias = far_bias[dil]
                news = [(s_new[i], vn[i])]
            s = jnp.sum(kc * q[i][None], axis=-1, keepdims=True) - bias
            if dil == 1:
                s = jnp.where(slot >= i, s, NEG_INF)
            m = jnp.max(s, axis=0)
            for sn, _ in news:
                m = jnp.maximum(m, sn)
            p = jnp.exp(s - m[None])
            l = jnp.sum(p, axis=0)
            acc = jnp.sum(p * vc, axis=0)
            for sn, v in news:
                pn = jnp.exp(sn - m)
                l = l + pn
                acc = acc + pn * v
            outs.append(acc / l)
            lses.append(m + jnp.log(l))
        mm = jnp.maximum(jnp.maximum(lses[0], lses[1]), lses[2])
        ws = [jnp.exp(x - mm) for x in lses]
        tot = ws[0] + ws[1] + ws[2]
        o_ref[i] = (ws[0] * outs[0] + ws[1] * outs[1] + ws[2] * outs[2]) / tot


def _att_sample(qkv_s, cache_k, cache_v, *, batch, t_new):
    w_cache = cache_k.shape[1]
    J = SUB_WINDOW
    he = (N_ATT_HEADS, ATT_HEAD_DIM)
    assert w_cache == max(DILATIONS) * J and t_new <= 4
    specs, views = [], []
    for dil in DILATIONS:
        rows = w_cache // dil
        last = rows // J - 1
        for c in (cache_k, cache_v):
            if dil == 1:
                views.append(c)
                specs.append(pl.BlockSpec((None, J) + he, lambda b, last=last: (b, last, 0, 0)))
            else:
                views.append(c.reshape((batch, rows, dil) + he))
                specs.append(pl.BlockSpec((None, J, 4) + he, lambda b, last=last: (b, last, 0, 0, 0)))
    kern = functools.partial(_att_sample_body, t_new=t_new)
    return pl.pallas_call(
        kern,
        grid=(batch,),
        in_specs=[pl.BlockSpec((None, t_new, 3) + he, lambda b: (b, 0, 0, 0, 0))] + specs,
        out_specs=pl.BlockSpec((None, t_new) + he, lambda b: (b, 0, 0, 0)),
        out_shape=jax.ShapeDtypeStruct((batch, t_new) + he, f32),
        compiler_params=_cparams("parallel"),
        name="att_sample",
    )(qkv_s, *views)


def _outproj_body(x_ref, go_ref, gg01_ref, gg23_ref, att_ref, gnw_ref, anw_ref, wg_ref, wa_ref, out_ref):
    parts = []
    for h in range(N_GLA_HEADS):
        vs = slice(h * GLA_DV, (h + 1) * GLA_DV)
        y = _rms(go_ref[:, vs], gnw_ref[...])
        gate = (gg01_ref, gg23_ref)[h // 2][:, (h % 2) * GLA_DV:(h % 2 + 1) * GLA_DV]
        parts.append((y * (gate * jax.nn.sigmoid(gate))).astype(bf16))
    gla_part = jnp.concatenate(parts, axis=-1)
    att = jnp.concatenate([att_ref[h] for h in range(N_ATT_HEADS)], axis=-1)
    att_part = _rms(att, anw_ref[...]).astype(bf16)
    mix = jnp.dot(gla_part, wg_ref[...], preferred_element_type=f32)
    mix = mix + jnp.dot(att_part, wa_ref[...], preferred_element_type=f32)
    out_ref[...] = x_ref[...] + mix


def _outproj(x, gla_o, gg_src, gg_tile, att_o, gla_norm_w, att_norm_w, w_out_bf, tm):
    n = x.shape[0]
    row = lambda i: (i, 0)
    const = lambda i: (0, 0)
    in_specs = [
        pl.BlockSpec((tm, D_MODEL), row),
        pl.BlockSpec((tm, GLA_WIDTH), row),
        pl.BlockSpec((None, tm, PROJ_TN), lambda i: (gg_tile, i, 0)),
        pl.BlockSpec((None, tm, PROJ_TN), lambda i: (gg_tile + 1, i, 0)),
        pl.BlockSpec((N_ATT_HEADS, tm, ATT_HEAD_DIM), lambda i: (0, i, 0)),
        pl.BlockSpec((1, GLA_DV), const),
        pl.BlockSpec((1, ATT_WIDTH), const),
        pl.BlockSpec((GLA_WIDTH, D_MODEL), lambda i: (0, 0)),
        pl.BlockSpec((ATT_WIDTH, D_MODEL), lambda i: (1, 0)),
    ]
    return pl.pallas_call(
        _outproj_body,
        grid=(n // tm,),
        in_specs=in_specs,
        out_specs=pl.BlockSpec((tm, D_MODEL), row),
        out_shape=jax.ShapeDtypeStruct((n, D_MODEL), f32),
        compiler_params=_cparams("parallel"),
        name="mixer_out_proj",
    )(x, gla_o, gg_src, gg_src, att_o, gla_norm_w.reshape(1, GLA_DV), att_norm_w.reshape(1, ATT_WIDTH),
      w_out_bf, w_out_bf)


def _mlp_body(x_ref, fw_ref, wu_ref, wd_ref, nw_ref, o_ref, *rest, emit_bf16):
    h_scr = rest[-1]
    f = pl.program_id(1)

    @pl.when(f == 0)
    def _():
        h_scr[...] = _rms(x_ref[...], fw_ref[...]).astype(bf16)
        o_ref[...] = jnp.zeros_like(o_ref)

    wu, wd = wu_ref[...], wd_ref[...]
    if emit_bf16:
        wu, wd = wu.astype(bf16), wd.astype(bf16)
        rest[0][...] = wu
        rest[1][...] = wd
    u = jnp.dot(h_scr[...], wu, preferred_element_type=f32)
    a = jnp.square(jnp.maximum(u, 0.0)).astype(bf16)
    o_ref[...] += jnp.dot(a, wd, preferred_element_type=f32)

    @pl.when(f == pl.num_programs(1) - 1)
    def _():
        o_ref[...] = _rms(x_ref[...] + o_ref[...], nw_ref[...])


def _mlp(x, ffn_norm_w, w_up, w_down, final_norm_w, tm, tf, emit_bf16=False):
    n = x.shape[0]
    assert not emit_bf16 or n == tm
    out_specs = [pl.BlockSpec((tm, D_MODEL), lambda i, f: (i, 0))]
    out_shape = [jax.ShapeDtypeStruct((n, D_MODEL), f32)]
    if emit_bf16:
        out_specs += [pl.BlockSpec((D_MODEL, tf), lambda i, f: (0, f)), pl.BlockSpec((tf, D_MODEL), lambda i, f: (f, 0))]
        out_shape += [jax.ShapeDtypeStruct((D_MODEL, D_FF), bf16), jax.ShapeDtypeStruct((D_FF, D_MODEL), bf16)]
    res = pl.pallas_call(
        functools.partial(_mlp_body, emit_bf16=emit_bf16),
        grid=(n // tm, D_FF // tf),
        in_specs=[
            pl.BlockSpec((tm, D_MODEL), lambda i, f: (i, 0)),
            pl.BlockSpec((1, D_MODEL), lambda i, f: (0, 0)),
            pl.BlockSpec((D_MODEL, tf), lambda i, f: (0, f)),
            pl.BlockSpec((tf, D_MODEL), lambda i, f: (f, 0)),
            pl.BlockSpec((1, D_MODEL), lambda i, f: (0, 0)),
        ],
        out_specs=out_specs,
        out_shape=out_shape,
        scratch_shapes=[pltpu.VMEM((tm, D_MODEL), bf16)],
        compiler_params=_cparams("parallel", "arbitrary"),
        name="mlp_final_norm",
    )(x, ffn_norm_w.reshape(1, D_MODEL), w_up, w_down, final_norm_w.reshape(1, D_MODEL))
    return res if emit_bf16 else res[0]


def kernel(x_prompt, x_sample, cache_k_win, cache_v_win, state_gla, attn_norm_w, w_in, w_gk_up, b_gk, gla_norm_w,
           att_out_norm_w, w_out, ffn_norm_w, w_up, w_down, final_norm_w):
    depth = w_in.shape[0]
    assert depth == 1, "single trunk layer"
    B, S, _ = x_prompt.shape
    Bs, Ts, _ = x_sample.shape
    w_p = min(MAX_WINDOW, S)
    assert S % (ATT_BLOCK * max(DILATIONS)) == 0 and Ts <= GLA_CHUNK

    w_in0 = jnp.swapaxes(w_in[0], 0, 1)
    w_gla_bf, w_att_bf = _split_w_in(w_in0, 256)
    wup_pad = jnp.pad(w_gk_up[0], ((0, LANES - GLA_GATE_RANK), (0, 0)))
    w_out_bf = w_out[0].astype(bf16)

    xp = x_prompt.reshape(B * S, D_MODEL)
    pg, loga, pa = _project(xp, attn_norm_w[0], w_gla_bf, w_att_bf, w_in0, wup_pad, b_gk[0], tm=1024, head_major=True)
    k_win, v_win = _window_kv(pa, batch=B, seq=S, window=w_p, tr=512)
    s0 = jnp.zeros((B, N_GLA_HEADS, GLA_DK, GLA_DV), f32)
    gla_o, gla_state_p = _gla(pg, loga, s0, batch=B, t_len=S, tb=512, t_valid=GLA_CHUNK)
    att = _att_prompt(pa, batch=B, seq=S)
    x1 = _outproj(xp, gla_o, pg, 4, att, gla_norm_w[0], att_out_norm_w[0], w_out_bf, 512)

    P = SAMPLE_PAD
    xs_pad = jnp.pad(x_sample, ((0, 0), (0, P - Ts), (0, 0))).reshape(Bs * P, D_MODEL)
    pgs, logas, pas = _project(xs_pad, attn_norm_w[0], w_gla_bf, w_att_bf, w_in0, wup_pad, b_gk[0], tm=Bs * P,
                               head_major=False)
    qkv_s = pas.reshape(Bs, P, 3, N_ATT_HEADS, ATT_HEAD_DIM)[:, :Ts]
    gla_o_s, gla_state_s = _gla(pgs, logas, state_gla[0], batch=Bs, t_len=P, tb=P, t_valid=Ts, nseq=8)
    att_s = _att_sample(qkv_s, cache_k_win[0], cache_v_win[0], batch=Bs, t_new=Ts)
    att_s = att_s.reshape(Bs * Ts, N_ATT_HEADS, ATT_HEAD_DIM).swapaxes(0, 1)
    gla_o_s = gla_o_s.reshape(Bs, P, GLA_WIDTH)[:, :Ts].reshape(Bs * Ts, GLA_WIDTH)
    gg_s = pgs[4:].reshape(2, Bs, P, PROJ_TN)[:, :, :Ts].reshape(2, Bs * Ts, PROJ_TN)
    xs = x_sample.reshape(Bs * Ts, D_MODEL)
    x1s = _outproj(xs, gla_o_s, gg_s, 0, att_s, gla_norm_w[0], att_out_norm_w[0], w_out_bf, Bs * Ts)
    y_sample, w_up_bf, w_down_bf = _mlp(x1s, ffn_norm_w[0], w_up[0], w_down[0], final_norm_w, Bs * Ts, 512,
                                        emit_bf16=True)
    y_sample = y_sample.reshape(Bs, Ts, D_MODEL)
    y_prompt = _mlp(x1, ffn_norm_w[0], w_up_bf, w_down_bf, final_norm_w, 1024, 512).reshape(B, S, D_MODEL)

    k_win_prompt = k_win.reshape(1, B, w_p, N_ATT_HEADS, ATT_HEAD_DIM)
    v_win_prompt = v_win.reshape(1, B, w_p, N_ATT_HEADS, ATT_HEAD_DIM)
    k_new_sample = qkv_s[:, :, 1][None]
    v_new_sample = qkv_s[:, :, 2][None]
    return (y_prompt, y_sample, k_win_prompt, v_win_prompt, gla_state_p[None], k_new_sample, v_new_sample,
            gla_state_s[None])
```

```python
import functools

import jax
import jax.numpy as jnp
from jax import lax
from jax.experimental import pallas as pl
from jax.experimental.pallas import tpu as pltpu

f32 = jnp.float32
bf16 = jnp.bfloat16

D_MODEL = 2048
N_GLA_HEADS = 4
GLA_DK = 128
GLA_DV = 256
GLA_KEY_WIDTH = N_GLA_HEADS * GLA_DK
GLA_WIDTH = N_GLA_HEADS * GLA_DV
GLA_GATE_RANK = 16
GLA_GATE_NORM = 16.0
GLA_CHUNK = 16
N_ATT_HEADS = 8
ATT_HEAD_DIM = 128
ATT_WIDTH = N_ATT_HEADS * ATT_HEAD_DIM
DILATIONS = (1, 4, 16)
SUB_WINDOW = 128
ATT_BLOCK = 128
MAX_WINDOW = 2048
D_FF = 4 * D_MODEL
RMS_EPS = 1e-6
GLA_COLS = 2 * GLA_KEY_WIDTH + 2 * GLA_WIDTH
ATT_COLS = 3 * ATT_WIDTH
LANES = 128
VMEM_LIMIT = 56 * 1024 * 1024
NEG_INF = float("-inf")
ATT_SCALE = ATT_HEAD_DIM ** -0.5

NT_DIMS = (((1,), (1,)), ((), ()))
TN_DIMS = (((0,), (0,)), ((), ()))


def _cparams(*sem):
    return pltpu.CompilerParams(dimension_semantics=sem, vmem_limit_bytes=VMEM_LIMIT)


def _rms(x, w):
    r = lax.rsqrt(jnp.mean(x * x, axis=-1, keepdims=True) + RMS_EPS)
    return (x * r) * w


def _split_w_in_body(w_ref, g_ref, a_ref):
    g_ref[...] = w_ref[:GLA_COLS, :].astype(bf16)
    a_ref[...] = w_ref[GLA_COLS + GLA_GATE_RANK:, :].astype(bf16)


def _split_w_in(w_in_t, cols):
    rows, d = w_in_t.shape
    assert rows == GLA_COLS + GLA_GATE_RANK + ATT_COLS and d % cols == 0
    return pl.pallas_call(
        _split_w_in_body,
        grid=(d // cols,),
        in_specs=[pl.BlockSpec((rows, cols), lambda i: (0, i))],
        out_specs=[pl.BlockSpec((GLA_COLS, cols), lambda i: (0, i)), pl.BlockSpec((ATT_COLS, cols), lambda i: (0, i))],
        out_shape=[jax.ShapeDtypeStruct((GLA_COLS, d), bf16), jax.ShapeDtypeStruct((ATT_COLS, d), bf16)],
        compiler_params=_cparams("parallel"),
        name="split_w_in",
    )(w_in_t)


PROJ_TN = 512
N_GLA_TILES = GLA_COLS // PROJ_TN
N_ATT_TILES = ATT_COLS // PROJ_TN
HEADS_PER_TILE = PROJ_TN // ATT_HEAD_DIM


def _proj_body(x_ref, nw_ref, wg_ref, wa_ref, wlr_ref, wup_ref, bgk_ref, pg_ref, loga_ref, pa_ref, h_scr, *,
               head_major):
    j = pl.program_id(1)

    @pl.when(j == 0)
    def _():
        h = _rms(x_ref[...], nw_ref[...]).astype(bf16)
        h_scr[...] = h
        glr = lax.dot_general(h, wlr_ref[...].astype(bf16), NT_DIMS, preferred_element_type=f32)
        R = GLA_GATE_RANK
        lane = lax.broadcasted_iota(jnp.int32, glr.shape, 1)
        g = jnp.where(lane < R, glr, 0.0)
        g_hi = g.astype(bf16).astype(f32)
        g3 = (g_hi + pltpu.roll(g_hi, R, 1) + pltpu.roll(g - g_hi, 2 * R, 1)).astype(bf16)
        w = wup_ref[...]
        w_hi = w.astype(bf16).astype(f32)
        w3 = (w_hi + pltpu.roll(w - w_hi, R, 0) + pltpu.roll(w_hi, 2 * R, 0)).astype(bf16)
        pre = jnp.dot(g3, w3, preferred_element_type=f32) + bgk_ref[...]
        log_sig = jnp.minimum(pre, 0.0) - jnp.log1p(jnp.exp(-jnp.abs(pre)))
        loga_ref[...] = log_sig * (1.0 / GLA_GATE_NORM)

    @pl.when(j < N_GLA_TILES)
    def _():
        pg_ref[...] = lax.dot_general(h_scr[...], wg_ref[...], NT_DIMS, preferred_element_type=f32)

    @pl.when(j >= N_GLA_TILES)
    def _():
        res = lax.dot_general(h_scr[...], wa_ref[...], NT_DIMS, preferred_element_type=f32)
        if head_major:
            for g in range(HEADS_PER_TILE):
                pa_ref[g] = res[:, g * LANES:(g + 1) * LANES]
        else:
            pa_ref[...] = res


def _project(x2d, norm_w, w_gla_bf, w_att_bf, w_in0, wup_pad, b_gk, *, tm, head_major):
    assert GLA_KEY_WIDTH == PROJ_TN and 2 * GLA_DV == PROJ_TN
    n, d = x2d.shape
    nb = n // tm
    nj = N_GLA_TILES + N_ATT_TILES
    gla_j = lambda j: jnp.minimum(j, N_GLA_TILES - 1)
    att_j = lambda j: jnp.maximum(j - N_GLA_TILES, 0)
    assert GLA_COLS % LANES == 0
    in_specs = [
        pl.BlockSpec((tm, d), lambda i, j: (i, 0)),
        pl.BlockSpec((1, d), lambda i, j: (0, 0)),
        pl.BlockSpec((PROJ_TN, d), lambda i, j: (gla_j(j), 0)),
        pl.BlockSpec((PROJ_TN, d), lambda i, j: (att_j(j), 0)),
        pl.BlockSpec((LANES, d), lambda i, j: (GLA_COLS // LANES, 0)),
        pl.BlockSpec((LANES, GLA_KEY_WIDTH), lambda i, j: (0, 0)),
        pl.BlockSpec((1, GLA_KEY_WIDTH), lambda i, j: (0, 0)),
    ]
    out_specs = [
        pl.BlockSpec((None, tm, PROJ_TN), lambda i, j: (gla_j(j), i, 0)),
        pl.BlockSpec((tm, GLA_KEY_WIDTH), lambda i, j: (i, 0)),
    ]
    out_shape = [jax.ShapeDtypeStruct((N_GLA_TILES, n, PROJ_TN), f32), jax.ShapeDtypeStruct((n, GLA_KEY_WIDTH), f32)]
    if head_major:
        out_specs.append(pl.BlockSpec((HEADS_PER_TILE, tm, LANES), lambda i, j: (att_j(j), i, 0)))
        out_shape.append(jax.ShapeDtypeStruct((3 * N_ATT_HEADS, n, ATT_HEAD_DIM), f32))
    else:
        out_specs.append(pl.BlockSpec((tm, PROJ_TN), lambda i, j: (i, att_j(j))))
        out_shape.append(jax.ShapeDtypeStruct((n, ATT_COLS), f32))
    return pl.pallas_call(
        functools.partial(_proj_body, head_major=head_major),
        grid=(nb, nj),
        in_specs=in_specs,
        out_specs=out_specs,
        out_shape=out_shape,
        scratch_shapes=[pltpu.VMEM((tm, d), bf16)],
        compiler_params=_cparams("parallel", "arbitrary"),
        name="norm_in_proj",
    )(x2d, norm_w.reshape(1, d), w_gla_bf, w_att_bf, w_in0, wup_pad, b_gk.reshape(1, GLA_KEY_WIDTH))


def _window_body(kh_ref, vh_ref, ko_ref, vo_ref):
    tr = kh_ref.shape[1]
    for src, dst in ((kh_ref, ko_ref), (vh_ref, vo_ref)):
        for hd in range(N_ATT_HEADS):
            dst[pl.ds(hd, tr, stride=N_ATT_HEADS), :] = src[hd]


def _window_kv(pa_hm, *, batch, seq, window, tr):
    H, E = N_ATT_HEADS, ATT_HEAD_DIM
    assert seq % tr == 0 and window % tr == 0
    per_seq, first = seq // tr, (seq - window) // tr
    nw = window // tr
    src = lambda which: pl.BlockSpec((H, tr, E), lambda b, t: (which, b * per_seq + first + t, 0))
    dst = pl.BlockSpec((tr * H, E), lambda b, t: (b * nw + t, 0))
    shape = jax.ShapeDtypeStruct((batch * window * H, E), f32)
    return pl.pallas_call(
        _window_body,
        grid=(batch, nw),
        in_specs=[src(1), src(2)],
        out_specs=[dst, dst],
        out_shape=[shape, shape],
        compiler_params=_cparams("parallel", "parallel"),
        name="kv_window",
    )(pa_hm, pa_hm)


GLA_SUB = 128


def _gla_body(q_ref, k_ref, v01_ref, v23_ref, g_ref, s0_ref, o_ref, sfin_ref, st_scr, bp_scr, *, tb, t_valid, nseq):
    C = GLA_CHUNK
    t_blk = pl.program_id(1)

    @pl.when(t_blk == 0)
    def _():
        for sq in range(nseq):
            for h in range(N_GLA_HEADS):
                st_scr[sq, h] = s0_ref[sq, h].T

    U = min(GLA_SUB, tb)
    levels = [w for w in (64, 32, 16, 8, 4, 2, 1) if w < U]
    row5 = lax.broadcasted_iota(jnp.int32, (U, GLA_KEY_WIDTH), 0)
    row1 = lax.broadcasted_iota(jnp.int32, (U, GLA_DK), 0)
    rowa = lax.broadcasted_iota(jnp.int32, (U, U), 0)
    cola = lax.broadcasted_iota(jnp.int32, (U, U), 1)
    ltri = (rowa >= cola).astype(bf16)
    off_diag = rowa != cola
    same_group = {w: ((rowa ^ cola) < 2 * w) & off_diag for w in levels}

    for sb in range(nseq * (tb // U)):
        r0 = sb * U
        sq = sb // (tb // U)
        g = g_ref[r0:r0 + U, :]
        if t_valid < C:
            g = jnp.where((row5 & (C - 1)) < t_valid, g, 0.0)
        g1 = g.astype(bf16)
        e1 = g - g1.astype(f32)
        g2 = e1.astype(bf16)
        g3 = (e1 - g2.astype(f32)).astype(bf16)
        bp = (jnp.dot(ltri, g1, preferred_element_type=f32) + jnp.dot(ltri, g2, preferred_element_type=f32)
              + jnp.dot(ltri, g3, preferred_element_type=f32))
        bp_scr[sq] = bp
        b_last = bp[U - 1:U, :]
        q = q_ref[r0:r0 + U, :] * (GLA_DK ** -0.5)
        k = k_ref[r0:r0 + U, :]
        qe = q * jnp.exp(bp)
        ke = k * jnp.exp(b_last - bp)
        dec = jnp.exp(b_last)
        facs = []
        for w in levels:
            if w >= 4:
                bc = jnp.concatenate([jnp.broadcast_to(bp_scr[sq, s + w - 1:s + w, :], (2 * w, GLA_KEY_WIDTH))
                                      for s in range(0, U, 2 * w)], axis=0)
            elif w == 2:
                pos = row5 & 3
                bc = jnp.where(pos == 0, pltpu.roll(bp, U - 1, 0),
                               jnp.where(pos == 1, bp, jnp.where(pos == 2, pltpu.roll(bp, 1, 0), pltpu.roll(bp, 2, 0))))
            else:
                bc = jnp.where((row5 & 1) == 1, pltpu.roll(bp, 1, 0), bp)
            facs.append(jnp.exp(jnp.where((row5 & w) != 0, bp - bc, bc - bp)))
        for h in range(N_GLA_HEADS):
            ks = slice(h * GLA_DK, (h + 1) * GLA_DK)
            vs = slice(h * GLA_DV, (h + 1) * GLA_DV)
            qh, kh = q[:, ks], k[:, ks]
            a = lax.dot_general(qh.astype(bf16), kh.astype(bf16), NT_DIMS, preferred_element_type=f32)
            a = jnp.where(rowa == cola, a, 0.0)
            for w, fac in zip(levels, facs):
                upper = (row1 & w) != 0
                fh = fac[:, ks]
                rq = jnp.where(upper, qh * fh, 0.0).astype(bf16)
                ck = jnp.where(upper, 0.0, kh * fh).astype(bf16)
                p = lax.dot_general(rq, ck, NT_DIMS, preferred_element_type=f32)
                a = jnp.where(same_group[w], p, a)
            v_ref = (v01_ref, v23_ref)[h // 2]
            vh = v_ref[r0:r0 + U, (h % 2) * GLA_DV:(h % 2 + 1) * GLA_DV].astype(bf16)
            st = st_scr[sq, h]
            o = jnp.dot(a.astype(bf16), vh, preferred_element_type=f32)
            o = o + lax.dot_general(qe[:, ks].astype(bf16), st.astype(bf16), NT_DIMS, preferred_element_type=f32)
            o_ref[r0:r0 + U, vs] = o
            kv_t = lax.dot_general(vh, ke[:, ks].astype(bf16), TN_DIMS, preferred_element_type=f32)
            st_scr[sq, h] = dec[:, ks] * st + kv_t

    @pl.when(t_blk == pl.num_programs(1) - 1)
    def _():
        for sq in range(nseq):
            for h in range(N_GLA_HEADS):
                sfin_ref[sq, h] = st_scr[sq, h].T


def _gla(pg, loga, s0, *, batch, t_len, tb, t_valid, nseq=1):
    nt = t_len // tb
    assert batch % nseq == 0 and (nseq == 1 or nt == 1)
    rb = nseq * tb
    kern = functools.partial(_gla_body, tb=tb, t_valid=t_valid, nseq=nseq)
    state_spec = pl.BlockSpec((nseq, N_GLA_HEADS, GLA_DK, GLA_DV), lambda b, t: (b, 0, 0, 0))
    return pl.pallas_call(
        kern,
        grid=(batch // nseq, nt),
        in_specs=[
            pl.BlockSpec((None, rb, PROJ_TN), lambda b, t: (0, b * nt + t, 0)),
            pl.BlockSpec((None, rb, PROJ_TN), lambda b, t: (1, b * nt + t, 0)),
            pl.BlockSpec((None, rb, PROJ_TN), lambda b, t: (2, b * nt + t, 0)),
            pl.BlockSpec((None, rb, PROJ_TN), lambda b, t: (3, b * nt + t, 0)),
            pl.BlockSpec((rb, GLA_KEY_WIDTH), lambda b, t: (b * nt + t, 0)),
            state_spec,
        ],
        out_specs=[pl.BlockSpec((rb, GLA_WIDTH), lambda b, t: (b * nt + t, 0)), state_spec],
        out_shape=[
            jax.ShapeDtypeStruct((batch * t_len, GLA_WIDTH), f32),
            jax.ShapeDtypeStruct((batch, N_GLA_HEADS, GLA_DK, GLA_DV), f32),
        ],
        scratch_shapes=[
            pltpu.VMEM((nseq, N_GLA_HEADS, GLA_DV, GLA_DK), f32),
            pltpu.VMEM((nseq, min(GLA_SUB, tb), GLA_KEY_WIDTH), f32),
        ],
        compiler_params=_cparams("parallel", "arbitrary"),
        name="gla",
    )(pg, pg, pg, pg, loga, s0)


ATT_ROWS = ATT_BLOCK * max(DILATIONS)


assert DILATIONS == (1, 4, 16)


def _att_prompt_body(q_ref, k_ref, v_ref, o_ref, qd_scr, kd1, vd1, kd4, vd4, kd16, vd16, od_scr, lse_scr, by4_scr):
    U = ATT_BLOCK
    h = pl.program_id(1)
    n = pl.program_id(2)
    kds, vds = (kd1, kd4, kd16), (vd1, vd4, vd16)

    @pl.when(n == 0)
    def _():
        for d, kd, vd in zip(DILATIONS, kds, vds):
            L = ATT_ROWS // d
            for r in range(d):
                kd[r * (U + L):r * (U + L) + U, :] = jnp.zeros((U, ATT_HEAD_DIM), bf16)
                vd[r * (U + L):r * (U + L) + U, :] = jnp.zeros((U, ATT_HEAD_DIM), bf16)

    qi = lax.broadcasted_iota(jnp.int32, (U, 2 * U), 0)
    ki = lax.broadcasted_iota(jnp.int32, (U, 2 * U), 1)
    dist = qi - ki + U
    in_window = (dist >= 0) & (dist <= SUB_WINDOW)
    slope = jnp.exp2(jnp.zeros((U, 2 * U), f32) - (h + 1).astype(f32) * (8.0 / N_ATT_HEADS))
    first_pen = jnp.where(n == 0, NEG_INF, 0.0)

    for di, d in enumerate(DILATIONS):
        L = ATT_ROWS // d
        nu = L // U
        kd, vd = kds[di], vds[di]
        bias = jnp.where(in_window, -(slope * float(d)) * dist.astype(f32), NEG_INF)
        for r in range(d):
            base = r * (U + L)
            for src_ref, by4, dst, off, scale in ((q_ref, by4_scr.at[0], qd_scr, r * L, ATT_SCALE),
                                                  (k_ref, by4_scr.at[1], kd, base + U, None),
                                                  (v_ref, by4_scr.at[2], vd, base + U, None)):
                if d == 1:
                    x = src_ref[...]
                elif d == 4:
                    x = src_ref[pl.ds(r, L, stride=4), :]
                    by4[r * L:(r + 1) * L, :] = x
                else:
                    x = by4[pl.ds((r % 4) * (ATT_ROWS // 4) + r // 4, L, stride=4), :]
                if scale is not None:
                    x = x * scale
                dst[off:off + L, :] = x.astype(bf16)

        bias_first = bias + jnp.where(ki < U, first_pen, 0.0)
        for r in range(d):
            for u in range(nu):
                qoff = r * L + u * U
                koff = r * (U + L) + u * U
                qq = qd_scr[qoff:qoff + U, :]
                kk = kd[koff:koff + 2 * U, :]
                vv = vd[koff:koff + 2 * U, :]
                s = lax.dot_general(qq, kk, NT_DIMS, preferred_element_type=f32) + (bias_first if u == 0 else bias)
                m = jnp.max(s, axis=-1, keepdims=True)
                p = jnp.exp(s - m)
                l = jnp.sum(p, axis=-1, keepdims=True)
                o = jnp.dot(p.astype(bf16), vv, preferred_element_type=f32) / l
                rows = pl.ds(u * (U * d) + r, U, stride=d) if d > 1 else slice(u * U, (u + 1) * U)
                od_scr[di, rows, :] = o
                lse_scr[di, rows, :] = jnp.broadcast_to(m + jnp.log(l), (U, ATT_HEAD_DIM))
        for r in range(d):
            base = r * (U + L)
            kd[base:base + U, :] = kd[base + L:base + L + U, :]
            vd[base:base + U, :] = vd[base + L:base + L + U, :]

    CH = 512
    for c in range(ATT_ROWS // CH):
        rs = slice(c * CH, (c + 1) * CH)
        ls = [lse_scr[di, rs, :] for di in range(len(DILATIONS))]
        mm = jnp.maximum(jnp.maximum(ls[0], ls[1]), ls[2])
        ws = [jnp.exp(x - mm) for x in ls]
        num = ws[0] * od_scr[0, rs, :] + ws[1] * od_scr[1, rs, :] + ws[2] * od_scr[2, rs, :]
        o_ref[rs, :] = num / (ws[0] + ws[1] + ws[2])


def _att_prompt(pa_hm, *, batch, seq):
    nb = seq // ATT_ROWS
    H, E, U = N_ATT_HEADS, ATT_HEAD_DIM, ATT_BLOCK
    blk = (None, ATT_ROWS, E)
    kv_scratch = []
    for d in DILATIONS:
        kv_scratch += [pltpu.VMEM((d * U + ATT_ROWS, E), bf16)] * 2
    return pl.pallas_call(
        _att_prompt_body,
        grid=(batch, H, nb),
        in_specs=[
            pl.BlockSpec(blk, lambda b, h, n: (h, b * nb + n, 0)),
            pl.BlockSpec(blk, lambda b, h, n: (H + h, b * nb + n, 0)),
            pl.BlockSpec(blk, lambda b, h, n: (2 * H + h, b * nb + n, 0)),
        ],
        out_specs=pl.BlockSpec(blk, lambda b, h, n: (h, b * nb + n, 0)),
        out_shape=jax.ShapeDtypeStruct((H, batch * seq, E), f32),
        scratch_shapes=[pltpu.VMEM((ATT_ROWS, E), bf16)] + kv_scratch + [
            pltpu.VMEM((len(DILATIONS), ATT_ROWS, E), f32),
            pltpu.VMEM((len(DILATIONS), ATT_ROWS, E), f32),
            pltpu.VMEM((3, ATT_ROWS, E), f32),
        ],
        compiler_params=_cparams("parallel", "parallel", "arbitrary"),
        name="att_prompt",
    )(pa_hm, pa_hm, pa_hm)


SAMPLE_PAD = 16


def _att_sample_body(qkv_ref, k1_ref, v1_ref, k4_ref, v4_ref, k16_ref, v16_ref, o_ref, *, t_new):
    J = SUB_WINDOW
    hidx = lax.broadcasted_iota(jnp.int32, (N_ATT_HEADS, 1), 0)
    slope = jnp.exp2((hidx + 1).astype(f32) * (-8.0 / N_ATT_HEADS))
    slot = lax.broadcasted_iota(jnp.int32, (J, N_ATT_HEADS, 1), 0)
    q = [qkv_ref[i, 0] * ATT_SCALE for i in range(t_new)]
    kn = [qkv_ref[i, 1] for i in range(t_new)]
    vn = [qkv_ref[i, 2] for i in range(t_new)]
    far_bias = {dil: (slope * float(dil))[None] * (J - slot).astype(f32) for dil in DILATIONS if dil > 1}
    for i in range(t_new):
        s_new = [jnp.sum(q[i] * kn[n], axis=-1, keepdims=True) for n in range(i + 1)]
        outs, lses = [], []
        for dil in DILATIONS:
            if dil == 1:
                kc, vc = k1_ref[...], v1_ref[...]
                bias = slope[None] * (J + i - slot).astype(f32)
                news = [(s_new[n] - slope * float(i - n), vn[n]) for n in range(i + 1)]
            else:
                kres_ref, vres_ref = (k4_ref, v4_ref) if dil == 4 else (k16_ref, v16_ref)
                kc, vc = kres_ref[:, i], vres_ref[:, i]
                bias = far_bias[dil]
                news = [(s_new[i], vn[i])]
            s = jnp.sum(kc * q[i][None], axis=-1, keepdims=True) - bias
            if dil == 1:
                s = jnp.where(slot >= i, s, NEG_INF)
            m = jnp.max(s, axis=0)
            for sn, _ in news:
                m = jnp.maximum(m, sn)
            p = jnp.exp(s - m[None])
            l = jnp.sum(p, axis=0)
            acc = jnp.sum(p * vc, axis=0)
            for sn, v in news:
                pn = jnp.exp(sn - m)
                l = l + pn
                acc = acc + pn * v
            outs.append(acc / l)
            lses.append(m + jnp.log(l))
        mm = jnp.maximum(jnp.maximum(lses[0], lses[1]), lses[2])
        ws = [jnp.exp(x - mm) for x in lses]
        tot = ws[0] + ws[1] + ws[2]
        o_ref[i] = (ws[0] * outs[0] + ws[1] * outs[1] + ws[2] * outs[2]) / tot


def _att_sample(qkv_s, cache_k, cache_v, *, batch, t_new):
    w_cache = cache_k.shape[1]
    J = SUB_WINDOW
    he = (N_ATT_HEADS, ATT_HEAD_DIM)
    assert w_cache == max(DILATIONS) * J and t_new <= 4
    specs, views = [], []
    for dil in DILATIONS:
        rows = w_cache // dil
        last = rows // J - 1
        for c in (cache_k, cache_v):
            if dil == 1:
                views.append(c)
                specs.append(pl.BlockSpec((None, J) + he, lambda b, last=last: (b, last, 0, 0)))
            else:
                views.append(c.reshape((batch, rows, dil) + he))
                specs.append(pl.BlockSpec((None, J, 4) + he, lambda b, last=last: (b, last, 0, 0, 0)))
    kern = functools.partial(_att_sample_body, t_new=t_new)
    return pl.pallas_call(
        kern,
        grid=(batch,),
        in_specs=[pl.BlockSpec((None, t_new, 3) + he, lambda b: (b, 0, 0, 0, 0))] + specs,
        out_specs=pl.BlockSpec((None, t_new) + he, lambda b: (b, 0, 0, 0)),
        out_shape=jax.ShapeDtypeStruct((batch, t_new) + he, f32),
        compiler_params=_cparams("parallel"),
        name="att_sample",
    )(qkv_s, *views)


def _outproj_body(x_ref, go_ref, gg01_ref, gg23_ref, att_ref, gnw_ref, anw_ref, wg_ref, wa_ref, out_ref):
    parts = []
    for h in range(N_GLA_HEADS):
        vs = slice(h * GLA_DV, (h + 1) * GLA_DV)
        y = _rms(go_ref[:, vs], gnw_ref[...])
        gate = (gg01_ref, gg23_ref)[h // 2][:, (h % 2) * GLA_DV:(h % 2 + 1) * GLA_DV]
        parts.append((y * (gate * jax.nn.sigmoid(gate))).astype(bf16))
    gla_part = jnp.concatenate(parts, axis=-1)
    att = jnp.concatenate([att_ref[h] for h in range(N_ATT_HEADS)], axis=-1)
    att_part = _rms(att, anw_ref[...]).astype(bf16)
    mix = jnp.dot(gla_part, wg_ref[...], preferred_element_type=f32)
    mix = mix + jnp.dot(att_part, wa_ref[...], preferred_element_type=f32)
    out_ref[...] = x_ref[...] + mix


def _outproj(x, gla_o, gg_src, gg_tile, att_o, gla_norm_w, att_norm_w, w_out_bf, tm):
    n = x.shape[0]
    row = lambda i: (i, 0)
    const = lambda i: (0, 0)
    in_specs = [
        pl.BlockSpec((tm, D_MODEL), row),
        pl.BlockSpec((tm, GLA_WIDTH), row),
        pl.BlockSpec((None, tm, PROJ_TN), lambda i: (gg_tile, i, 0)),
        pl.BlockSpec((None, tm, PROJ_TN), lambda i: (gg_tile + 1, i, 0)),
        pl.BlockSpec((N_ATT_HEADS, tm, ATT_HEAD_DIM), lambda i: (0, i, 0)),
        pl.BlockSpec((1, GLA_DV), const),
        pl.BlockSpec((1, ATT_WIDTH), const),
        pl.BlockSpec((GLA_WIDTH, D_MODEL), lambda i: (0, 0)),
        pl.BlockSpec((ATT_WIDTH, D_MODEL), lambda i: (1, 0)),
    ]
    return pl.pallas_call(
        _outproj_body,
        grid=(n // tm,),
        in_specs=in_specs,
        out_specs=pl.BlockSpec((tm, D_MODEL), row),
        out_shape=jax.ShapeDtypeStruct((n, D_MODEL), f32),
        compiler_params=_cparams("parallel"),
        name="mixer_out_proj",
    )(x, gla_o, gg_src, gg_src, att_o, gla_norm_w.reshape(1, GLA_DV), att_norm_w.reshape(1, ATT_WIDTH),
      w_out_bf, w_out_bf)


def _mlp_body(x_ref, fw_ref, wu_ref, wd_ref, nw_ref, o_ref, *rest, emit_bf16):
    h_scr = rest[-1]
    f = pl.program_id(1)

    @pl.when(f == 0)
    def _():
        h_scr[...] = _rms(x_ref[...], fw_ref[...]).astype(bf16)
        o_ref[...] = jnp.zeros_like(o_ref)

    wu, wd = wu_ref[...], wd_ref[...]
    if emit_bf16:
        wu, wd = wu.astype(bf16), wd.astype(bf16)
        rest[0][...] = wu
        rest[1][...] = wd
    u = jnp.dot(h_scr[...], wu, preferred_element_type=f32)
    a = jnp.square(jnp.maximum(u, 0.0)).astype(bf16)
    o_ref[...] += jnp.dot(a, wd, preferred_element_type=f32)

    @pl.when(f == pl.num_programs(1) - 1)
    def _():
        o_ref[...] = _rms(x_ref[...] + o_ref[...], nw_ref[...])


def _mlp(x, ffn_norm_w, w_up, w_down, final_norm_w, tm, tf, emit_bf16=False):
    n = x.shape[0]
    assert not emit_bf16 or n == tm
    out_specs = [pl.BlockSpec((tm, D_MODEL), lambda i, f: (i, 0))]
    out_shape = [jax.ShapeDtypeStruct((n, D_MODEL), f32)]
    if emit_bf16:
        out_specs += [pl.BlockSpec((D_MODEL, tf), lambda i, f: (0, f)), pl.BlockSpec((tf, D_MODEL), lambda i, f: (f, 0))]
        out_shape += [jax.ShapeDtypeStruct((D_MODEL, D_FF), bf16), jax.ShapeDtypeStruct((D_FF, D_MODEL), bf16)]
    res = pl.pallas_call(
        functools.partial(_mlp_body, emit_bf16=emit_bf16),
        grid=(n // tm, D_FF // tf),
        in_specs=[
            pl.BlockSpec((tm, D_MODEL), lambda i, f: (i, 0)),
            pl.BlockSpec((1, D_MODEL), lambda i, f: (0, 0)),
            pl.BlockSpec((D_MODEL, tf), lambda i, f: (0, f)),
            pl.BlockSpec((tf, D_MODEL), lambda i, f: (f, 0)),
            pl.BlockSpec((1, D_MODEL), lambda i, f: (0, 0)),
        ],
        out_specs=out_specs,
        out_shape=out_shape,
        scratch_shapes=[pltpu.VMEM((tm, D_MODEL), bf16)],
        compiler_params=_cparams("parallel", "arbitrary"),
        name="mlp_final_norm",
    )(x, ffn_norm_w.reshape(1, D_MODEL), w_up, w_down, final_norm_w.reshape(1, D_MODEL))
    return res if emit_bf16 else res[0]


def kernel(x_prompt, x_sample, cache_k_win, cache_v_win, state_gla, attn_norm_w, w_in, w_gk_up, b_gk, gla_norm_w,
           att_out_norm_w, w_out, ffn_norm_w, w_up, w_down, final_norm_w):
    depth = w_in.shape[0]
    assert depth == 1, "single trunk layer"
    B, S, _ = x_prompt.shape
    Bs, Ts, _ = x_sample.shape
    w_p = min(MAX_WINDOW, S)
    assert S % (ATT_BLOCK * max(DILATIONS)) == 0 and Ts <= GLA_CHUNK

    w_in0 = jnp.swapaxes(w_in[0], 0, 1)
    w_gla_bf, w_att_bf = _split_w_in(w_in0, 256)
    wup_pad = jnp.pad(w_gk_up[0], ((0, LANES - GLA_GATE_RANK), (0, 0)))
    w_out_bf = w_out[0].astype(bf16)

    xp = x_prompt.reshape(B * S, D_MODEL)
    pg, loga, pa = _project(xp, attn_norm_w[0], w_gla_bf, w_att_bf, w_in0, wup_pad, b_gk[0], tm=1024, head_major=True)
    k_win, v_win = _window_kv(pa, batch=B, seq=S, window=w_p, tr=512)
    s0 = jnp.zeros((B, N_GLA_HEADS, GLA_DK, GLA_DV), f32)
    gla_o, gla_state_p = _gla(pg, loga, s0, batch=B, t_len=S, tb=512, t_valid=GLA_CHUNK)
    att = _att_prompt(pa, batch=B, seq=S)
    x1 = _outproj(xp, gla_o, pg, 4, att, gla_norm_w[0], att_out_norm_w[0], w_out_bf, 512)

    P = SAMPLE_PAD
    xs_pad = jnp.pad(x_sample, ((0, 0), (0, P - Ts), (0, 0))).reshape(Bs * P, D_MODEL)
    pgs, logas, pas = _project(xs_pad, attn_norm_w[0], w_gla_bf, w_att_bf, w_in0, wup_pad, b_gk[0], tm=Bs * P,
                               head_major=False)
    qkv_s = pas.reshape(Bs, P, 3, N_ATT_HEADS, ATT_HEAD_DIM)[:, :Ts]
    gla_o_s, gla_state_s = _gla(pgs, logas, state_gla[0], batch=Bs, t_len=P, tb=P, t_valid=Ts, nseq=4)
    att_s = _att_sample(qkv_s, cache_k_win[0], cache_v_win[0], batch=Bs, t_new=Ts)
    att_s = att_s.reshape(Bs * Ts, N_ATT_HEADS, ATT_HEAD_DIM).swapaxes(0, 1)
    gla_o_s = gla_o_s.reshape(Bs, P, GLA_WIDTH)[:, :Ts].reshape(Bs * Ts, GLA_WIDTH)
    gg_s = pgs[4:].reshape(2, Bs, P, PROJ_TN)[:, :, :Ts].reshape(2, Bs * Ts, PROJ_TN)
    xs = x_sample.reshape(Bs * Ts, D_MODEL)
    x1s = _outproj(xs, gla_o_s, gg_s, 0, att_s, gla_norm_w[0], att_out_norm_w[0], w_out_bf, Bs * Ts)
    y_sample, w_up_bf, w_down_bf = _mlp(x1s, ffn_norm_w[0], w_up[0], w_down[0], final_norm_w, Bs * Ts, 512,
                                        emit_bf16=True)
    y_sample = y_sample.reshape(Bs, Ts, D_MODEL)
    y_prompt = _mlp(x1, ffn_norm_w[0], w_up_bf, w_down_bf, final_norm_w, 1024, 512).reshape(B, S, D_MODEL)

    k_win_prompt = k_win.reshape(1, B, w_p, N_ATT_HEADS, ATT_HEAD_DIM)
    v_win_prompt = v_win.reshape(1, B, w_p, N_ATT_HEADS, ATT_HEAD_DIM)
    k_new_sample = qkv_s[:, :, 1][None]
    v_new_sample = qkv_s[:, :, 2][None]
    return (y_prompt, y_sample, k_win_prompt, v_win_prompt, gla_state_p[None], k_new_sample, v_new_sample,
            gla_state_s[None])
```

```python
import functools

import jax
import jax.numpy as jnp
from jax import lax
from jax.experimental import pallas as pl
from jax.experimental.pallas import tpu as pltpu

f32 = jnp.float32
bf16 = jnp.bfloat16

D_MODEL = 2048
N_GLA_HEADS = 4
GLA_DK = 128
GLA_DV = 256
GLA_KEY_WIDTH = N_GLA_HEADS * GLA_DK
GLA_WIDTH = N_GLA_HEADS * GLA_DV
GLA_GATE_RANK = 16
GLA_GATE_NORM = 16.0
GLA_CHUNK = 16
N_ATT_HEADS = 8
ATT_HEAD_DIM = 128
ATT_WIDTH = N_ATT_HEADS * ATT_HEAD_DIM
DILATIONS = (1, 4, 16)
SUB_WINDOW = 128
ATT_BLOCK = 128
MAX_WINDOW = 2048
D_FF = 4 * D_MODEL
RMS_EPS = 1e-6
GLA_COLS = 2 * GLA_KEY_WIDTH + 2 * GLA_WIDTH
ATT_COLS = 3 * ATT_WIDTH
LANES = 128
VMEM_LIMIT = 56 * 1024 * 1024
NEG_INF = float("-inf")
ATT_SCALE = ATT_HEAD_DIM ** -0.5

NT_DIMS = (((1,), (1,)), ((), ()))
TN_DIMS = (((0,), (0,)), ((), ()))


def _cparams(*sem):
    return pltpu.CompilerParams(dimension_semantics=sem, vmem_limit_bytes=VMEM_LIMIT)


def _rms(x, w):
    r = lax.rsqrt(jnp.mean(x * x, axis=-1, keepdims=True) + RMS_EPS)
    return (x * r) * w


def _split_w_in_body(w_ref, g_ref, a_ref):
    g_ref[...] = w_ref[:GLA_COLS, :].astype(bf16)
    a_ref[...] = w_ref[GLA_COLS + GLA_GATE_RANK:, :].astype(bf16)


def _split_w_in(w_in_t, cols):
    rows, d = w_in_t.shape
    assert rows == GLA_COLS + GLA_GATE_RANK + ATT_COLS and d % cols == 0
    return pl.pallas_call(
        _split_w_in_body,
        grid=(d // cols,),
        in_specs=[pl.BlockSpec((rows, cols), lambda i: (0, i))],
        out_specs=[pl.BlockSpec((GLA_COLS, cols), lambda i: (0, i)), pl.BlockSpec((ATT_COLS, cols), lambda i: (0, i))],
        out_shape=[jax.ShapeDtypeStruct((GLA_COLS, d), bf16), jax.ShapeDtypeStruct((ATT_COLS, d), bf16)],
        compiler_params=_cparams("parallel"),
        name="split_w_in",
    )(w_in_t)


PROJ_TN = 512
N_GLA_TILES = GLA_COLS // PROJ_TN
N_ATT_TILES = ATT_COLS // PROJ_TN
HEADS_PER_TILE = PROJ_TN // ATT_HEAD_DIM


def _proj_body(x_ref, nw_ref, wg_ref, wa_ref, wlr_ref, wup_ref, bgk_ref, pg_ref, loga_ref, pa_ref, h_scr, *,
               head_major):
    j = pl.program_id(1)

    @pl.when(j == 0)
    def _():
        h = _rms(x_ref[...], nw_ref[...]).astype(bf16)
        h_scr[...] = h
        glr = lax.dot_general(h, wlr_ref[...].astype(bf16), NT_DIMS, preferred_element_type=f32)
        R = GLA_GATE_RANK
        lane = lax.broadcasted_iota(jnp.int32, glr.shape, 1)
        g = jnp.where(lane < R, glr, 0.0)
        g_hi = g.astype(bf16).astype(f32)
        g3 = (g_hi + pltpu.roll(g_hi, R, 1) + pltpu.roll(g - g_hi, 2 * R, 1)).astype(bf16)
        w = wup_ref[...]
        w_hi = w.astype(bf16).astype(f32)
        w3 = (w_hi + pltpu.roll(w - w_hi, R, 0) + pltpu.roll(w_hi, 2 * R, 0)).astype(bf16)
        pre = jnp.dot(g3, w3, preferred_element_type=f32) + bgk_ref[...]
        log_sig = jnp.minimum(pre, 0.0) - jnp.log1p(jnp.exp(-jnp.abs(pre)))
        loga_ref[...] = log_sig * (1.0 / GLA_GATE_NORM)

    @pl.when(j < N_GLA_TILES)
    def _():
        pg_ref[...] = lax.dot_general(h_scr[...], wg_ref[...], NT_DIMS, preferred_element_type=f32)

    @pl.when(j >= N_GLA_TILES)
    def _():
        res = lax.dot_general(h_scr[...], wa_ref[...], NT_DIMS, preferred_element_type=f32)
        if head_major:
            for g in range(HEADS_PER_TILE):
                pa_ref[g] = res[:, g * LANES:(g + 1) * LANES]
        else:
            pa_ref[...] = res


def _project(x2d, norm_w, w_gla_bf, w_att_bf, w_in0, wup_pad, b_gk, *, tm, head_major):
    assert GLA_KEY_WIDTH == PROJ_TN and 2 * GLA_DV == PROJ_TN
    n, d = x2d.shape
    nb = n // tm
    nj = N_GLA_TILES + N_ATT_TILES
    gla_j = lambda j: jnp.minimum(j, N_GLA_TILES - 1)
    att_j = lambda j: jnp.maximum(j - N_GLA_TILES, 0)
    assert GLA_COLS % LANES == 0
    in_specs = [
        pl.BlockSpec((tm, d), lambda i, j: (i, 0)),
        pl.BlockSpec((1, d), lambda i, j: (0, 0)),
        pl.BlockSpec((PROJ_TN, d), lambda i, j: (gla_j(j), 0)),
        pl.BlockSpec((PROJ_TN, d), lambda i, j: (att_j(j), 0)),
        pl.BlockSpec((LANES, d), lambda i, j: (GLA_COLS // LANES, 0)),
        pl.BlockSpec((LANES, GLA_KEY_WIDTH), lambda i, j: (0, 0)),
        pl.BlockSpec((1, GLA_KEY_WIDTH), lambda i, j: (0, 0)),
    ]
    out_specs = [
        pl.BlockSpec((None, tm, PROJ_TN), lambda i, j: (gla_j(j), i, 0)),
        pl.BlockSpec((tm, GLA_KEY_WIDTH), lambda i, j: (i, 0)),
    ]
    out_shape = [jax.ShapeDtypeStruct((N_GLA_TILES, n, PROJ_TN), f32), jax.ShapeDtypeStruct((n, GLA_KEY_WIDTH), f32)]
    if head_major:
        out_specs.append(pl.BlockSpec((HEADS_PER_TILE, tm, LANES), lambda i, j: (att_j(j), i, 0)))
        out_shape.append(jax.ShapeDtypeStruct((3 * N_ATT_HEADS, n, ATT_HEAD_DIM), f32))
    else:
        out_specs.append(pl.BlockSpec((tm, PROJ_TN), lambda i, j: (i, att_j(j))))
        out_shape.append(jax.ShapeDtypeStruct((n, ATT_COLS), f32))
    return pl.pallas_call(
        functools.partial(_proj_body, head_major=head_major),
        grid=(nb, nj),
        in_specs=in_specs,
        out_specs=out_specs,
        out_shape=out_shape,
        scratch_shapes=[pltpu.VMEM((tm, d), bf16)],
        compiler_params=_cparams("parallel", "arbitrary"),
        name="norm_in_proj",
    )(x2d, norm_w.reshape(1, d), w_gla_bf, w_att_bf, w_in0, wup_pad, b_gk.reshape(1, GLA_KEY_WIDTH))


def _window_body(kh_ref, vh_ref, ko_ref, vo_ref):
    tr = kh_ref.shape[1]
    for src, dst in ((kh_ref, ko_ref), (vh_ref, vo_ref)):
        for hd in range(N_ATT_HEADS):
            dst[pl.ds(hd, tr, stride=N_ATT_HEADS), :] = src[hd]


def _window_kv(pa_hm, *, batch, seq, window, tr):
    H, E = N_ATT_HEADS, ATT_HEAD_DIM
    assert seq % tr == 0 and window % tr == 0
    per_seq, first = seq // tr, (seq - window) // tr
    nw = window // tr
    src = lambda which: pl.BlockSpec((H, tr, E), lambda b, t: (which, b * per_seq + first + t, 0))
    dst = pl.BlockSpec((tr * H, E), lambda b, t: (b * nw + t, 0))
    shape = jax.ShapeDtypeStruct((batch * window * H, E), f32)
    return pl.pallas_call(
        _window_body,
        grid=(batch, nw),
        in_specs=[src(1), src(2)],
        out_specs=[dst, dst],
        out_shape=[shape, shape],
        compiler_params=_cparams("parallel", "parallel"),
        name="kv_window",
    )(pa_hm, pa_hm)


GLA_SUB = 128


def _gla_body(q_ref, k_ref, v01_ref, v23_ref, g_ref, s0_ref, o_ref, sfin_ref, st_scr, bp_scr, *, tb, t_valid, nseq):
    C = GLA_CHUNK
    t_blk = pl.program_id(1)

    @pl.when(t_blk == 0)
    def _():
        for sq in range(nseq):
            for h in range(N_GLA_HEADS):
                st_scr[sq, h] = s0_ref[sq, h].T

    U = min(GLA_SUB, tb)
    levels = [w for w in (64, 32, 16, 8, 4, 2, 1) if w < U]
    row5 = lax.broadcasted_iota(jnp.int32, (U, GLA_KEY_WIDTH), 0)
    row1 = lax.broadcasted_iota(jnp.int32, (U, GLA_DK), 0)
    rowa = lax.broadcasted_iota(jnp.int32, (U, U), 0)
    cola = lax.broadcasted_iota(jnp.int32, (U, U), 1)
    ltri = (rowa >= cola).astype(bf16)
    off_diag = rowa != cola
    same_group = {w: ((rowa ^ cola) < 2 * w) & off_diag for w in levels}

    for sb in range(nseq * (tb // U)):
        r0 = sb * U
        sq = sb // (tb // U)
        g = g_ref[r0:r0 + U, :]
        if t_valid < C:
            g = jnp.where((row5 & (C - 1)) < t_valid, g, 0.0)
        g1 = g.astype(bf16)
        e1 = g - g1.astype(f32)
        g2 = e1.astype(bf16)
        g3 = (e1 - g2.astype(f32)).astype(bf16)
        bp = (jnp.dot(ltri, g1, preferred_element_type=f32) + jnp.dot(ltri, g2, preferred_element_type=f32)
              + jnp.dot(ltri, g3, preferred_element_type=f32))
        bp_scr[sq] = bp
        b_last = bp[U - 1:U, :]
        q = q_ref[r0:r0 + U, :] * (GLA_DK ** -0.5)
        k = k_ref[r0:r0 + U, :]
        qe = q * jnp.exp(bp)
        ke = k * jnp.exp(b_last - bp)
        dec = jnp.exp(b_last)
        facs = []
        for w in levels:
            if w >= 4:
                bc = jnp.concatenate([jnp.broadcast_to(bp_scr[sq, s + w - 1:s + w, :], (2 * w, GLA_KEY_WIDTH))
                                      for s in range(0, U, 2 * w)], axis=0)
            elif w == 2:
                pos = row5 & 3
                bc = jnp.where(pos == 0, pltpu.roll(bp, U - 1, 0),
                               jnp.where(pos == 1, bp, jnp.where(pos == 2, pltpu.roll(bp, 1, 0), pltpu.roll(bp, 2, 0))))
            else:
                bc = jnp.where((row5 & 1) == 1, pltpu.roll(bp, 1, 0), bp)
            facs.append(jnp.exp(jnp.where((row5 & w) != 0, bp - bc, bc - bp)))
        for h in range(N_GLA_HEADS):
            ks = slice(h * GLA_DK, (h + 1) * GLA_DK)
            vs = slice(h * GLA_DV, (h + 1) * GLA_DV)
            qh, kh = q[:, ks], k[:, ks]
            a = lax.dot_general(qh.astype(bf16), kh.astype(bf16), NT_DIMS, preferred_element_type=f32)
            a = jnp.where(rowa == cola, a, 0.0)
            for w, fac in zip(levels, facs):
                upper = (row1 & w) != 0
                fh = fac[:, ks]
                rq = jnp.where(upper, qh * fh, 0.0).astype(bf16)
                ck = jnp.where(upper, 0.0, kh * fh).astype(bf16)
                p = lax.dot_general(rq, ck, NT_DIMS, preferred_element_type=f32)
                a = jnp.where(same_group[w], p, a)
            v_ref = (v01_ref, v23_ref)[h // 2]
            vh = v_ref[r0:r0 + U, (h % 2) * GLA_DV:(h % 2 + 1) * GLA_DV].astype(bf16)
            st = st_scr[sq, h]
            o = jnp.dot(a.astype(bf16), vh, preferred_element_type=f32)
            o = o + lax.dot_general(qe[:, ks].astype(bf16), st.astype(bf16), NT_DIMS, preferred_element_type=f32)
            o_ref[r0:r0 + U, vs] = o
            kv_t = lax.dot_general(vh, ke[:, ks].astype(bf16), TN_DIMS, preferred_element_type=f32)
            st_scr[sq, h] = dec[:, ks] * st + kv_t

    @pl.when(t_blk == pl.num_programs(1) - 1)
    def _():
        for sq in range(nseq):
            for h in range(N_GLA_HEADS):
                sfin_ref[sq, h] = st_scr[sq, h].T


def _gla(pg, loga, s0, *, batch, t_len, tb, t_valid, nseq=1):
    nt = t_len // tb
    assert batch % nseq == 0 and (nseq == 1 or nt == 1)
    rb = nseq * tb
    kern = functools.partial(_gla_body, tb=tb, t_valid=t_valid, nseq=nseq)
    state_spec = pl.BlockSpec((nseq, N_GLA_HEADS, GLA_DK, GLA_DV), lambda b, t: (b, 0, 0, 0))
    return pl.pallas_call(
        kern,
        grid=(batch // nseq, nt),
        in_specs=[
            pl.BlockSpec((None, rb, PROJ_TN), lambda b, t: (0, b * nt + t, 0)),
            pl.BlockSpec((None, rb, PROJ_TN), lambda b, t: (1, b * nt + t, 0)),
            pl.BlockSpec((None, rb, PROJ_TN), lambda b, t: (2, b * nt + t, 0)),
            pl.BlockSpec((None, rb, PROJ_TN), lambda b, t: (3, b * nt + t, 0)),
            pl.BlockSpec((rb, GLA_KEY_WIDTH), lambda b, t: (b * nt + t, 0)),
            state_spec,
        ],
        out_specs=[pl.BlockSpec((rb, GLA_WIDTH), lambda b, t: (b * nt + t, 0)), state_spec],
        out_shape=[
            jax.ShapeDtypeStruct((batch * t_len, GLA_WIDTH), f32),
            jax.ShapeDtypeStruct((batch, N_GLA_HEADS, GLA_DK, GLA_DV), f32),
        ],
        scratch_shapes=[
            pltpu.VMEM((nseq, N_GLA_HEADS, GLA_DV, GLA_DK), f32),
            pltpu.VMEM((nseq, min(GLA_SUB, tb), GLA_KEY_WIDTH), f32),
        ],
        compiler_params=_cparams("parallel", "arbitrary"),
        name="gla",
    )(pg, pg, pg, pg, loga, s0)


ATT_ROWS = ATT_BLOCK * max(DILATIONS)


assert DILATIONS == (1, 4, 16)


def _att_prompt_body(q_ref, k_ref, v_ref, o_ref, qd_scr, kd1, vd1, kd4, vd4, kd16, vd16, od_scr, lse_scr, by4_scr):
    U = ATT_BLOCK
    h = pl.program_id(1)
    n = pl.program_id(2)
    kds, vds = (kd1, kd4, kd16), (vd1, vd4, vd16)

    @pl.when(n == 0)
    def _():
        for d, kd, vd in zip(DILATIONS, kds, vds):
            L = ATT_ROWS // d
            for r in range(d):
                kd[r * (U + L):r * (U + L) + U, :] = jnp.zeros((U, ATT_HEAD_DIM), bf16)
                vd[r * (U + L):r * (U + L) + U, :] = jnp.zeros((U, ATT_HEAD_DIM), bf16)

    qi = lax.broadcasted_iota(jnp.int32, (U, 2 * U), 0)
    ki = lax.broadcasted_iota(jnp.int32, (U, 2 * U), 1)
    dist = qi - ki + U
    in_window = (dist >= 0) & (dist <= SUB_WINDOW)
    slope = jnp.exp2(jnp.zeros((U, 2 * U), f32) - (h + 1).astype(f32) * (8.0 / N_ATT_HEADS))
    first_pen = jnp.where(n == 0, NEG_INF, 0.0)

    for di, d in enumerate(DILATIONS):
        L = ATT_ROWS // d
        nu = L // U
        kd, vd = kds[di], vds[di]
        bias = jnp.where(in_window, -(slope * float(d)) * dist.astype(f32), NEG_INF)
        for r in range(d):
            base = r * (U + L)
            for src_ref, by4, dst, off, scale in ((q_ref, by4_scr.at[0], qd_scr, r * L, ATT_SCALE),
                                                  (k_ref, by4_scr.at[1], kd, base + U, None),
                                                  (v_ref, by4_scr.at[2], vd, base + U, None)):
                if d == 1:
                    x = src_ref[...]
                elif d == 4:
                    x = src_ref[pl.ds(r, L, stride=4), :]
                    by4[r * L:(r + 1) * L, :] = x
                else:
                    x = by4[pl.ds((r % 4) * (ATT_ROWS // 4) + r // 4, L, stride=4), :]
                if scale is not None:
                    x = x * scale
                dst[off:off + L, :] = x.astype(bf16)

        bias_first = bias + jnp.where(ki < U, first_pen, 0.0)
        for r in range(d):
            for u in range(nu):
                qoff = r * L + u * U
                koff = r * (U + L) + u * U
                qq = qd_scr[qoff:qoff + U, :]
                kk = kd[koff:koff + 2 * U, :]
                vv = vd[koff:koff + 2 * U, :]
                s = lax.dot_general(qq, kk, NT_DIMS, preferred_element_type=f32) + (bias_first if u == 0 else bias)
                m = jnp.max(s, axis=-1, keepdims=True)
                p = jnp.exp(s - m)
                l = jnp.sum(p, axis=-1, keepdims=True)
                o = jnp.dot(p.astype(bf16), vv, preferred_element_type=f32) / l
                rows = pl.ds(u * (U * d) + r, U, stride=d) if d > 1 else slice(u * U, (u + 1) * U)
                od_scr[di, rows, :] = o
                lse_scr[di, rows, :] = jnp.broadcast_to(m + jnp.log(l), (U, ATT_HEAD_DIM))
        for r in range(d):
            base = r * (U + L)
            kd[base:base + U, :] = kd[base + L:base + L + U, :]
            vd[base:base + U, :] = vd[base + L:base + L + U, :]

    CH = 512
    for c in range(ATT_ROWS // CH):
        rs = slice(c * CH, (c + 1) * CH)
        ls = [lse_scr[di, rs, :] for di in range(len(DILATIONS))]
        mm = jnp.maximum(jnp.maximum(ls[0], ls[1]), ls[2])
        ws = [jnp.exp(x - mm) for x in ls]
        num = ws[0] * od_scr[0, rs, :] + ws[1] * od_scr[1, rs, :] + ws[2] * od_scr[2, rs, :]
        o_ref[rs, :] = num / (ws[0] + ws[1] + ws[2])


def _att_prompt(pa_hm, *, batch, seq):
    nb = seq // ATT_ROWS
    H, E, U = N_ATT_HEADS, ATT_HEAD_DIM, ATT_BLOCK
    blk = (None, ATT_ROWS, E)
    kv_scratch = []
    for d in DILATIONS:
        kv_scratch += [pltpu.VMEM((d * U + ATT_ROWS, E), bf16)] * 2
    return pl.pallas_call(
        _att_prompt_body,
        grid=(batch, H, nb),
        in_specs=[
            pl.BlockSpec(blk, lambda b, h, n: (h, b * nb + n, 0)),
            pl.BlockSpec(blk, lambda b, h, n: (H + h, b * nb + n, 0)),
            pl.BlockSpec(blk, lambda b, h, n: (2 * H + h, b * nb + n, 0)),
        ],
        out_specs=pl.BlockSpec(blk, lambda b, h, n: (h, b * nb + n, 0)),
        out_shape=jax.ShapeDtypeStruct((H, batch * seq, E), f32),
        scratch_shapes=[pltpu.VMEM((ATT_ROWS, E), bf16)] + kv_scratch + [
            pltpu.VMEM((len(DILATIONS), ATT_ROWS, E), f32),
            pltpu.VMEM((len(DILATIONS), ATT_ROWS, E), f32),
            pltpu.VMEM((3, ATT_ROWS, E), f32),
        ],
        compiler_params=_cparams("parallel", "parallel", "arbitrary"),
        name="att_prompt",
    )(pa_hm, pa_hm, pa_hm)


SAMPLE_PAD = 16


def _att_sample_body(qkv_ref, k1_ref, v1_ref, k4_ref, v4_ref, k16_ref, v16_ref, o_ref, *, t_new):
    J = SUB_WINDOW
    hidx = lax.broadcasted_iota(jnp.int32, (N_ATT_HEADS, 1), 0)
    slope = jnp.exp2((hidx + 1).astype(f32) * (-8.0 / N_ATT_HEADS))
    slot = lax.broadcasted_iota(jnp.int32, (J, N_ATT_HEADS, 1), 0)
    q = [qkv_ref[i, 0] * ATT_SCALE for i in range(t_new)]
    kn = [qkv_ref[i, 1] for i in range(t_new)]
    vn = [qkv_ref[i, 2] for i in range(t_new)]
    far_bias = {dil: (slope * float(dil))[None] * (J - slot).astype(f32) for dil in DILATIONS if dil > 1}
    for i in range(t_new):
        s_new = [jnp.sum(q[i] * kn[n], axis=-1, keepdims=True) for n in range(i + 1)]
        outs, lses = [], []
        for dil in DILATIONS:
            if dil == 1:
                kc, vc = k1_ref[...], v1_ref[...]
                bias = slope[None] * (J + i - slot).astype(f32)
                news = [(s_new[n] - slope * float(i - n), vn[n]) for n in range(i + 1)]
            else:
                kres_ref, vres_ref = (k4_ref, v4_ref) if dil == 4 else (k16_ref, v16_ref)
                kc, vc = kres_ref[:, i], vres_ref[:, i]
                bias = far_bias[dil]
                news = [(s_new[i], vn[i])]
            s = jnp.sum(kc * q[i][None], axis=-1, keepdims=True) - bias
            if dil == 1:
                s = jnp.where(slot >= i, s, NEG_INF)
            m = jnp.max(s, axis=0)
            for sn, _ in news:
                m = jnp.maximum(m, sn)
            p = jnp.exp(s - m[None])
            l = jnp.sum(p, axis=0)
            acc = jnp.sum(p * vc, axis=0)
            for sn, v in news:
                pn = jnp.exp(sn - m)
                l = l + pn
                acc = acc + pn * v
            outs.append(acc / l)
            lses.append(m + jnp.log(l))
        mm = jnp.maximum(jnp.maximum(lses[0], lses[1]), lses[2])
        ws = [jnp.exp(x - mm) for x in lses]
        tot = ws[0] + ws[1] + ws[2]
        o_ref[i] = (ws[0] * outs[0] + ws[1] * outs[1] + ws[2] * outs[2]) / tot


def _att_sample(qkv_s, cache_k, cache_v, *, batch, t_new):
    w_cache = cache_k.shape[1]
    J = SUB_WINDOW
    he = (N_ATT_HEADS, ATT_HEAD_DIM)
    assert w_cache == max(DILATIONS) * J and t_new <= 4
    specs, views = [], []
    for dil in DILATIONS:
        rows = w_cache // dil
        last = rows // J - 1
        for c in (cache_k, cache_v):
            if dil == 1:
                views.append(c)
                specs.append(pl.BlockSpec((None, J) + he, lambda b, last=last: (b, last, 0, 0)))
            else:
                views.append(c.reshape((batch, rows, dil) + he))
                specs.append(pl.BlockSpec((None, J, 4) + he, lambda b, last=last: (b, last, 0, 0, 0)))
    kern = functools.partial(_att_sample_body, t_new=t_new)
    return pl.pallas_call(
        kern,
        grid=(batch,),
        in_specs=[pl.BlockSpec((None, t_new, 3) + he, lambda b: (b, 0, 0, 0, 0))] + specs,
        out_specs=pl.BlockSpec((None, t_new) + he, lambda b: (b, 0, 0, 0)),
        out_shape=jax.ShapeDtypeStruct((batch, t_new) + he, f32),
        compiler_params=_cparams("parallel"),
        name="att_sample",
    )(qkv_s, *views)


def _outproj_body(x_ref, go_ref, gg01_ref, gg23_ref, att_ref, gnw_ref, anw_ref, wg_ref, wa_ref, out_ref):
    parts = []
    for h in range(N_GLA_HEADS):
        vs = slice(h * GLA_DV, (h + 1) * GLA_DV)
        y = _rms(go_ref[:, vs], gnw_ref[...])
        gate = (gg01_ref, gg23_ref)[h // 2][:, (h % 2) * GLA_DV:(h % 2 + 1) * GLA_DV]
        parts.append((y * (gate * jax.nn.sigmoid(gate))).astype(bf16))
    gla_part = jnp.concatenate(parts, axis=-1)
    att = jnp.concatenate([att_ref[h] for h in range(N_ATT_HEADS)], axis=-1)
    att_part = _rms(att, anw_ref[...]).astype(bf16)
    mix = jnp.dot(gla_part, wg_ref[...], preferred_element_type=f32)
    mix = mix + jnp.dot(att_part, wa_ref[...], preferred_element_type=f32)
    out_ref[...] = x_ref[...] + mix


def _outproj(x, gla_o, gg_src, gg_tile, att_o, gla_norm_w, att_norm_w, w_out_bf, tm):
    n = x.shape[0]
    row = lambda i: (i, 0)
    const = lambda i: (0, 0)
    in_specs = [
        pl.BlockSpec((tm, D_MODEL), row),
        pl.BlockSpec((tm, GLA_WIDTH), row),
        pl.BlockSpec((None, tm, PROJ_TN), lambda i: (gg_tile, i, 0)),
        pl.BlockSpec((None, tm, PROJ_TN), lambda i: (gg_tile + 1, i, 0)),
        pl.BlockSpec((N_ATT_HEADS, tm, ATT_HEAD_DIM), lambda i: (0, i, 0)),
        pl.BlockSpec((1, GLA_DV), const),
        pl.BlockSpec((1, ATT_WIDTH), const),
        pl.BlockSpec((GLA_WIDTH, D_MODEL), lambda i: (0, 0)),
        pl.BlockSpec((ATT_WIDTH, D_MODEL), lambda i: (1, 0)),
    ]
    return pl.pallas_call(
        _outproj_body,
        grid=(n // tm,),
        in_specs=in_specs,
        out_specs=pl.BlockSpec((tm, D_MODEL), row),
        out_shape=jax.ShapeDtypeStruct((n, D_MODEL), f32),
        compiler_params=_cparams("parallel"),
        name="mixer_out_proj",
    )(x, gla_o, gg_src, gg_src, att_o, gla_norm_w.reshape(1, GLA_DV), att_norm_w.reshape(1, ATT_WIDTH),
      w_out_bf, w_out_bf)


def _mlp_body(x_ref, fw_ref, wu_ref, wd_ref, nw_ref, o_ref, *rest, emit_bf16):
    h_scr = rest[-1]
    f = pl.program_id(1)

    @pl.when(f == 0)
    def _():
        h_scr[...] = _rms(x_ref[...], fw_ref[...]).astype(bf16)
        o_ref[...] = jnp.zeros_like(o_ref)

    wu, wd = wu_ref[...], wd_ref[...]
    if emit_bf16:
        wu, wd = wu.astype(bf16), wd.astype(bf16)
        rest[0][...] = wu
        rest[1][...] = wd
    u = jnp.dot(h_scr[...], wu, preferred_element_type=f32)
    a = jnp.square(jnp.maximum(u, 0.0)).astype(bf16)
    o_ref[...] += jnp.dot(a, wd, preferred_element_type=f32)

    @pl.when(f == pl.num_programs(1) - 1)
    def _():
        o_ref[...] = _rms(x_ref[...] + o_ref[...], nw_ref[...])


def _mlp(x, ffn_norm_w, w_up, w_down, final_norm_w, tm, tf, emit_bf16=False):
    n = x.shape[0]
    assert not emit_bf16 or n == tm
    out_specs = [pl.BlockSpec((tm, D_MODEL), lambda i, f: (i, 0))]
    out_shape = [jax.ShapeDtypeStruct((n, D_MODEL), f32)]
    if emit_bf16:
        out_specs += [pl.BlockSpec((D_MODEL, tf), lambda i, f: (0, f)), pl.BlockSpec((tf, D_MODEL), lambda i, f: (f, 0))]
        out_shape += [jax.ShapeDtypeStruct((D_MODEL, D_FF), bf16), jax.ShapeDtypeStruct((D_FF, D_MODEL), bf16)]
    res = pl.pallas_call(
        functools.partial(_mlp_body, emit_bf16=emit_bf16),
        grid=(n // tm, D_FF // tf),
        in_specs=[
            pl.BlockSpec((tm, D_MODEL), lambda i, f: (i, 0)),
            pl.BlockSpec((1, D_MODEL), lambda i, f: (0, 0)),
            pl.BlockSpec((D_MODEL, tf), lambda i, f: (0, f)),
            pl.BlockSpec((tf, D_MODEL), lambda i, f: (f, 0)),
            pl.BlockSpec((1, D_MODEL), lambda i, f: (0, 0)),
        ],
        out_specs=out_specs,
        out_shape=out_shape,
        scratch_shapes=[pltpu.VMEM((tm, D_MODEL), bf16)],
        compiler_params=_cparams("parallel", "arbitrary"),
        name="mlp_final_norm",
    )(x, ffn_norm_w.reshape(1, D_MODEL), w_up, w_down, final_norm_w.reshape(1, D_MODEL))
    return res if emit_bf16 else res[0]


def kernel(x_prompt, x_sample, cache_k_win, cache_v_win, state_gla, attn_norm_w, w_in, w_gk_up, b_gk, gla_norm_w,
           att_out_norm_w, w_out, ffn_norm_w, w_up, w_down, final_norm_w):
    depth = w_in.shape[0]
    assert depth == 1, "single trunk layer"
    B, S, _ = x_prompt.shape
    Bs, Ts, _ = x_sample.shape
    w_p = min(MAX_WINDOW, S)
    assert S % (ATT_BLOCK * max(DILATIONS)) == 0 and Ts <= GLA_CHUNK

    w_in0 = jnp.swapaxes(w_in[0], 0, 1)
    w_gla_bf, w_att_bf = _split_w_in(w_in0, 256)
    wup_pad = jnp.pad(w_gk_up[0], ((0, LANES - GLA_GATE_RANK), (0, 0)))
    w_out_bf = w_out[0].astype(bf16)

    xp = x_prompt.reshape(B * S, D_MODEL)
    pg, loga, pa = _project(xp, attn_norm_w[0], w_gla_bf, w_att_bf, w_in0, wup_pad, b_gk[0], tm=1024, head_major=True)
    k_win, v_win = _window_kv(pa, batch=B, seq=S, window=w_p, tr=512)
    s0 = jnp.zeros((B, N_GLA_HEADS, GLA_DK, GLA_DV), f32)
    gla_o, gla_state_p = _gla(pg, loga, s0, batch=B, t_len=S, tb=512, t_valid=GLA_CHUNK)
    att = _att_prompt(pa, batch=B, seq=S)
    x1 = _outproj(xp, gla_o, pg, 4, att, gla_norm_w[0], att_out_norm_w[0], w_out_bf, 512)

    P = SAMPLE_PAD
    xs = x_sample.reshape(Bs * Ts, D_MODEL)
    pgs, logas, pas = _project(xs, attn_norm_w[0], w_gla_bf, w_att_bf, w_in0, wup_pad, b_gk[0], tm=Bs * Ts,
                               head_major=False)
    qkv_s = pas.reshape(Bs, Ts, 3, N_ATT_HEADS, ATT_HEAD_DIM)
    pad_t = lambda a: jnp.pad(a.reshape(a.shape[:-2] + (Bs, Ts, a.shape[-1])),
                              ((0, 0),) * (a.ndim - 2) + ((0, 0), (0, P - Ts), (0, 0))).reshape(
                                  a.shape[:-2] + (Bs * P, a.shape[-1]))
    gla_o_s, gla_state_s = _gla(pad_t(pgs), pad_t(logas), state_gla[0], batch=Bs, t_len=P, tb=P, t_valid=Ts, nseq=4)
    att_s = _att_sample(qkv_s, cache_k_win[0], cache_v_win[0], batch=Bs, t_new=Ts)
    att_s = att_s.reshape(Bs * Ts, N_ATT_HEADS, ATT_HEAD_DIM).swapaxes(0, 1)
    gla_o_s = gla_o_s.reshape(Bs, P, GLA_WIDTH)[:, :Ts].reshape(Bs * Ts, GLA_WIDTH)
    gg_s = pgs[4:]
    x1s = _outproj(xs, gla_o_s, gg_s, 0, att_s, gla_norm_w[0], att_out_norm_w[0], w_out_bf, Bs * Ts)
    y_sample, w_up_bf, w_down_bf = _mlp(x1s, ffn_norm_w[0], w_up[0], w_down[0], final_norm_w, Bs * Ts, 512,
                                        emit_bf16=True)
    y_sample = y_sample.reshape(Bs, Ts, D_MODEL)
    y_prompt = _mlp(x1, ffn_norm_w[0], w_up_bf, w_down_bf, final_norm_w, 1024, 512).reshape(B, S, D_MODEL)

    k_win_prompt = k_win.reshape(1, B, w_p, N_ATT_HEADS, ATT_HEAD_DIM)
    v_win_prompt = v_win.reshape(1, B, w_p, N_ATT_HEADS, ATT_HEAD_DIM)
    k_new_sample = qkv_s[:, :, 1][None]
    v_new_sample = qkv_s[:, :, 2][None]
    return (y_prompt, y_sample, k_win_prompt, v_win_prompt, gla_state_p[None], k_new_sample, v_new_sample,
            gla_state_s[None])
```
